```python
import math
import jax, jax.numpy as jnp
from jax import lax
import numpy as np

D_MODEL = 1024
BATCH = 2
SEQ = 8192
DEPTH = 4

A_HEADS = 4
A_QK_DIM = 64
A_V_DIM = 2 * A_QK_DIM
B_Q_HEADS = 8
B_KV_HEADS = 2
B_HEAD_DIM = 64
WINDOW = 128
BAND_BLOCK = 128
C_HEADS = 4
C_Q_RANK = 384
C_KV_RANK = 256
C_NOPE_DIM = 128
C_ROPE_DIM = 64
C_V_DIM = 128
ROPE_THETA = 10000.0
Q_BLOCK = 128
MIX_WIDTH = 512
N_BRANCH = 3
D_FF = 2816
EPS = 1e-6
NEG_INF = -1e30

A_QK_COLS = A_HEADS * 2 * A_QK_DIM
A_V_COLS = A_HEADS * A_V_DIM
B_Q_COLS = B_Q_HEADS * B_HEAD_DIM
B_KV_COLS = B_KV_HEADS * B_HEAD_DIM
IN_SPLIT_SIZES = (A_QK_COLS, A_QK_COLS, A_V_COLS, B_Q_COLS, B_KV_COLS, B_KV_COLS, C_Q_RANK, C_KV_RANK, C_ROPE_DIM)
IN_COLS = sum(IN_SPLIT_SIZES)

kernel_name = "hybrid_diff_window_mla_macaron"


def rms_norm(x, g):
    x32 = x.astype(jnp.float32)
    y = x32 * lax.rsqrt(jnp.mean(x32 * x32, axis=-1, keepdims=True) + EPS) * g.astype(jnp.float32)
    return y.astype(x.dtype)


def swiglu(h, w13, w2):
    a, g = jnp.split(h @ w13, 2, axis=-1)
    return (jax.nn.silu(a) * g) @ w2


def alibi_slopes(n):
    return 2.0 ** (-8.0 * jnp.arange(1, n + 1, dtype=jnp.float32) / n)


def rope(x, cos, sin):
    x1, x2 = jnp.split(x, 2, axis=-1)
    c = cos[None, :, None, :].astype(x.dtype)
    s = sin[None, :, None, :].astype(x.dtype)
    return jnp.concatenate([x1 * c - x2 * s, x1 * s + x2 * c], axis=-1)


def _split_cols(z):
    outs, off = [], 0
    for size in IN_SPLIT_SIZES:
        outs.append(z[..., off:off + size])
        off += size
    return outs


def _query_blocks(t):
    b, s = t.shape[:2]
    return jnp.moveaxis(t.reshape((b, s // Q_BLOCK, Q_BLOCK) + t.shape[2:]), 1, 0)


def _from_blocks(t):
    t = jnp.moveaxis(t, 0, 1)
    return t.reshape((t.shape[0], t.shape[1] * t.shape[2]) + t.shape[3:])


def diff_attention(q, k, v, lam, lam_init, subln_g, slopes):
    b, s_len = q.shape[:2]
    scale = A_QK_DIM ** -0.5
    key_pos = jnp.arange(s_len, dtype=jnp.float32)

    def block(args):
        qb, start = args
        sc = jnp.einsum('bqhmd,bkhmd->bhmqk', qb, k).astype(jnp.float32) * scale
        qpos = start + jnp.arange(Q_BLOCK, dtype=jnp.float32)
        dist = jnp.abs(qpos[:, None] - key_pos[None, :])
        sc = sc - slopes[None, :, None, None, None] * dist
        p = jax.nn.softmax(sc, axis=-1).astype(v.dtype)
        o = jnp.einsum('bhmqk,bkhe->bqhme', p, v)
        return (o[..., 0, :] - lam * o[..., 1, :]).astype(v.dtype)

    starts = jnp.arange(s_len // Q_BLOCK, dtype=jnp.float32) * Q_BLOCK
    o = _from_blocks(lax.map(block, (_query_blocks(q), starts)))
    o = rms_norm(o, subln_g) * (1.0 - lam_init)
    return o.reshape(b, s_len, A_HEADS * A_V_DIM)


def window_gqa(q, k, v, sinks, slopes):
    b, s_len = q.shape[:2]
    nb = s_len // BAND_BLOCK
    g = B_Q_HEADS // B_KV_HEADS
    qb = q.reshape(b, nb, BAND_BLOCK, B_KV_HEADS, g, B_HEAD_DIM)

    def band(t):
        tp = jnp.pad(t, ((0, 0), (BAND_BLOCK, BAND_BLOCK), (0, 0), (0, 0)))
        tp = tp.reshape(b, nb + 2, BAND_BLOCK, B_KV_HEADS, B_HEAD_DIM)
        return jnp.concatenate([tp[:, :-2], tp[:, 1:-1], tp[:, 2:]], axis=2)

    kb, vb = band(k), band(v)
    sc = jnp.einsum('bnqhgd,bnkhd->bnhgqk', qb, kb).astype(jnp.float32) * (B_HEAD_DIM ** -0.5)
    qi = jnp.arange(BAND_BLOCK)
    kj = jnp.arange(3 * BAND_BLOCK) - BAND_BLOCK
    rel = kj[None, :] - qi[:, None]
    kpos = jnp.arange(nb)[:, None] * BAND_BLOCK + kj[None, :]
    valid = (jnp.abs(rel) <= WINDOW)[None] & ((kpos >= 0) & (kpos < s_len))[:, None, :]
    bias = -slopes.reshape(B_KV_HEADS, g)[:, :, None, None] * jnp.abs(rel).astype(jnp.float32)
    sc = jnp.where(valid[None, :, None, None], sc + bias, NEG_INF)
    sink = jnp.broadcast_to(sinks.astype(jnp.float32).reshape(B_KV_HEADS, g)[None, None, :, :, None, None], sc.shape[:-1] + (1,))
    p = jax.nn.softmax(jnp.concatenate([sc, sink], axis=-1), axis=-1)[..., :-1].astype(v.dtype)
    o = jnp.einsum('bnhgqk,bnkhd->bnqhgd', p, vb)
    return o.reshape(b, s_len, B_Q_HEADS * B_HEAD_DIM)


def mla(c_q, c_kv, k_rope, q_norm_g, w_uq, kv_norm_g, w_ukv, cos, sin):
    b, s_len = c_q.shape[:2]
    q = jnp.einsum('bsr,rhe->bshe', rms_norm(c_q, q_norm_g), w_uq)
    q_nope, q_pe = q[..., :C_NOPE_DIM], rope(q[..., C_NOPE_DIM:], cos, sin)
    kv = jnp.einsum('bsr,rhe->bshe', rms_norm(c_kv, kv_norm_g), w_ukv)
    k_nope, v = kv[..., :C_NOPE_DIM], kv[..., C_NOPE_DIM:]
    k_pe = rope(k_rope[:, :, None, :], cos, sin)[:, :, 0, :]
    scale = (C_NOPE_DIM + C_ROPE_DIM) ** -0.5

    def block(args):
        qn, qp = args
        sc = (jnp.einsum('bqhd,bkhd->bhqk', qn, k_nope) + jnp.einsum('bqhd,bkd->bhqk', qp, k_pe)).astype(jnp.float32) * scale
        p = jax.nn.softmax(sc, axis=-1).astype(v.dtype)
        return jnp.einsum('bhqk,bkhd->bqhd', p, v)

    o = _from_blocks(lax.map(block, (_query_blocks(q_nope), _query_blocks(q_pe))))
    return o.reshape(b, s_len, C_HEADS * C_V_DIM)


def setup_inputs(seed: int = 0) -> dict:
    key = jax.random.key(seed)
    ks = jax.random.split(key, 24)

    def nrm(k, shape, scale):
        return jax.random.normal(k, shape, jnp.float32) * scale

    def gain(k, shape):
        return 1.0 + 0.05 * jax.random.normal(k, shape, jnp.float32)

    L, D, F = DEPTH, D_MODEL, D_FF
    return {
        "x": nrm(ks[0], (BATCH, SEQ, D), 1.0),
        "ffn1_norm": gain(ks[1], (L, D)),
        "ffn1_w13": nrm(ks[2], (L, D, 2 * F), D ** -0.5),
        "ffn1_w2": nrm(ks[3], (L, F, D), F ** -0.5),
        "mix_norm": gain(ks[4], (L, D)),
        "w_in": nrm(ks[5], (L, D, IN_COLS), D ** -0.5),
        "w_gate": nrm(ks[6], (L, D, N_BRANCH * D), D ** -0.5),
        "b_gate": nrm(ks[7], (L, N_BRANCH * D), 0.01),
        "a_lambda": nrm(ks[8], (L, 4, A_QK_DIM), 0.1),
        "a_subln": gain(ks[9], (L, A_V_DIM)),
        "b_sinks": nrm(ks[10], (L, B_Q_HEADS), 0.5),
        "c_q_norm": gain(ks[11], (L, C_Q_RANK)),
        "c_w_uq": nrm(ks[12], (L, C_Q_RANK, C_HEADS, C_NOPE_DIM + C_ROPE_DIM), C_Q_RANK ** -0.5),
        "c_kv_norm": gain(ks[13], (L, C_KV_RANK)),
        "c_w_ukv": nrm(ks[14], (L, C_KV_RANK, C_HEADS, C_NOPE_DIM + C_V_DIM), C_KV_RANK ** -0.5),
        "w_branch": nrm(ks[15], (L, N_BRANCH, MIX_WIDTH, D), MIX_WIDTH ** -0.5),
        "w_out": nrm(ks[16], (L, D, D), D ** -0.5),
        "ffn2_norm": gain(ks[17], (L, D)),
        "ffn2_w13": nrm(ks[18], (L, D, 2 * F), D ** -0.5),
        "ffn2_w2": nrm(ks[19], (L, F, D), F ** -0.5),
        "final_norm": gain(ks[20], (D,)),
    }


def reference(x, ffn1_norm, ffn1_w13, ffn1_w2, mix_norm, w_in, w_gate, b_gate, a_lambda, a_subln, b_sinks, c_q_norm, c_w_uq, c_kv_norm, c_w_ukv, w_branch, w_out, ffn2_norm, ffn2_w13, ffn2_w2, final_norm):
    b, s_len, _ = x.shape
    pos = jnp.arange(s_len, dtype=jnp.float32)
    inv_freq = ROPE_THETA ** (-jnp.arange(0, C_ROPE_DIM, 2, dtype=jnp.float32) / C_ROPE_DIM)
    ang = pos[:, None] * inv_freq[None, :]
    cos, sin = jnp.cos(ang), jnp.sin(ang)
    slopes_a = alibi_slopes(A_HEADS)
    slopes_b = alibi_slopes(B_Q_HEADS)

    for l in range(DEPTH):
        x = x + 0.5 * swiglu(rms_norm(x, ffn1_norm[l]), ffn1_w13[l], ffn1_w2[l])

        h = rms_norm(x, mix_norm[l])
        aq, ak, av, bq, bk, bv, cq, ckv, ckr = _split_cols(h @ w_in[l])

        lam_init = 0.8 - 0.6 * math.exp(-0.3 * l)
        lp = a_lambda[l].astype(jnp.float32)
        lam = jnp.exp(jnp.sum(lp[0] * lp[1])) - jnp.exp(jnp.sum(lp[2] * lp[3])) + lam_init
        oa = diff_attention(aq.reshape(b, s_len, A_HEADS, 2, A_QK_DIM),
                            ak.reshape(b, s_len, A_HEADS, 2, A_QK_DIM),
                            av.reshape(b, s_len, A_HEADS, A_V_DIM),
                            lam, lam_init, a_subln[l], slopes_a)
        ob = window_gqa(bq.reshape(b, s_len, B_Q_HEADS, B_HEAD_DIM),
                        bk.reshape(b, s_len, B_KV_HEADS, B_HEAD_DIM),
                        bv.reshape(b, s_len, B_KV_HEADS, B_HEAD_DIM),
                        b_sinks[l], slopes_b)
        oc = mla(cq, ckv, ckr, c_q_norm[l], c_w_uq[l], c_kv_norm[l], c_w_ukv[l], cos, sin)

        branches = jnp.stack([oa, ob, oc], axis=2)
        proj = jnp.einsum('bsnw,nwd->bsnd', branches, w_branch[l])
        gates = jax.nn.sigmoid(h @ w_gate[l] + b_gate[l]).reshape(b, s_len, N_BRANCH, D_MODEL)
        x = x + jnp.sum(gates * proj, axis=2) @ w_out[l]

        x = x + 0.5 * swiglu(rms_norm(x, ffn2_norm[l]), ffn2_w13[l], ffn2_w2[l])

    return rms_norm(x, final_norm)
```

```python
import functools
import math

import jax
import jax.numpy as jnp
from jax import lax
from jax.experimental import pallas as pl
from jax.experimental.pallas import tpu as pltpu

BF16 = jnp.bfloat16
F32 = jnp.float32

EPS = 1e-6
NEG_INF = -1e30
ROPE_THETA = 10000.0

A_HEADS = 4
A_QK_DIM = 64
B_Q_HEADS = 8
B_KV_HEADS = 2
B_HEAD_DIM = 64
BAND = 128
C_HEADS = 4
C_Q_RANK = 384
C_KV_RANK = 256
C_NOPE = 128
C_ROPE = 64
MIX_WIDTH = 512
LANES = 128

VMEM_LIMIT = 52 * 1024 * 1024

TOKEN_TILE = 512
FFN_CHUNK = 1408
Q_TILE = 256
K_TILE = 512


def _params():
    return pltpu.CompilerParams(vmem_limit_bytes=VMEM_LIMIT)


def _rms(x, g):
    return x * lax.rsqrt(jnp.mean(x * x, axis=-1, keepdims=True) + EPS) * g


def _dot(a, b):
    return jnp.dot(a, b, preferred_element_type=F32)


def _dot_nt(a, b):
    return lax.dot_general(a, b, (((1,), (1,)), ((), ())), preferred_element_type=F32)


def _full(shape):
    return pl.BlockSpec(shape, lambda *_: (0,) * len(shape))


def _ffn_kernel(x_ref, g_ref, w13_ref, w2_ref, gf_ref, o_ref, *, final):
    x = x_ref[...]
    h = _rms(x, g_ref[...]).astype(BF16)
    d_ff = w2_ref.shape[0]
    acc = jnp.zeros(x.shape, F32)
    for f in range(d_ff // FFN_CHUNK):
        lo = f * FFN_CHUNK
        a = _dot(h, w13_ref[:, lo:lo + FFN_CHUNK])
        g = _dot(h, w13_ref[:, d_ff + lo:d_ff + lo + FFN_CHUNK])
        act = (a * (1.0 / (1.0 + jnp.exp(-a))) * g).astype(BF16)
        acc = acc + _dot(act, w2_ref[lo:lo + FFN_CHUNK, :])
    y = x + 0.5 * acc
    if final:
        y = _rms(y, gf_ref[...])
    o_ref[...] = y


def _ffn(x, g, w13, w2, gf, final):
    b, s, d = x.shape
    tm = TOKEN_TILE
    xspec = pl.BlockSpec((None, tm, d), lambda bi, i: (bi, i, 0))
    return pl.pallas_call(
        functools.partial(_ffn_kernel, final=final),
        grid=(b, s // tm),
        in_specs=[xspec, _full(g.shape), _full(w13.shape), _full(w2.shape), _full(gf.shape)],
        out_specs=xspec,
        out_shape=jax.ShapeDtypeStruct(x.shape, F32),
        compiler_params=_params(),
        name="ffn",
    )(x, g, w13, w2, gf)


def _mix_in_kernel(x_ref, g_ref, w_ref, cqg_ref, wq_ref, ckvg_ref, wkn_ref, wv_ref,
                   cos_ref, sin_ref,
                   qa_ref, ka_ref, vat_ref, qb_ref, kb_ref, vb_ref, qc_ref, kc_ref, vct_ref):
    h = _rms(x_ref[...], g_ref[...]).astype(BF16)
    cos2 = cos_ref[...]
    sin2 = sin_ref[...]

    za = _dot(h, w_ref[:, 0:1536])
    qa_ref[...] = (za[:, 0:512] * (A_QK_DIM ** -0.5)).astype(BF16)
    ka_ref[...] = za[:, 512:1024].astype(BF16)
    vat_ref[...] = za[:, 1024:1536].T.astype(BF16)

    zb = _dot(h, w_ref[:, 1536:2560])
    qb_ref[...] = (zb[:, 0:512] * (B_HEAD_DIM ** -0.5)).astype(BF16)
    kb_ref[...] = zb[:, 512:768].astype(BF16)
    vb_ref[...] = zb[:, 768:1024].astype(BF16)

    zc = _dot(h, w_ref[:, 2560:3328])
    c_scale = (C_NOPE + C_ROPE) ** -0.5
    cqn = _rms(zc[:, 0:C_Q_RANK], cqg_ref[...]).astype(BF16)
    q = _dot(cqn, wq_ref[...])
    for hd in range(C_HEADS):
        lo = hd * 2 * LANES
        qc_ref[:, lo:lo + LANES] = (q[:, lo:lo + LANES] * c_scale).astype(BF16)
        qp = q[:, lo + LANES:lo + 2 * LANES]
        qpr = qp * cos2 + pltpu.roll(qp, 64, 1) * sin2
        qc_ref[:, lo + LANES:lo + 2 * LANES] = (qpr * c_scale).astype(BF16)
    ckvn = _rms(zc[:, C_Q_RANK:C_Q_RANK + C_KV_RANK], ckvg_ref[...]).astype(BF16)
    kc_ref[:, 0:512] = _dot(ckvn, wkn_ref[...]).astype(BF16)
    kr = zc[:, C_Q_RANK + C_KV_RANK:]
    kc_ref[:, 512:640] = (kr * cos2 + pltpu.roll(kr, 64, 1) * sin2).astype(BF16)
    vct_ref[...] = _dot(ckvn, wv_ref[...]).T.astype(BF16)


def _mix_in(x, g, w, cqg, wq, ckvg, wkn, wv, cos2, sin2):
    b, s, d = x.shape
    tm = TOKEN_TILE
    row = lambda c: pl.BlockSpec((None, tm, c), lambda bi, i: (bi, i, 0))
    colT = pl.BlockSpec((None, 512, tm), lambda bi, i: (bi, 0, i))
    tab = pl.BlockSpec((tm, LANES), lambda bi, i: (i, 0))
    sd = lambda c: jax.ShapeDtypeStruct((b, s, c), BF16)
    sdT = jax.ShapeDtypeStruct((b, 512, s), BF16)
    return pl.pallas_call(
        _mix_in_kernel,
        grid=(b, s // tm),
        in_specs=[row(d), _full(g.shape), _full(w.shape), _full(cqg.shape), _full(wq.shape),
                  _full(ckvg.shape), _full(wkn.shape), _full(wv.shape), tab, tab],
        out_specs=[row(512), row(512), colT, row(512), row(256), row(256), row(1024), row(640), colT],
        out_shape=[sd(512), sd(512), sdT, sd(512), sd(256), sd(256), sd(1024), sd(640), sdT],
        compiler_params=_params(),
        name="mix_in",
    )(x, g, w, cqg, wq, ckvg, wkn, wv, cos2, sin2)


def _flash_loop(qq, k_tile_fn, vt_ref, m_ref, l_ref, acc_ref, bias_fn, n_kt):
    m_ref[...] = jnp.full(m_ref.shape, -jnp.inf, F32)
    l_ref[...] = jnp.zeros(l_ref.shape, F32)
    acc_ref[...] = jnp.zeros(acc_ref.shape, F32)

    def body(kt, carry):
        k0 = pl.multiple_of(kt * K_TILE, K_TILE)
        st = _dot_nt(k_tile_fn(k0), qq)
        if bias_fn is not None:
            st = st - bias_fn(k0)
        m_old = m_ref[...]
        m_new = jnp.maximum(m_old, jnp.max(st, axis=0, keepdims=True))
        p = jnp.exp(st - m_new)
        alpha = jnp.exp(m_old - m_new)
        l_ref[...] = alpha * l_ref[...] + jnp.sum(p, axis=0, keepdims=True)
        acc_ref[...] = alpha * acc_ref[...] + _dot(vt_ref[:, pl.ds(k0, K_TILE)], p.astype(BF16))
        m_ref[...] = m_new
        return carry

    lax.fori_loop(0, n_kt, body, 0)


def _attn_a_kernel(slopes_ref, q_ref, k_ref, vt_ref, lam_ref, lam_init_ref, g_ref, o_ref,
                   m_ref, l_ref, acc_ref):
    hd = pl.program_id(1)
    qi = pl.program_id(2)
    tq = q_ref.shape[0]
    q = q_ref[...]
    lane = lax.broadcasted_iota(jnp.int32, q.shape, 1)
    zero = jnp.zeros_like(q)
    qq = jnp.concatenate([jnp.where(lane < A_QK_DIM, q, zero),
                          jnp.where(lane >= A_QK_DIM, q, zero)], axis=0)
    slope = slopes_ref[hd]
    rel = (lax.broadcasted_iota(jnp.int32, (K_TILE, tq), 0)
           - lax.broadcasted_iota(jnp.int32, (K_TILE, tq), 1))

    def bias_fn(k0):
        dist = jnp.abs(rel + (k0 - qi * tq)).astype(F32) * slope
        return jnp.concatenate([dist, dist], axis=1)

    _flash_loop(qq, lambda k0: k_ref[pl.ds(k0, K_TILE), :], vt_ref, m_ref, l_ref, acc_ref,
                bias_fn, k_ref.shape[0] // K_TILE)

    lp = lam_ref[...]
    lam_init = lam_init_ref[...]
    lam = (jnp.exp(jnp.sum(lp[0:1] * lp[1:2], axis=1, keepdims=True))
           - jnp.exp(jnp.sum(lp[2:3] * lp[3:4], axis=1, keepdims=True)) + lam_init)
    inv_l = 1.0 / l_ref[...]
    acc = acc_ref[...]
    o = acc[:, :tq] * inv_l[:, :tq] - lam * (acc[:, tq:] * inv_l[:, tq:])
    o = o * lax.rsqrt(jnp.mean(o * o, axis=0, keepdims=True) + EPS) * g_ref[...]
    o = o * (1.0 - lam_init)
    o_ref[...] = o.T.astype(BF16)


def _attn_a(slopes, qa, ka, vat, lam_p, lam_init, g_col):
    b, s, _ = qa.shape
    tq = Q_TILE
    return pl.pallas_call(
        _attn_a_kernel,
        grid=(b, A_HEADS, s // tq),
        in_specs=[
            pl.BlockSpec(memory_space=pltpu.SMEM),
            pl.BlockSpec((None, tq, LANES), lambda bi, h, i: (bi, i, h)),
            pl.BlockSpec((None, s, LANES), lambda bi, h, i: (bi, 0, h)),
            pl.BlockSpec((None, LANES, s), lambda bi, h, i: (bi, h, 0)),
            _full(lam_p.shape), _full(lam_init.shape), _full(g_col.shape),
        ],
        out_specs=pl.BlockSpec((None, tq, LANES), lambda bi, h, i: (bi, i, h)),
        out_shape=jax.ShapeDtypeStruct((b, s, MIX_WIDTH), BF16),
        scratch_shapes=[pltpu.VMEM((1, 2 * tq), F32), pltpu.VMEM((1, 2 * tq), F32),
                        pltpu.VMEM((LANES, 2 * tq), F32)],
        compiler_params=_params(),
        name="attn_a",
    )(slopes, qa, ka, vat, lam_p, lam_init, g_col)


def _attn_c_kernel(q_ref, kn_ref, kpe_ref, vt_ref, o_ref, m_ref, l_ref, acc_ref):
    def k_tile(k0):
        return jnp.concatenate([kn_ref[pl.ds(k0, K_TILE), :], kpe_ref[pl.ds(k0, K_TILE), :]], axis=1)

    _flash_loop(q_ref[...], k_tile, vt_ref, m_ref, l_ref, acc_ref, None,
                kn_ref.shape[0] // K_TILE)
    o = acc_ref[...] * (1.0 / l_ref[...])
    o_ref[...] = o.T.astype(BF16)


def _attn_c(qc, kc, vct):
    b, s, _ = qc.shape
    tq = Q_TILE
    return pl.pallas_call(
        _attn_c_kernel,
        grid=(b, C_HEADS, s // tq),
        in_specs=[
            pl.BlockSpec((None, tq, 2 * LANES), lambda bi, h, i: (bi, i, h)),
            pl.BlockSpec((None, s, LANES), lambda bi, h, i: (bi, 0, h)),
            pl.BlockSpec((None, s, LANES), lambda bi, h, i: (bi, 0, C_HEADS)),
            pl.BlockSpec((None, LANES, s), lambda bi, h, i: (bi, h, 0)),
        ],
        out_specs=pl.BlockSpec((None, tq, LANES), lambda bi, h, i: (bi, i, h)),
        out_shape=jax.ShapeDtypeStruct((b, s, MIX_WIDTH), BF16),
        scratch_shapes=[pltpu.VMEM((1, tq), F32), pltpu.VMEM((1, tq), F32),
                        pltpu.VMEM((LANES, tq), F32)],
        compiler_params=_params(),
        name="attn_c",
    )(qc, kc, kc, vct)


def _attn_b_kernel(sinks_ref, q_ref, kp_ref, ko_ref, kx_ref, vp_ref, vo_ref, vx_ref, o_ref):
    n = pl.program_id(1)
    nb = pl.num_programs(1)
    qi = lax.broadcasted_iota(jnp.int32, (BAND, 3 * BAND), 0)
    kj = lax.broadcasted_iota(jnp.int32, (BAND, 3 * BAND), 1) - BAND
    absrel = jnp.abs(kj - qi)
    kpos = n * BAND + kj
    valid = (absrel <= BAND) & (kpos >= 0) & (kpos < nb * BAND)
    absrel_f = absrel.astype(F32)
    half = lax.broadcasted_iota(jnp.int32, (BAND, LANES), 1) // B_HEAD_DIM
    group = B_Q_HEADS // B_KV_HEADS

    outs = []
    for h in range(B_Q_HEADS):
        c, p, hk = h // 2, h % 2, h // group
        qh = q_ref[:, c * LANES:(c + 1) * LANES]
        qh = jnp.where(half == p, qh, jnp.zeros_like(qh))
        ksl = slice(hk * LANES, (hk + 1) * LANES)
        k3 = jnp.concatenate([kp_ref[:, ksl], ko_ref[:, ksl], kx_ref[:, ksl]], axis=0)
        v3 = jnp.concatenate([vp_ref[:, ksl], vo_ref[:, ksl], vx_ref[:, ksl]], axis=0)
        sc = _dot_nt(qh, k3)
        sc = jnp.where(valid, sc - (2.0 ** -(h + 1)) * absrel_f, NEG_INF)
        sink = sinks_ref[h]
        m = jnp.maximum(jnp.max(sc, axis=1, keepdims=True), sink)
        e = jnp.exp(sc - m)
        den = jnp.sum(e, axis=1, keepdims=True) + jnp.exp(sink - m)
        outs.append(_dot((e * (1.0 / den)).astype(BF16), v3))
    for c in range(B_Q_HEADS // 2):
        o_ref[:, c * LANES:(c + 1) * LANES] = jnp.where(half == 0, outs[2 * c], outs[2 * c + 1]).astype(BF16)


def _attn_b(sinks, qb, kb, vb):
    b, s, _ = qb.shape
    nb = s // BAND
    kv = lambda f: pl.BlockSpec((None, BAND, 2 * LANES), lambda bi, n: (bi, f(n), 0))
    prev = lambda n: jnp.maximum(n - 1, 0)
    own = lambda n: n
    nxt = lambda n: jnp.minimum(n + 1, nb - 1)
    return pl.pallas_call(
        _attn_b_kernel,
        grid=(b, nb),
        in_specs=[pl.BlockSpec(memory_space=pltpu.SMEM),
                  pl.BlockSpec((None, BAND, MIX_WIDTH), lambda bi, n: (bi, n, 0)),
                  kv(prev), kv(own), kv(nxt), kv(prev), kv(own), kv(nxt)],
        out_specs=pl.BlockSpec((None, BAND, MIX_WIDTH), lambda bi, n: (bi, n, 0)),
        out_shape=jax.ShapeDtypeStruct((b, s, MIX_WIDTH), BF16),
        compiler_params=_params(),
        name="attn_b",
    )(sinks, qb, kb, kb, kb, vb, vb, vb)


def _merge_kernel(x_ref, g_ref, oa_ref, ob_ref, oc_ref, wg_ref, bg_ref, wb_ref, wo_ref, o_ref):
    x = x_ref[...]
    d = x.shape[1]
    h = _rms(x, g_ref[...]).astype(BF16)
    merged = jnp.zeros(x.shape, F32)
    for n, br_ref in enumerate((oa_ref, ob_ref, oc_ref)):
        z = _dot(h, wg_ref[:, n * d:(n + 1) * d]) + bg_ref[:, n * d:(n + 1) * d]
        gate = 1.0 / (1.0 + jnp.exp(-z))
        merged = merged + gate * _dot(br_ref[...], wb_ref[n])
    o_ref[...] = x + _dot(merged.astype(BF16), wo_ref[...])


def _merge(x, g, oa, ob, oc, wg, bg, wb, wo):
    b, s, d = x.shape
    tm = TOKEN_TILE
    row = lambda c: pl.BlockSpec((None, tm, c), lambda bi, i: (bi, i, 0))
    return pl.pallas_call(
        _merge_kernel,
        grid=(b, s // tm),
        in_specs=[row(d), _full(g.shape), row(MIX_WIDTH), row(MIX_WIDTH), row(MIX_WIDTH),
                  _full(wg.shape), _full(bg.shape), _full(wb.shape), _full(wo.shape)],
        out_specs=row(d),
        out_shape=jax.ShapeDtypeStruct(x.shape, F32),
        compiler_params=_params(),
        name="merge",
    )(x, g, oa, ob, oc, wg, bg, wb, wo)


def _prep_w_in(w_in):
    dup = lambda w: jnp.concatenate([w[:, 0:64], w[:, 0:64], w[:, 64:128], w[:, 64:128]], axis=1)
    kr = w_in[:, 2944:3008]
    return jnp.concatenate([
        w_in[:, 0:2048], dup(w_in[:, 2048:2176]), dup(w_in[:, 2176:2304]), w_in[:, 2304:2944],
        kr, kr[:, 32:64], kr[:, 0:32]], axis=1).astype(BF16)


def _prep_w_uq(w_uq):
    pe = w_uq[:, :, C_NOPE:]
    half = C_ROPE // 2
    w = jnp.concatenate([w_uq, pe[:, :, half:], pe[:, :, :half]], axis=2)
    return w.reshape(w_uq.shape[0], C_HEADS * 2 * LANES).astype(BF16)


def kernel(x, ffn1_norm, ffn1_w13, ffn1_w2, mix_norm, w_in, w_gate, b_gate, a_lambda, a_subln,
           b_sinks, c_q_norm, c_w_uq, c_kv_norm, c_w_ukv, w_branch, w_out, ffn2_norm, ffn2_w13,
           ffn2_w2, final_norm):
    b, s, d = x.shape
    depth = w_in.shape[0]
    assert s % K_TILE == 0 and s % TOKEN_TILE == 0 and d % LANES == 0

    pos = jnp.arange(s, dtype=F32)
    inv_freq = ROPE_THETA ** (-jnp.arange(0, C_ROPE, 2, dtype=F32) / C_ROPE)
    ang = pos[:, None] * inv_freq[None, :]
    cos, sin = jnp.cos(ang), jnp.sin(ang)
    pad = jnp.zeros((s, LANES - C_ROPE), F32)
    cos2 = jnp.concatenate([cos, cos, pad], axis=1)
    sin2 = jnp.concatenate([-sin, sin, pad], axis=1)
    slopes_a = 2.0 ** (-8.0 * jnp.arange(1, A_HEADS + 1, dtype=F32) / A_HEADS)
    row = lambda v: v.reshape(1, -1)
    gf = row(final_norm)

    for l in range(depth):
        x = _ffn(x, row(ffn1_norm[l]), ffn1_w13[l].astype(BF16), ffn1_w2[l].astype(BF16), gf, False)

        wkv = c_w_ukv[l]
        qa, ka, vat, qb, kb, vb, qc, kc, vct = _mix_in(
            x, row(mix_norm[l]), _prep_w_in(w_in[l]), row(c_q_norm[l]), _prep_w_uq(c_w_uq[l]),
            row(c_kv_norm[l]),
            wkv[:, :, :C_NOPE].reshape(C_KV_RANK, -1).astype(BF16),
            wkv[:, :, C_NOPE:].reshape(C_KV_RANK, -1).astype(BF16),
            cos2, sin2)

        lam_init = jnp.full((1, 1), 0.8 - 0.6 * math.exp(-0.3 * l), F32)
        oa = _attn_a(slopes_a, qa, ka, vat, a_lambda[l], lam_init, a_subln[l].reshape(-1, 1))
        ob = _attn_b(b_sinks[l], qb, kb, vb)
        oc = _attn_c(qc, kc, vct)

        x = _merge(x, row(mix_norm[l]), oa, ob, oc, w_gate[l].astype(BF16), row(b_gate[l]),
                   w_branch[l].astype(BF16), w_out[l].astype(BF16))
        x = _ffn(x, row(ffn2_norm[l]), ffn2_w13[l].astype(BF16), ffn2_w2[l].astype(BF16), gf,
                 l == depth - 1)
    return x
```

```python
import functools
import math

import jax
import jax.numpy as jnp
from jax import lax
from jax.experimental import pallas as pl
from jax.experimental.pallas import tpu as pltpu

BF16 = jnp.bfloat16
F32 = jnp.float32

EPS = 1e-6
NEG_INF = -1e30
LOG2E = math.log2(math.e)
ROPE_THETA = 10000.0

A_HEADS = 4
A_QK_DIM = 64
B_Q_HEADS = 8
B_KV_HEADS = 2
B_HEAD_DIM = 64
BAND = 128
C_HEADS = 4
C_Q_RANK = 384
C_KV_RANK = 256
C_NOPE = 128
C_ROPE = 64
MIX_WIDTH = 512
LANES = 128

VMEM_LIMIT = 52 * 1024 * 1024

TOKEN_TILE = 512
FFN_CHUNK = 1408
Q_TILE = 512
K_TILE = 512


def _params():
    return pltpu.CompilerParams(vmem_limit_bytes=VMEM_LIMIT)


def _rms(x, g):
    return x * lax.rsqrt(jnp.mean(x * x, axis=-1, keepdims=True) + EPS) * g


def _dot(a, b):
    return jnp.dot(a, b, preferred_element_type=F32)


def _dot_nt(a, b):
    return lax.dot_general(a, b, (((1,), (1,)), ((), ())), preferred_element_type=F32)


def _full(shape):
    return pl.BlockSpec(shape, lambda *_: (0,) * len(shape))


def _ffn_kernel(x_ref, g_ref, w13_ref, w2_ref, gf_ref, o_ref, *, final):
    x = x_ref[...]
    h = _rms(x, g_ref[...]).astype(BF16)
    d_ff = w2_ref.shape[0]
    acc = jnp.zeros(x.shape, F32)
    for f in range(d_ff // FFN_CHUNK):
        lo = f * FFN_CHUNK
        a = _dot(h, w13_ref[:, lo:lo + FFN_CHUNK])
        g = _dot(h, w13_ref[:, d_ff + lo:d_ff + lo + FFN_CHUNK])
        act = (a * (1.0 / (1.0 + jnp.exp(-a))) * g).astype(BF16)
        acc = acc + _dot(act, w2_ref[lo:lo + FFN_CHUNK, :])
    y = x + 0.5 * acc
    if final:
        y = _rms(y, gf_ref[...])
    o_ref[...] = y


def _ffn(x, g, w13, w2, gf, final):
    b, s, d = x.shape
    tm = TOKEN_TILE
    xspec = pl.BlockSpec((None, tm, d), lambda bi, i: (bi, i, 0))
    return pl.pallas_call(
        functools.partial(_ffn_kernel, final=final),
        grid=(b, s // tm),
        in_specs=[xspec, _full(g.shape), _full(w13.shape), _full(w2.shape), _full(gf.shape)],
        out_specs=xspec,
        out_shape=jax.ShapeDtypeStruct(x.shape, F32),
        compiler_params=_params(),
        name="ffn",
    )(x, g, w13, w2, gf)


def _mix_in_kernel(x_ref, g_ref, w_ref, cqg_ref, wq_ref, ckvg_ref, wkn_ref, wv_ref,
                   cos_ref, sin_ref,
                   qa_ref, ka_ref, vat_ref, qb_ref, kb_ref, vb_ref, qc_ref, kc_ref, vct_ref):
    h = _rms(x_ref[...], g_ref[...]).astype(BF16)
    cos2 = cos_ref[...]
    sin2 = sin_ref[...]

    za = _dot(h, w_ref[:, 0:1536])
    qa_ref[...] = (za[:, 0:512] * (A_QK_DIM ** -0.5 * LOG2E)).astype(BF16)
    ka_ref[...] = za[:, 512:1024].astype(BF16)
    vat_ref[...] = za[:, 1024:1536].T.astype(BF16)

    zb = _dot(h, w_ref[:, 1536:2560])
    qb_ref[...] = (zb[:, 0:512] * (B_HEAD_DIM ** -0.5)).astype(BF16)
    kb_ref[...] = zb[:, 512:768].astype(BF16)
    vb_ref[...] = zb[:, 768:1024].astype(BF16)

    zc = _dot(h, w_ref[:, 2560:3328])
    c_scale = (C_NOPE + C_ROPE) ** -0.5 * LOG2E
    cqn = _rms(zc[:, 0:C_Q_RANK], cqg_ref[...]).astype(BF16)
    q = _dot(cqn, wq_ref[...])
    for hd in range(C_HEADS):
        lo = hd * 2 * LANES
        qc_ref[:, lo:lo + LANES] = (q[:, lo:lo + LANES] * c_scale).astype(BF16)
        qp = q[:, lo + LANES:lo + 2 * LANES]
        qpr = qp * cos2 + pltpu.roll(qp, 64, 1) * sin2
        qc_ref[:, lo + LANES:lo + 2 * LANES] = (qpr * c_scale).astype(BF16)
    ckvn = _rms(zc[:, C_Q_RANK:C_Q_RANK + C_KV_RANK], ckvg_ref[...]).astype(BF16)
    kc_ref[:, 0:512] = _dot(ckvn, wkn_ref[...]).astype(BF16)
    kr = zc[:, C_Q_RANK + C_KV_RANK:]
    kc_ref[:, 512:640] = (kr * cos2 + pltpu.roll(kr, 64, 1) * sin2).astype(BF16)
    vct_ref[...] = _dot(ckvn, wv_ref[...]).T.astype(BF16)


def _mix_in(x, g, w, cqg, wq, ckvg, wkn, wv, cos2, sin2):
    b, s, d = x.shape
    tm = TOKEN_TILE
    row = lambda c: pl.BlockSpec((None, tm, c), lambda bi, i: (bi, i, 0))
    colT = pl.BlockSpec((None, 512, tm), lambda bi, i: (bi, 0, i))
    tab = pl.BlockSpec((tm, LANES), lambda bi, i: (i, 0))
    sd = lambda c: jax.ShapeDtypeStruct((b, s, c), BF16)
    sdT = jax.ShapeDtypeStruct((b, 512, s), BF16)
    return pl.pallas_call(
        _mix_in_kernel,
        grid=(b, s // tm),
        in_specs=[row(d), _full(g.shape), _full(w.shape), _full(cqg.shape), _full(wq.shape),
                  _full(ckvg.shape), _full(wkn.shape), _full(wv.shape), tab, tab],
        out_specs=[row(512), row(512), colT, row(512), row(256), row(256), row(1024), row(640), colT],
        out_shape=[sd(512), sd(512), sdT, sd(512), sd(256), sd(256), sd(1024), sd(640), sdT],
        compiler_params=_params(),
        name="mix_in",
    )(x, g, w, cqg, wq, ckvg, wkn, wv, cos2, sin2)


def _tile_stats(st, vt):
    m_t = jnp.max(st, axis=0, keepdims=True)
    p = jnp.exp2(st - m_t)
    return m_t, jnp.sum(p, axis=0, keepdims=True), _dot(vt, p.astype(BF16))


def _combine(carry, m_t, l_t, pv_t):
    m, l, acc = carry
    m_new = jnp.maximum(m, m_t)
    a = jnp.exp2(m - m_new)
    b = jnp.exp2(m_t - m_new)
    return m_new, a * l + b * l_t, a * acc + b * pv_t


def _unroll(n_tiles):
    return max(u for u in (1, 2, 3, 4) if n_tiles % u == 0)


def _flash(tile_fn, n_tiles, carry):
    unroll = _unroll(n_tiles)

    def body(it, carry):
        for u in range(unroll):
            carry = _combine(carry, *tile_fn(it * unroll + u))
        return carry

    return lax.fori_loop(0, n_tiles // unroll, body, carry)


def _attn_a_kernel(slopes_ref, q_ref, k_ref, vt_ref, lam_ref, lam_init_ref, g_ref, o_ref):
    hd = pl.program_id(1)
    qi = pl.program_id(2)
    tq = q_ref.shape[0]
    q = q_ref[...]
    lane = lax.broadcasted_iota(jnp.int32, q.shape, 1)
    zero = jnp.zeros_like(q)
    qq = jnp.concatenate([jnp.where(lane < A_QK_DIM, q, zero),
                          jnp.where(lane >= A_QK_DIM, q, zero)], axis=0)
    c = slopes_ref[hd] * LOG2E

    def scores(kt):
        k0 = pl.multiple_of(kt * K_TILE, K_TILE)
        return _dot_nt(k_ref[pl.ds(k0, K_TILE), :], qq), vt_ref[:, pl.ds(k0, K_TILE)]

    st, vt = scores(qi)
    dist = jnp.abs(lax.broadcasted_iota(jnp.int32, (K_TILE, tq), 0)
                   - lax.broadcasted_iota(jnp.int32, (K_TILE, tq), 1)).astype(F32) * c
    carry = _tile_stats(st - jnp.concatenate([dist, dist], axis=1), vt)

    key_col = lax.broadcasted_iota(jnp.int32, (K_TILE, 1), 0).astype(F32) * c
    q_row = lax.broadcasted_iota(jnp.int32, (1, tq), 1).astype(F32) * c
    q_row = jnp.concatenate([q_row, q_row], axis=1)

    def off_tile(t):
        kt = t + (t >= qi).astype(jnp.int32)
        sgn = jnp.where(kt < qi, 1.0, -1.0)
        st, vt = scores(kt)
        m_t, l_t, pv_t = _tile_stats(st + key_col * sgn, vt)
        shift = (((kt - qi) * K_TILE).astype(F32) * c - q_row) * sgn
        return m_t + shift, l_t, pv_t

    _, l, acc = _flash(off_tile, k_ref.shape[0] // K_TILE - 1, carry)

    lp = lam_ref[...]
    lam_init = lam_init_ref[...]
    lam = (jnp.exp(jnp.sum(lp[0:1] * lp[1:2], axis=1, keepdims=True))
           - jnp.exp(jnp.sum(lp[2:3] * lp[3:4], axis=1, keepdims=True)) + lam_init)
    inv_l = 1.0 / l
    o = acc[:, :tq] * inv_l[:, :tq] - lam * (acc[:, tq:] * inv_l[:, tq:])
    o = o * lax.rsqrt(jnp.mean(o * o, axis=0, keepdims=True) + EPS) * g_ref[...]
    o = o * (1.0 - lam_init)
    o_ref[...] = o.T.astype(BF16)


def _attn_a(slopes, qa, ka, vat, lam_p, lam_init, g_col):
    b, s, _ = qa.shape
    tq = K_TILE
    return pl.pallas_call(
        _attn_a_kernel,
        grid=(b, A_HEADS, s // tq),
        in_specs=[
            pl.BlockSpec(memory_space=pltpu.SMEM),
            pl.BlockSpec((None, tq, LANES), lambda bi, h, i: (bi, i, h)),
            pl.BlockSpec((None, s, LANES), lambda bi, h, i: (bi, 0, h)),
            pl.BlockSpec((None, LANES, s), lambda bi, h, i: (bi, h, 0)),
            _full(lam_p.shape), _full(lam_init.shape), _full(g_col.shape),
        ],
        out_specs=pl.BlockSpec((None, tq, LANES), lambda bi, h, i: (bi, i, h)),
        out_shape=jax.ShapeDtypeStruct((b, s, MIX_WIDTH), BF16),
        compiler_params=_params(),
        name="attn_a",
    )(slopes, qa, ka, vat, lam_p, lam_init, g_col)


def _attn_c_kernel(q_ref, kn_ref, kpe_ref, vt_ref, o_ref):
    q = q_ref[...]
    tq = q.shape[0]

    def tile(kt):
        k0 = pl.multiple_of(kt * K_TILE, K_TILE)
        k = jnp.concatenate([kn_ref[pl.ds(k0, K_TILE), :], kpe_ref[pl.ds(k0, K_TILE), :]], axis=1)
        return _tile_stats(_dot_nt(k, q), vt_ref[:, pl.ds(k0, K_TILE)])

    init = (jnp.full((1, tq), -jnp.inf, F32), jnp.zeros((1, tq), F32), jnp.zeros((LANES, tq), F32))
    _, l, acc = _flash(tile, kn_ref.shape[0] // K_TILE, init)
    o_ref[...] = (acc * (1.0 / l)).T.astype(BF16)


def _attn_c(qc, kc, vct):
    b, s, _ = qc.shape
    tq = Q_TILE
    return pl.pallas_call(
        _attn_c_kernel,
        grid=(b, C_HEADS, s // tq),
        in_specs=[
            pl.BlockSpec((None, tq, 2 * LANES), lambda bi, h, i: (bi, i, h)),
            pl.BlockSpec((None, s, LANES), lambda bi, h, i: (bi, 0, h)),
            pl.BlockSpec((None, s, LANES), lambda bi, h, i: (bi, 0, C_HEADS)),
            pl.BlockSpec((None, LANES, s), lambda bi, h, i: (bi, h, 0)),
        ],
        out_specs=pl.BlockSpec((None, tq, LANES), lambda bi, h, i: (bi, i, h)),
        out_shape=jax.ShapeDtypeStruct((b, s, MIX_WIDTH), BF16),
        compiler_params=_params(),
        name="attn_c",
    )(qc, kc, kc, vct)


def _attn_b_kernel(sinks_ref, q_ref, kp_ref, ko_ref, kx_ref, vp_ref, vo_ref, vx_ref, o_ref):
    n = pl.program_id(1)
    nb = pl.num_programs(1)
    qi = lax.broadcasted_iota(jnp.int32, (BAND, 3 * BAND), 0)
    kj = lax.broadcasted_iota(jnp.int32, (BAND, 3 * BAND), 1) - BAND
    absrel = jnp.abs(kj - qi)
    kpos = n * BAND + kj
    valid = (absrel <= BAND) & (kpos >= 0) & (kpos < nb * BAND)
    absrel_f = absrel.astype(F32)
    half = lax.broadcasted_iota(jnp.int32, (BAND, LANES), 1) // B_HEAD_DIM
    group = B_Q_HEADS // B_KV_HEADS

    outs = []
    for h in range(B_Q_HEADS):
        c, p, hk = h // 2, h % 2, h // group
        qh = q_ref[:, c * LANES:(c + 1) * LANES]
        qh = jnp.where(half == p, qh, jnp.zeros_like(qh))
        ksl = slice(hk * LANES, (hk + 1) * LANES)
        k3 = jnp.concatenate([kp_ref[:, ksl], ko_ref[:, ksl], kx_ref[:, ksl]], axis=0)
        v3 = jnp.concatenate([vp_ref[:, ksl], vo_ref[:, ksl], vx_ref[:, ksl]], axis=0)
        sc = _dot_nt(qh, k3)
        sc = jnp.where(valid, sc - (2.0 ** -(h + 1)) * absrel_f, NEG_INF)
        sink = sinks_ref[h]
        m = jnp.maximum(jnp.max(sc, axis=1, keepdims=True), sink)
        e = jnp.exp(sc - m)
        den = jnp.sum(e, axis=1, keepdims=True) + jnp.exp(sink - m)
        outs.append(_dot((e * (1.0 / den)).astype(BF16), v3))
    for c in range(B_Q_HEADS // 2):
        o_ref[:, c * LANES:(c + 1) * LANES] = jnp.where(half == 0, outs[2 * c], outs[2 * c + 1]).astype(BF16)


def _attn_b(sinks, qb, kb, vb):
    b, s, _ = qb.shape
    nb = s // BAND
    kv = lambda f: pl.BlockSpec((None, BAND, 2 * LANES), lambda bi, n: (bi, f(n), 0))
    prev = lambda n: jnp.maximum(n - 1, 0)
    own = lambda n: n
    nxt = lambda n: jnp.minimum(n + 1, nb - 1)
    return pl.pallas_call(
        _attn_b_kernel,
        grid=(b, nb),
        in_specs=[pl.BlockSpec(memory_space=pltpu.SMEM),
                  pl.BlockSpec((None, BAND, MIX_WIDTH), lambda bi, n: (bi, n, 0)),
                  kv(prev), kv(own), kv(nxt), kv(prev), kv(own), kv(nxt)],
        out_specs=pl.BlockSpec((None, BAND, MIX_WIDTH), lambda bi, n: (bi, n, 0)),
        out_shape=jax.ShapeDtypeStruct((b, s, MIX_WIDTH), BF16),
        compiler_params=_params(),
        name="attn_b",
    )(sinks, qb, kb, kb, kb, vb, vb, vb)


def _merge_kernel(x_ref, g_ref, oa_ref, ob_ref, oc_ref, wg_ref, bg_ref, wb_ref, wo_ref, o_ref):
    x = x_ref[...]
    d = x.shape[1]
    h = _rms(x, g_ref[...]).astype(BF16)
    merged = jnp.zeros(x.shape, F32)
    for n, br_ref in enumerate((oa_ref, ob_ref, oc_ref)):
        z = _dot(h, wg_ref[:, n * d:(n + 1) * d]) + bg_ref[:, n * d:(n + 1) * d]
        gate = 1.0 / (1.0 + jnp.exp(-z))
        merged = merged + gate * _dot(br_ref[...], wb_ref[n])
    o_ref[...] = x + _dot(merged.astype(BF16), wo_ref[...])


def _merge(x, g, oa, ob, oc, wg, bg, wb, wo):
    b, s, d = x.shape
    tm = TOKEN_TILE
    row = lambda c: pl.BlockSpec((None, tm, c), lambda bi, i: (bi, i, 0))
    return pl.pallas_call(
        _merge_kernel,
        grid=(b, s // tm),
        in_specs=[row(d), _full(g.shape), row(MIX_WIDTH), row(MIX_WIDTH), row(MIX_WIDTH),
                  _full(wg.shape), _full(bg.shape), _full(wb.shape), _full(wo.shape)],
        out_specs=row(d),
        out_shape=jax.ShapeDtypeStruct(x.shape, F32),
        compiler_params=_params(),
        name="merge",
    )(x, g, oa, ob, oc, wg, bg, wb, wo)


def _prep_w_in(w_in):
    dup = lambda w: jnp.concatenate([w[:, 0:64], w[:, 0:64], w[:, 64:128], w[:, 64:128]], axis=1)
    kr = w_in[:, 2944:3008]
    return jnp.concatenate([
        w_in[:, 0:2048], dup(w_in[:, 2048:2176]), dup(w_in[:, 2176:2304]), w_in[:, 2304:2944],
        kr, kr[:, 32:64], kr[:, 0:32]], axis=1).astype(BF16)


def _prep_w_uq(w_uq):
    pe = w_uq[:, :, C_NOPE:]
    half = C_ROPE // 2
    w = jnp.concatenate([w_uq, pe[:, :, half:], pe[:, :, :half]], axis=2)
    return w.reshape(w_uq.shape[0], C_HEADS * 2 * LANES).astype(BF16)


def kernel(x, ffn1_norm, ffn1_w13, ffn1_w2, mix_norm, w_in, w_gate, b_gate, a_lambda, a_subln,
           b_sinks, c_q_norm, c_w_uq, c_kv_norm, c_w_ukv, w_branch, w_out, ffn2_norm, ffn2_w13,
           ffn2_w2, final_norm):
    b, s, d = x.shape
    depth = w_in.shape[0]
    assert s % K_TILE == 0 and s % TOKEN_TILE == 0 and d % LANES == 0

    pos = jnp.arange(s, dtype=F32)
    inv_freq = ROPE_THETA ** (-jnp.arange(0, C_ROPE, 2, dtype=F32) / C_ROPE)
    ang = pos[:, None] * inv_freq[None, :]
    cos, sin = jnp.cos(ang), jnp.sin(ang)
    pad = jnp.zeros((s, LANES - C_ROPE), F32)
    cos2 = jnp.concatenate([cos, cos, pad], axis=1)
    sin2 = jnp.concatenate([-sin, sin, pad], axis=1)
    slopes_a = 2.0 ** (-8.0 * jnp.arange(1, A_HEADS + 1, dtype=F32) / A_HEADS)
    row = lambda v: v.reshape(1, -1)
    gf = row(final_norm)

    for l in range(depth):
        x = _ffn(x, row(ffn1_norm[l]), ffn1_w13[l].astype(BF16), ffn1_w2[l].astype(BF16), gf, False)

        wkv = c_w_ukv[l]
        qa, ka, vat, qb, kb, vb, qc, kc, vct = _mix_in(
            x, row(mix_norm[l]), _prep_w_in(w_in[l]), row(c_q_norm[l]), _prep_w_uq(c_w_uq[l]),
            row(c_kv_norm[l]),
            wkv[:, :, :C_NOPE].reshape(C_KV_RANK, -1).astype(BF16),
            wkv[:, :, C_NOPE:].reshape(C_KV_RANK, -1).astype(BF16),
            cos2, sin2)

        lam_init = jnp.full((1, 1), 0.8 - 0.6 * math.exp(-0.3 * l), F32)
        oa = _attn_a(slopes_a, qa, ka, vat, a_lambda[l], lam_init, a_subln[l].reshape(-1, 1))
        ob = _attn_b(b_sinks[l], qb, kb, vb)
        oc = _attn_c(qc, kc, vct)

        x = _merge(x, row(mix_norm[l]), oa, ob, oc, w_gate[l].astype(BF16), row(b_gate[l]),
                   w_branch[l].astype(BF16), w_out[l].astype(BF16))
        x = _ffn(x, row(ffn2_norm[l]), ffn2_w13[l].astype(BF16), ffn2_w2[l].astype(BF16), gf,
                 l == depth - 1)
    return x
```

```python
import functools
import math

import jax
import jax.numpy as jnp
from jax import lax
from jax.experimental import pallas as pl
from jax.experimental.pallas import tpu as pltpu

BF16 = jnp.bfloat16
F32 = jnp.float32

EPS = 1e-6
NEG_INF = -1e30
LOG2E = math.log2(math.e)
ROPE_THETA = 10000.0

A_HEADS = 4
A_QK_DIM = 64
B_Q_HEADS = 8
B_KV_HEADS = 2
B_HEAD_DIM = 64
BAND = 128
C_HEADS = 4
C_Q_RANK = 384
C_KV_RANK = 256
C_NOPE = 128
C_ROPE = 64
MIX_WIDTH = 512
LANES = 128

VMEM_LIMIT = 52 * 1024 * 1024

TOKEN_TILE = 512
FFN_CHUNK = 1408
Q_TILE = 512
A_Q_TILE = 256
K_TILE = 512


def _params(**flags):
    return pltpu.CompilerParams(vmem_limit_bytes=VMEM_LIMIT, flags=flags or None)


def _rms(x, g):
    return x * lax.rsqrt(jnp.mean(x * x, axis=-1, keepdims=True) + EPS) * g


def _dot(a, b):
    return jnp.dot(a, b, preferred_element_type=F32)


def _dot_nt(a, b):
    return lax.dot_general(a, b, (((1,), (1,)), ((), ())), preferred_element_type=F32)


def _full(shape):
    return pl.BlockSpec(shape, lambda *_: (0,) * len(shape))


def _ffn_kernel(x_ref, g_ref, w13_ref, w2_ref, gf_ref, o_ref, *, final):
    x = x_ref[...]
    h = _rms(x, g_ref[...]).astype(BF16)
    d_ff = w2_ref.shape[0]
    acc = jnp.zeros(x.shape, F32)
    for f in range(d_ff // FFN_CHUNK):
        lo = f * FFN_CHUNK
        a = _dot(h, w13_ref[:, lo:lo + FFN_CHUNK])
        g = _dot(h, w13_ref[:, d_ff + lo:d_ff + lo + FFN_CHUNK])
        act = (a * (1.0 / (1.0 + jnp.exp(-a))) * g).astype(BF16)
        acc = acc + _dot(act, w2_ref[lo:lo + FFN_CHUNK, :])
    y = x + 0.5 * acc
    if final:
        y = _rms(y, gf_ref[...])
    o_ref[...] = y


def _ffn(x, g, w13, w2, gf, final):
    b, s, d = x.shape
    tm = TOKEN_TILE
    xspec = pl.BlockSpec((None, tm, d), lambda bi, i: (bi, i, 0))
    return pl.pallas_call(
        functools.partial(_ffn_kernel, final=final),
        grid=(b, s // tm),
        in_specs=[xspec, _full(g.shape), _full(w13.shape), _full(w2.shape), _full(gf.shape)],
        out_specs=xspec,
        out_shape=jax.ShapeDtypeStruct(x.shape, F32),
        compiler_params=_params(),
        name="ffn",
    )(x, g, w13, w2, gf)


def _mix_in_kernel(x_ref, g_ref, w_ref, cqg_ref, wq_ref, ckvg_ref, wkn_ref, wv_ref,
                   cos_ref, sin_ref,
                   qa_ref, ka_ref, vat_ref, qb_ref, kb_ref, vb_ref, qc_ref, kc_ref, vct_ref):
    h = _rms(x_ref[...], g_ref[...]).astype(BF16)
    cos2 = cos_ref[...]
    sin2 = sin_ref[...]

    za = _dot(h, w_ref[:, 0:1536])
    qa_ref[...] = (za[:, 0:512] * (A_QK_DIM ** -0.5 * LOG2E)).astype(BF16)
    ka_ref[...] = za[:, 512:1024].astype(BF16)
    vat_ref[...] = za[:, 1024:1536].T.astype(BF16)

    zb = _dot(h, w_ref[:, 1536:2560])
    qb_ref[...] = (zb[:, 0:512] * (B_HEAD_DIM ** -0.5)).astype(BF16)
    kb_ref[...] = zb[:, 512:768].astype(BF16)
    vb_ref[...] = zb[:, 768:1024].astype(BF16)

    zc = _dot(h, w_ref[:, 2560:3328])
    c_scale = (C_NOPE + C_ROPE) ** -0.5 * LOG2E
    cqn = _rms(zc[:, 0:C_Q_RANK], cqg_ref[...]).astype(BF16)
    q = _dot(cqn, wq_ref[...])
    for hd in range(C_HEADS):
        lo = hd * 2 * LANES
        qc_ref[:, lo:lo + LANES] = (q[:, lo:lo + LANES] * c_scale).astype(BF16)
        qp = q[:, lo + LANES:lo + 2 * LANES]
        qpr = qp * cos2 + pltpu.roll(qp, 64, 1) * sin2
        qc_ref[:, lo + LANES:lo + 2 * LANES] = (qpr * c_scale).astype(BF16)
    ckvn = _rms(zc[:, C_Q_RANK:C_Q_RANK + C_KV_RANK], ckvg_ref[...]).astype(BF16)
    kc_ref[:, 0:512] = _dot(ckvn, wkn_ref[...]).astype(BF16)
    kr = zc[:, C_Q_RANK + C_KV_RANK:]
    kc_ref[:, 512:640] = (kr * cos2 + pltpu.roll(kr, 64, 1) * sin2).astype(BF16)
    vct_ref[...] = _dot(ckvn, wv_ref[...]).T.astype(BF16)


def _mix_in(x, g, w, cqg, wq, ckvg, wkn, wv, cos2, sin2):
    b, s, d = x.shape
    tm = TOKEN_TILE
    row = lambda c: pl.BlockSpec((None, tm, c), lambda bi, i: (bi, i, 0))
    colT = pl.BlockSpec((None, 512, tm), lambda bi, i: (bi, 0, i))
    tab = pl.BlockSpec((tm, LANES), lambda bi, i: (i, 0))
    sd = lambda c: jax.ShapeDtypeStruct((b, s, c), BF16)
    sdT = jax.ShapeDtypeStruct((b, 512, s), BF16)
    return pl.pallas_call(
        _mix_in_kernel,
        grid=(b, s // tm),
        in_specs=[row(d), _full(g.shape), _full(w.shape), _full(cqg.shape), _full(wq.shape),
                  _full(ckvg.shape), _full(wkn.shape), _full(wv.shape), tab, tab],
        out_specs=[row(512), row(512), colT, row(512), row(256), row(256), row(1024), row(640), colT],
        out_shape=[sd(512), sd(512), sdT, sd(512), sd(256), sd(256), sd(1024), sd(640), sdT],
        compiler_params=_params(),
        name="mix_in",
    )(x, g, w, cqg, wq, ckvg, wkn, wv, cos2, sin2)


def _tile_stats(st, vt):
    m_t = jnp.max(st, axis=0, keepdims=True)
    p = jnp.exp2(st - m_t)
    return m_t, jnp.sum(p, axis=0, keepdims=True), _dot(vt, p.astype(BF16))


def _combine(carry, m_t, l_t, pv_t):
    m, l, acc = carry
    m_new = jnp.maximum(m, m_t)
    a = jnp.exp2(m - m_new)
    b = jnp.exp2(m_t - m_new)
    return m_new, a * l + b * l_t, a * acc + b * pv_t


def _flash_scratch(width):
    return [pltpu.VMEM((K_TILE, width), F32), pltpu.VMEM((K_TILE, width), F32),
            pltpu.VMEM((K_TILE, width), BF16), pltpu.VMEM((K_TILE, width), BF16)]


def _flash_pipeline(n, qk_fn, v_fn, shift_fn, carry, st_refs, p_refs):
    def stage_a(t, par):
        st = qk_fn(t)
        st_refs[par][...] = st
        return jnp.max(st, axis=0, keepdims=True)

    def stage_b(par, m_t):
        p = jnp.exp2(st_refs[par][...] - m_t)
        p_refs[par][...] = p.astype(BF16)
        return jnp.sum(p, axis=0, keepdims=True)

    def stage_c(t, par, m_t, l_t, carry):
        pv_t = _dot(v_fn(t), p_refs[par][...])
        if shift_fn is not None:
            m_t = m_t + shift_fn(t)
        return _combine(carry, m_t, l_t, pv_t)

    def step(t, par, state, do_a=True, do_b=True, do_c=True):
        carry, m_cur, m_prv, l_prv = state
        m_nxt = stage_a(t + 1, 1 - par) if do_a else None
        l_cur = stage_b(par, m_cur) if do_b else None
        if do_c:
            carry = stage_c(t - 1, 1 - par, m_prv, l_prv, carry)
        return carry, m_nxt, m_cur, l_cur

    i32 = jnp.int32
    state = (carry, stage_a(i32(0), 0), None, None)
    state = step(i32(0), 0, state, do_a=n > 1, do_c=False)
    pairs = 0

    def body(j, state):
        t = 1 + 2 * j
        return step(t + 1, 0, step(t, 1, state))

    if pairs:
        state = lax.fori_loop(0, pairs, body, state)
    for t in range(1 + 2 * pairs, n - 1):
        state = step(i32(t), t % 2, state)
    if n > 1:
        state = step(i32(n - 1), (n - 1) % 2, state, do_a=False)
    return step(i32(n), n % 2, state, do_a=False, do_b=False)[0]


def _attn_a_kernel(slopes_ref, q_ref, k_ref, vt_ref, lam_ref, lam_init_ref, g_ref, o_ref,
                   st0_ref, st1_ref, p0_ref, p1_ref):
    hd = pl.program_id(1)
    qi = pl.program_id(2)
    tq = q_ref.shape[0]
    q = q_ref[...]
    lane = lax.broadcasted_iota(jnp.int32, q.shape, 1)
    zero = jnp.zeros_like(q)
    qq = jnp.concatenate([jnp.where(lane < A_QK_DIM, q, zero),
                          jnp.where(lane >= A_QK_DIM, q, zero)], axis=0)
    c = slopes_ref[hd] * LOG2E

    def scores(kt):
        return _dot_nt(k_ref[pl.ds(pl.multiple_of(kt * K_TILE, K_TILE), K_TILE), :], qq)

    def scores_v(kt):
        return vt_ref[:, pl.ds(pl.multiple_of(kt * K_TILE, K_TILE), K_TILE)]

    q0 = qi * tq
    kd = q0 // K_TILE
    q_off = q0 - kd * K_TILE
    dist = jnp.abs(lax.broadcasted_iota(jnp.int32, (K_TILE, tq), 0)
                   - lax.broadcasted_iota(jnp.int32, (K_TILE, tq), 1) - q_off).astype(F32) * c
    carry = _tile_stats(scores(kd) - jnp.concatenate([dist, dist], axis=1), scores_v(kd))

    key_col = lax.broadcasted_iota(jnp.int32, (K_TILE, 1), 0).astype(F32) * c
    q_row = lax.broadcasted_iota(jnp.int32, (1, tq), 1).astype(F32) * c
    q_row = jnp.concatenate([q_row, q_row], axis=1)

    def key_tile(t):
        kt = t + (t >= kd).astype(jnp.int32)
        return kt, jnp.where(kt < kd, 1.0, -1.0)

    def qk_fn(t):
        kt, sgn = key_tile(t)
        return scores(kt) + key_col * sgn

    def v_fn(t):
        return scores_v(key_tile(t)[0])

    def shift_fn(t):
        kt, sgn = key_tile(t)
        return ((kt * K_TILE - q0).astype(F32) * c - q_row) * sgn

    _, l, acc = _flash_pipeline(k_ref.shape[0] // K_TILE - 1, qk_fn, v_fn, shift_fn, carry,
                                (st0_ref, st1_ref), (p0_ref, p1_ref))

    lp = lam_ref[...]
    lam_init = lam_init_ref[...]
    lam = (jnp.exp(jnp.sum(lp[0:1] * lp[1:2], axis=1, keepdims=True))
           - jnp.exp(jnp.sum(lp[2:3] * lp[3:4], axis=1, keepdims=True)) + lam_init)
    inv_l = 1.0 / l
    o = acc[:, :tq] * inv_l[:, :tq] - lam * (acc[:, tq:] * inv_l[:, tq:])
    o = o * lax.rsqrt(jnp.mean(o * o, axis=0, keepdims=True) + EPS) * g_ref[...]
    o = o * (1.0 - lam_init)
    o_ref[...] = o.T.astype(BF16)


def _attn_a(slopes, qa, ka, vat, lam_p, lam_init, g_col):
    b, s, _ = qa.shape
    tq = A_Q_TILE
    return pl.pallas_call(
        _attn_a_kernel,
        grid=(b, A_HEADS, s // tq),
        in_specs=[
            pl.BlockSpec(memory_space=pltpu.SMEM),
            pl.BlockSpec((None, tq, LANES), lambda bi, h, i: (bi, i, h)),
            pl.BlockSpec((None, s, LANES), lambda bi, h, i: (bi, 0, h)),
            pl.BlockSpec((None, LANES, s), lambda bi, h, i: (bi, h, 0)),
            _full(lam_p.shape), _full(lam_init.shape), _full(g_col.shape),
        ],
        out_specs=pl.BlockSpec((None, tq, LANES), lambda bi, h, i: (bi, i, h)),
        out_shape=jax.ShapeDtypeStruct((b, s, MIX_WIDTH), BF16),
        scratch_shapes=_flash_scratch(2 * tq),
        compiler_params=_params(),
        name="attn_a",
    )(slopes, qa, ka, vat, lam_p, lam_init, g_col)


def _attn_c_kernel(q_ref, kn_ref, kpe_ref, vt_ref, o_ref, st0_ref, st1_ref, p0_ref, p1_ref):
    q = q_ref[...]
    tq = q.shape[0]

    def qk_fn(kt):
        k0 = pl.multiple_of(kt * K_TILE, K_TILE)
        k = jnp.concatenate([kn_ref[pl.ds(k0, K_TILE), :], kpe_ref[pl.ds(k0, K_TILE), :]], axis=1)
        return _dot_nt(k, q)

    def v_fn(kt):
        return vt_ref[:, pl.ds(pl.multiple_of(kt * K_TILE, K_TILE), K_TILE)]

    init = (jnp.full((1, tq), -jnp.inf, F32), jnp.zeros((1, tq), F32), jnp.zeros((LANES, tq), F32))
    _, l, acc = _flash_pipeline(kn_ref.shape[0] // K_TILE, qk_fn, v_fn, None, init,
                                (st0_ref, st1_ref), (p0_ref, p1_ref))
    o_ref[...] = (acc * (1.0 / l)).T.astype(BF16)


def _attn_c(qc, kc, vct):
    b, s, _ = qc.shape
    tq = Q_TILE
    return pl.pallas_call(
        _attn_c_kernel,
        grid=(b, C_HEADS, s // tq),
        in_specs=[
            pl.BlockSpec((None, tq, 2 * LANES), lambda bi, h, i: (bi, i, h)),
            pl.BlockSpec((None, s, LANES), lambda bi, h, i: (bi, 0, h)),
            pl.BlockSpec((None, s, LANES), lambda bi, h, i: (bi, 0, C_HEADS)),
            pl.BlockSpec((None, LANES, s), lambda bi, h, i: (bi, h, 0)),
        ],
        out_specs=pl.BlockSpec((None, tq, LANES), lambda bi, h, i: (bi, i, h)),
        out_shape=jax.ShapeDtypeStruct((b, s, MIX_WIDTH), BF16),
        scratch_shapes=_flash_scratch(tq),
        compiler_params=_params(),
        name="attn_c",
    )(qc, kc, kc, vct)


def _attn_b_kernel(sinks_ref, q_ref, kp_ref, ko_ref, kx_ref, vp_ref, vo_ref, vx_ref, o_ref):
    n = pl.program_id(1)
    nb = pl.num_programs(1)
    qi = lax.broadcasted_iota(jnp.int32, (BAND, 3 * BAND), 0)
    kj = lax.broadcasted_iota(jnp.int32, (BAND, 3 * BAND), 1) - BAND
    absrel = jnp.abs(kj - qi)
    kpos = n * BAND + kj
    valid = (absrel <= BAND) & (kpos >= 0) & (kpos < nb * BAND)
    absrel_f = absrel.astype(F32)
    half = lax.broadcasted_iota(jnp.int32, (BAND, LANES), 1) // B_HEAD_DIM
    group = B_Q_HEADS // B_KV_HEADS

    outs = []
    for h in range(B_Q_HEADS):
        c, p, hk = h // 2, h % 2, h // group
        qh = q_ref[:, c * LANES:(c + 1) * LANES]
        qh = jnp.where(half == p, qh, jnp.zeros_like(qh))
        ksl = slice(hk * LANES, (hk + 1) * LANES)
        k3 = jnp.concatenate([kp_ref[:, ksl], ko_ref[:, ksl], kx_ref[:, ksl]], axis=0)
        v3 = jnp.concatenate([vp_ref[:, ksl], vo_ref[:, ksl], vx_ref[:, ksl]], axis=0)
        sc = _dot_nt(qh, k3)
        sc = jnp.where(valid, sc - (2.0 ** -(h + 1)) * absrel_f, NEG_INF)
        sink = sinks_ref[h]
        m = jnp.maximum(jnp.max(sc, axis=1, keepdims=True), sink)
        e = jnp.exp(sc - m)
        den = jnp.sum(e, axis=1, keepdims=True) + jnp.exp(sink - m)
        outs.append(_dot((e * (1.0 / den)).astype(BF16), v3))
    for c in range(B_Q_HEADS // 2):
        o_ref[:, c * LANES:(c + 1) * LANES] = jnp.where(half == 0, outs[2 * c], outs[2 * c + 1]).astype(BF16)


def _attn_b(sinks, qb, kb, vb):
    b, s, _ = qb.shape
    nb = s // BAND
    kv = lambda f: pl.BlockSpec((None, BAND, 2 * LANES), lambda bi, n: (bi, f(n), 0))
    prev = lambda n: jnp.maximum(n - 1, 0)
    own = lambda n: n
    nxt = lambda n: jnp.minimum(n + 1, nb - 1)
    return pl.pallas_call(
        _attn_b_kernel,
        grid=(b, nb),
        in_specs=[pl.BlockSpec(memory_space=pltpu.SMEM),
                  pl.BlockSpec((None, BAND, MIX_WIDTH), lambda bi, n: (bi, n, 0)),
                  kv(prev), kv(own), kv(nxt), kv(prev), kv(own), kv(nxt)],
        out_specs=pl.BlockSpec((None, BAND, MIX_WIDTH), lambda bi, n: (bi, n, 0)),
        out_shape=jax.ShapeDtypeStruct((b, s, MIX_WIDTH), BF16),
        compiler_params=_params(),
        name="attn_b",
    )(sinks, qb, kb, kb, kb, vb, vb, vb)


def _merge_kernel(x_ref, g_ref, oa_ref, ob_ref, oc_ref, wg_ref, bg_ref, wb_ref, wo_ref, o_ref):
    x = x_ref[...]
    d = x.shape[1]
    h = _rms(x, g_ref[...]).astype(BF16)
    merged = jnp.zeros(x.shape, F32)
    for n, br_ref in enumerate((oa_ref, ob_ref, oc_ref)):
        z = _dot(h, wg_ref[:, n * d:(n + 1) * d]) + bg_ref[:, n * d:(n + 1) * d]
        gate = 1.0 / (1.0 + jnp.exp(-z))
        merged = merged + gate * _dot(br_ref[...], wb_ref[n])
    o_ref[...] = x + _dot(merged.astype(BF16), wo_ref[...])


def _merge(x, g, oa, ob, oc, wg, bg, wb, wo):
    b, s, d = x.shape
    tm = TOKEN_TILE
    row = lambda c: pl.BlockSpec((None, tm, c), lambda bi, i: (bi, i, 0))
    return pl.pallas_call(
        _merge_kernel,
        grid=(b, s // tm),
        in_specs=[row(d), _full(g.shape), row(MIX_WIDTH), row(MIX_WIDTH), row(MIX_WIDTH),
                  _full(wg.shape), _full(bg.shape), _full(wb.shape), _full(wo.shape)],
        out_specs=row(d),
        out_shape=jax.ShapeDtypeStruct(x.shape, F32),
        compiler_params=_params(),
        name="merge",
    )(x, g, oa, ob, oc, wg, bg, wb, wo)


def _prep_w_in(w_in):
    dup = lambda w: jnp.concatenate([w[:, 0:64], w[:, 0:64], w[:, 64:128], w[:, 64:128]], axis=1)
    kr = w_in[:, 2944:3008]
    return jnp.concatenate([
        w_in[:, 0:2048], dup(w_in[:, 2048:2176]), dup(w_in[:, 2176:2304]), w_in[:, 2304:2944],
        kr, kr[:, 32:64], kr[:, 0:32]], axis=1).astype(BF16)


def _prep_w_uq(w_uq):
    pe = w_uq[:, :, C_NOPE:]
    half = C_ROPE // 2
    w = jnp.concatenate([w_uq, pe[:, :, half:], pe[:, :, :half]], axis=2)
    return w.reshape(w_uq.shape[0], C_HEADS * 2 * LANES).astype(BF16)


def kernel(x, ffn1_norm, ffn1_w13, ffn1_w2, mix_norm, w_in, w_gate, b_gate, a_lambda, a_subln,
           b_sinks, c_q_norm, c_w_uq, c_kv_norm, c_w_ukv, w_branch, w_out, ffn2_norm, ffn2_w13,
           ffn2_w2, final_norm):
    b, s, d = x.shape
    depth = w_in.shape[0]
    assert s % K_TILE == 0 and s % TOKEN_TILE == 0 and d % LANES == 0

    pos = jnp.arange(s, dtype=F32)
    inv_freq = ROPE_THETA ** (-jnp.arange(0, C_ROPE, 2, dtype=F32) / C_ROPE)
    ang = pos[:, None] * inv_freq[None, :]
    cos, sin = jnp.cos(ang), jnp.sin(ang)
    pad = jnp.zeros((s, LANES - C_ROPE), F32)
    cos2 = jnp.concatenate([cos, cos, pad], axis=1)
    sin2 = jnp.concatenate([-sin, sin, pad], axis=1)
    slopes_a = 2.0 ** (-8.0 * jnp.arange(1, A_HEADS + 1, dtype=F32) / A_HEADS)
    row = lambda v: v.reshape(1, -1)
    gf = row(final_norm)

    for l in range(depth):
        x = _ffn(x, row(ffn1_norm[l]), ffn1_w13[l].astype(BF16), ffn1_w2[l].astype(BF16), gf, False)

        wkv = c_w_ukv[l]
        qa, ka, vat, qb, kb, vb, qc, kc, vct = _mix_in(
            x, row(mix_norm[l]), _prep_w_in(w_in[l]), row(c_q_norm[l]), _prep_w_uq(c_w_uq[l]),
            row(c_kv_norm[l]),
            wkv[:, :, :C_NOPE].reshape(C_KV_RANK, -1).astype(BF16),
            wkv[:, :, C_NOPE:].reshape(C_KV_RANK, -1).astype(BF16),
            cos2, sin2)

        lam_init = jnp.full((1, 1), 0.8 - 0.6 * math.exp(-0.3 * l), F32)
        oa = _attn_a(slopes_a, qa, ka, vat, a_lambda[l], lam_init, a_subln[l].reshape(-1, 1))
        ob = _attn_b(b_sinks[l], qb, kb, vb)
        oc = _attn_c(qc, kc, vct)

        x = _merge(x, row(mix_norm[l]), oa, ob, oc, w_gate[l].astype(BF16), row(b_gate[l]),
                   w_branch[l].astype(BF16), w_out[l].astype(BF16))
        x = _ffn(x, row(ffn2_norm[l]), ffn2_w13[l].astype(BF16), ffn2_w2[l].astype(BF16), gf,
                 l == depth - 1)
    return x
```

```python
import functools
import math

import jax
import jax.numpy as jnp
from jax import lax
from jax.experimental import pallas as pl
from jax.experimental.pallas import tpu as pltpu

BF16 = jnp.bfloat16
F32 = jnp.float32

EPS = 1e-6
NEG_INF = -1e30
LOG2E = math.log2(math.e)
ROPE_THETA = 10000.0

A_HEADS = 4
A_QK_DIM = 64
B_Q_HEADS = 8
B_KV_HEADS = 2
B_HEAD_DIM = 64
BAND = 128
C_HEADS = 4
C_Q_RANK = 384
C_KV_RANK = 256
C_NOPE = 128
C_ROPE = 64
MIX_WIDTH = 512
LANES = 128

VMEM_LIMIT = 52 * 1024 * 1024

TOKEN_TILE = 512
FFN_CHUNK = 1408
K_TILE = 512
Q_TILE = 512
A_Q_TILE = 256
V_ROWS = LANES + 16
ACC_ROWS = LANES + 8
ALIBI_TERMS = 3


def _params(**flags):
    return pltpu.CompilerParams(vmem_limit_bytes=VMEM_LIMIT, flags=flags or None)


def _rms(x, g):
    return x * lax.rsqrt(jnp.mean(x * x, axis=-1, keepdims=True) + EPS) * g


def _dot(a, b):
    return jnp.dot(a, b, preferred_element_type=F32)


def _dot_nt(a, b):
    return lax.dot_general(a, b, (((1,), (1,)), ((), ())), preferred_element_type=F32)


def _full(shape):
    return pl.BlockSpec(shape, lambda *_: (0,) * len(shape))


def _ffn_kernel(x_ref, g_ref, w13_ref, w2_ref, gf_ref, o_ref, *, final):
    x = x_ref[...]
    h = _rms(x, g_ref[...]).astype(BF16)
    d_ff = w2_ref.shape[0]
    acc = jnp.zeros(x.shape, F32)
    for f in range(d_ff // FFN_CHUNK):
        lo = f * FFN_CHUNK
        a = _dot(h, w13_ref[:, lo:lo + FFN_CHUNK])
        g = _dot(h, w13_ref[:, d_ff + lo:d_ff + lo + FFN_CHUNK])
        act = (a * (1.0 / (1.0 + jnp.exp(-a))) * g).astype(BF16)
        acc = acc + _dot(act, w2_ref[lo:lo + FFN_CHUNK, :])
    y = x + 0.5 * acc
    if final:
        y = _rms(y, gf_ref[...])
    o_ref[...] = y


def _ffn(x, g, w13, w2, gf, final):
    b, s, d = x.shape
    tm = TOKEN_TILE
    xspec = pl.BlockSpec((None, tm, d), lambda bi, i: (bi, i, 0))
    return pl.pallas_call(
        functools.partial(_ffn_kernel, final=final),
        grid=(b, s // tm),
        in_specs=[xspec, _full(g.shape), _full(w13.shape), _full(w2.shape), _full(gf.shape)],
        out_specs=xspec,
        out_shape=jax.ShapeDtypeStruct(x.shape, F32),
        compiler_params=_params(),
        name="ffn",
    )(x, g, w13, w2, gf)


def _store_values_t(vt_ref, v, heads):
    vt = v.T.astype(BF16)
    ones = jnp.ones((V_ROWS - LANES, vt.shape[1]), BF16)
    for hd in range(heads):
        vt_ref[hd * V_ROWS:hd * V_ROWS + LANES, :] = vt[hd * LANES:(hd + 1) * LANES]
        vt_ref[hd * V_ROWS + LANES:(hd + 1) * V_ROWS, :] = ones


def _mix_in_kernel(x_ref, g_ref, w_ref, cqg_ref, wq_ref, ckvg_ref, wkn_ref, wv_ref,
                   cos_ref, sin_ref, alibi_ref,
                   qa_ref, ka_ref, vat_ref, qb_ref, kb_ref, vb_ref, qc_ref, kc_ref, vct_ref):
    h = _rms(x_ref[...], g_ref[...]).astype(BF16)
    cos2 = cos_ref[...]
    sin2 = sin_ref[...]

    za = _dot(h, w_ref[:, 0:1536])
    qa_ref[...] = (za[:, 0:512] * (A_QK_DIM ** -0.5 * LOG2E)).astype(BF16)
    for hd in range(A_HEADS):
        ka_ref[:, 2 * hd * LANES:(2 * hd + 1) * LANES] = (
            za[:, 512 + hd * LANES:512 + (hd + 1) * LANES].astype(BF16))
        ka_ref[:, (2 * hd + 1) * LANES:(2 * hd + 2) * LANES] = alibi_ref[:, hd * LANES:(hd + 1) * LANES]
    _store_values_t(vat_ref, za[:, 1024:1536], A_HEADS)

    zb = _dot(h, w_ref[:, 1536:2560])
    qb_ref[...] = (zb[:, 0:512] * (B_HEAD_DIM ** -0.5)).astype(BF16)
    kb_ref[...] = zb[:, 512:768].astype(BF16)
    vb_ref[...] = zb[:, 768:1024].astype(BF16)

    zc = _dot(h, w_ref[:, 2560:3328])
    c_scale = (C_NOPE + C_ROPE) ** -0.5 * LOG2E
    cqn = _rms(zc[:, 0:C_Q_RANK], cqg_ref[...]).astype(BF16)
    q = _dot(cqn, wq_ref[...])
    for hd in range(C_HEADS):
        lo = hd * 2 * LANES
        qc_ref[:, lo:lo + LANES] = (q[:, lo:lo + LANES] * c_scale).astype(BF16)
        qp = q[:, lo + LANES:lo + 2 * LANES]
        qpr = qp * cos2 + pltpu.roll(qp, 64, 1) * sin2
        qc_ref[:, lo + LANES:lo + 2 * LANES] = (qpr * c_scale).astype(BF16)
    ckvn = _rms(zc[:, C_Q_RANK:C_Q_RANK + C_KV_RANK], ckvg_ref[...]).astype(BF16)
    kc_ref[:, 0:512] = _dot(ckvn, wkn_ref[...]).astype(BF16)
    kr = zc[:, C_Q_RANK + C_KV_RANK:]
    kc_ref[:, 512:640] = (kr * cos2 + pltpu.roll(kr, 64, 1) * sin2).astype(BF16)
    _store_values_t(vct_ref, _dot(ckvn, wv_ref[...]), C_HEADS)


def _mix_in(x, g, w, cqg, wq, ckvg, wkn, wv, cos2, sin2, alibi):
    b, s, d = x.shape
    tm = TOKEN_TILE
    row = lambda c: pl.BlockSpec((None, tm, c), lambda bi, i: (bi, i, 0))
    colT = pl.BlockSpec((None, 4 * V_ROWS, tm), lambda bi, i: (bi, 0, i))
    tab = pl.BlockSpec((tm, LANES), lambda bi, i: (i, 0))
    sd = lambda c: jax.ShapeDtypeStruct((b, s, c), BF16)
    sdT = jax.ShapeDtypeStruct((b, 4 * V_ROWS, s), BF16)
    return pl.pallas_call(
        _mix_in_kernel,
        grid=(b, s // tm),
        in_specs=[row(d), _full(g.shape), _full(w.shape), _full(cqg.shape), _full(wq.shape),
                  _full(ckvg.shape), _full(wkn.shape), _full(wv.shape), tab, tab, _full(alibi.shape)],
        out_specs=[row(512), row(1024), colT, row(512), row(256), row(256), row(1024), row(640), colT],
        out_shape=[sd(512), sd(1024), sdT, sd(512), sd(256), sd(256), sd(1024), sd(640), sdT],
        compiler_params=_params(),
        name="mix_in",
    )(x, g, w, cqg, wq, ckvg, wkn, wv, cos2, sin2, alibi)


def _value_product(vt, p):
    return _dot(vt, p)[:ACC_ROWS]


def _tile_stats(st, vt):
    m_t = jnp.max(st, axis=0, keepdims=True)
    return m_t, _value_product(vt, jnp.exp2(st - m_t).astype(BF16))


def _combine(carry, m_t, pv_t):
    m, acc = carry
    m_new = jnp.maximum(m, m_t)
    return m_new, jnp.exp2(m - m_new) * acc + jnp.exp2(m_t - m_new) * pv_t


def _flash_scratch(width):
    return [pltpu.VMEM((K_TILE, width), F32), pltpu.VMEM((K_TILE, width), F32),
            pltpu.VMEM((K_TILE, width), BF16), pltpu.VMEM((K_TILE, width), BF16)]


def _flash_pipeline(n, qk_fn, v_fn, shift_fn, carry, st_refs, p_refs):
    def stage_a(t):
        st = qk_fn(jnp.int32(t))
        st_refs[t % 2][...] = st
        return jnp.max(st, axis=0, keepdims=True)

    def stage_b(t, m_t):
        p_refs[t % 2][...] = jnp.exp2(st_refs[t % 2][...] - m_t).astype(BF16)

    def stage_c(t, m_t, carry):
        pv_t = _value_product(v_fn(jnp.int32(t)), p_refs[t % 2][...])
        if shift_fn is not None:
            m_t = m_t + shift_fn(jnp.int32(t))
        return _combine(carry, m_t, pv_t)

    tile_max = {0: stage_a(0)}
    for t in range(n + 1):
        if t + 1 < n:
            tile_max[t + 1] = stage_a(t + 1)
        if t < n:
            stage_b(t, tile_max[t])
        if t >= 1:
            carry = stage_c(t - 1, tile_max.pop(t - 1), carry)
    return carry


def _attn_a_kernel(c_ref, q_ref, k_ref, vt_ref, lam_ref, lam_init_ref, g_ref, o_ref,
                   qq_ref, st0_ref, st1_ref, p0_ref, p1_ref):
    hd = pl.program_id(1)
    qi = pl.program_id(2)
    tq = q_ref.shape[0]
    c = c_ref[hd]

    q = q_ref[...]
    lane = lax.broadcasted_iota(jnp.int32, q.shape, 1)
    zero = jnp.zeros_like(q)
    qm = jnp.concatenate([jnp.where(lane < A_QK_DIM, q, zero),
                          jnp.where(lane >= A_QK_DIM, q, zero)], axis=0)
    sel = (lax.broadcasted_iota(jnp.int32, qm.shape, 1) < ALIBI_TERMS).astype(BF16)
    for slot, sign in enumerate((-1.0, 1.0, 0.0)):
        qq_ref[slot, :, 0:LANES] = qm
        qq_ref[slot, :, LANES:2 * LANES] = sel * sign

    def keys(kt):
        return k_ref[pl.ds(pl.multiple_of(kt * K_TILE, K_TILE), K_TILE), :]

    def values(kt):
        return vt_ref[:, pl.ds(pl.multiple_of(kt * K_TILE, K_TILE), K_TILE)]

    q0 = qi * tq
    kd = q0 // K_TILE
    q_off = q0 - kd * K_TILE
    dist = jnp.abs(lax.broadcasted_iota(jnp.int32, (K_TILE, tq), 0)
                   - lax.broadcasted_iota(jnp.int32, (K_TILE, tq), 1) - q_off).astype(F32) * c
    carry = _tile_stats(_dot_nt(keys(kd), qq_ref[2]) - jnp.concatenate([dist, dist], axis=1),
                        values(kd))

    q_row = lax.broadcasted_iota(jnp.int32, (1, tq), 1).astype(F32) * c
    q_row = jnp.concatenate([q_row, q_row], axis=1)

    def key_tile(t):
        kt = t + (t >= kd).astype(jnp.int32)
        return kt, (kt < kd).astype(jnp.int32)

    def qk_fn(t):
        kt, before = key_tile(t)
        return _dot_nt(keys(kt), qq_ref[before])

    def v_fn(t):
        return values(key_tile(t)[0])

    def shift_fn(t):
        kt, before = key_tile(t)
        sgn = (2 * before - 1).astype(F32)
        return ((kt * K_TILE - q0).astype(F32) * c - q_row) * sgn

    _, acc = _flash_pipeline(k_ref.shape[0] // K_TILE - 1, qk_fn, v_fn, shift_fn, carry,
                             (st0_ref, st1_ref), (p0_ref, p1_ref))

    lp = lam_ref[...]
    lam_init = lam_init_ref[...]
    lam = (jnp.exp(jnp.sum(lp[0:1] * lp[1:2], axis=1, keepdims=True))
           - jnp.exp(jnp.sum(lp[2:3] * lp[3:4], axis=1, keepdims=True)) + lam_init)
    inv_l = 1.0 / acc[LANES:LANES + 1]
    o = acc[:LANES, :tq] * inv_l[:, :tq] - lam * (acc[:LANES, tq:] * inv_l[:, tq:])
    o = o * lax.rsqrt(jnp.mean(o * o, axis=0, keepdims=True) + EPS) * g_ref[...]
    o = o * (1.0 - lam_init)
    o_ref[...] = o.T.astype(BF16)


def _attn_a(c_alibi, qa, ka, vat, lam_p, lam_init, g_col):
    b, s, _ = qa.shape
    tq = A_Q_TILE
    return pl.pallas_call(
        _attn_a_kernel,
        grid=(b, A_HEADS, s // tq),
        in_specs=[
            pl.BlockSpec(memory_space=pltpu.SMEM),
            pl.BlockSpec((None, tq, LANES), lambda bi, h, i: (bi, i, h)),
            pl.BlockSpec((None, s, 2 * LANES), lambda bi, h, i: (bi, 0, h)),
            pl.BlockSpec((None, V_ROWS, s), lambda bi, h, i: (bi, h, 0)),
            _full(lam_p.shape), _full(lam_init.shape), _full(g_col.shape),
        ],
        out_specs=pl.BlockSpec((None, tq, LANES), lambda bi, h, i: (bi, i, h)),
        out_shape=jax.ShapeDtypeStruct((b, s, MIX_WIDTH), BF16),
        scratch_shapes=[pltpu.VMEM((3, 2 * tq, 2 * LANES), BF16)] + _flash_scratch(2 * tq),
        compiler_params=_params(),
        name="attn_a",
    )(c_alibi, qa, ka, vat, lam_p, lam_init, g_col)


def _attn_c_kernel(q_ref, kn_ref, kpe_ref, vt_ref, o_ref, st0_ref, st1_ref, p0_ref, p1_ref):
    q = q_ref[...]
    tq = q.shape[0]

    def qk_fn(kt):
        k0 = pl.multiple_of(kt * K_TILE, K_TILE)
        k = jnp.concatenate([kn_ref[pl.ds(k0, K_TILE), :], kpe_ref[pl.ds(k0, K_TILE), :]], axis=1)
        return _dot_nt(k, q)

    def v_fn(kt):
        return vt_ref[:, pl.ds(pl.multiple_of(kt * K_TILE, K_TILE), K_TILE)]

    init = (jnp.full((1, tq), -jnp.inf, F32), jnp.zeros((ACC_ROWS, tq), F32))
    _, acc = _flash_pipeline(kn_ref.shape[0] // K_TILE, qk_fn, v_fn, None, init,
                             (st0_ref, st1_ref), (p0_ref, p1_ref))
    o_ref[...] = (acc[:LANES] * (1.0 / acc[LANES:LANES + 1])).T.astype(BF16)


def _attn_c(qc, kc, vct):
    b, s, _ = qc.shape
    tq = Q_TILE
    return pl.pallas_call(
        _attn_c_kernel,
        grid=(b, C_HEADS, s // tq),
        in_specs=[
            pl.BlockSpec((None, tq, 2 * LANES), lambda bi, h, i: (bi, i, h)),
            pl.BlockSpec((None, s, LANES), lambda bi, h, i: (bi, 0, h)),
            pl.BlockSpec((None, s, LANES), lambda bi, h, i: (bi, 0, C_HEADS)),
            pl.BlockSpec((None, V_ROWS, s), lambda bi, h, i: (bi, h, 0)),
        ],
        out_specs=pl.BlockSpec((None, tq, LANES), lambda bi, h, i: (bi, i, h)),
        out_shape=jax.ShapeDtypeStruct((b, s, MIX_WIDTH), BF16),
        scratch_shapes=_flash_scratch(tq),
        compiler_params=_params(),
        name="attn_c",
    )(qc, kc, kc, vct)


def _attn_b_kernel(sinks_ref, q_ref, kp_ref, ko_ref, kx_ref, vp_ref, vo_ref, vx_ref, o_ref):
    n = pl.program_id(1)
    nb = pl.num_programs(1)
    qi = lax.broadcasted_iota(jnp.int32, (BAND, 3 * BAND), 0)
    kj = lax.broadcasted_iota(jnp.int32, (BAND, 3 * BAND), 1) - BAND
    absrel = jnp.abs(kj - qi)
    kpos = n * BAND + kj
    valid = (absrel <= BAND) & (kpos >= 0) & (kpos < nb * BAND)
    absrel_f = absrel.astype(F32)
    half = lax.broadcasted_iota(jnp.int32, (BAND, LANES), 1) // B_HEAD_DIM
    group = B_Q_HEADS // B_KV_HEADS

    outs = []
    for h in range(B_Q_HEADS):
        c, p, hk = h // 2, h % 2, h // group
        qh = q_ref[:, c * LANES:(c + 1) * LANES]
        qh = jnp.where(half == p, qh, jnp.zeros_like(qh))
        ksl = slice(hk * LANES, (hk + 1) * LANES)
        k3 = jnp.concatenate([kp_ref[:, ksl], ko_ref[:, ksl], kx_ref[:, ksl]], axis=0)
        v3 = jnp.concatenate([vp_ref[:, ksl], vo_ref[:, ksl], vx_ref[:, ksl]], axis=0)
        sc = _dot_nt(qh, k3)
        sc = jnp.where(valid, sc - (2.0 ** -(h + 1)) * absrel_f, NEG_INF)
        sink = sinks_ref[h]
        m = jnp.maximum(jnp.max(sc, axis=1, keepdims=True), sink)
        e = jnp.exp(sc - m)
        den = jnp.sum(e, axis=1, keepdims=True) + jnp.exp(sink - m)
        outs.append(_dot((e * (1.0 / den)).astype(BF16), v3))
    for c in range(B_Q_HEADS // 2):
        o_ref[:, c * LANES:(c + 1) * LANES] = jnp.where(half == 0, outs[2 * c], outs[2 * c + 1]).astype(BF16)


def _attn_b(sinks, qb, kb, vb):
    b, s, _ = qb.shape
    nb = s // BAND
    kv = lambda f: pl.BlockSpec((None, BAND, 2 * LANES), lambda bi, n: (bi, f(n), 0))
    prev = lambda n: jnp.maximum(n - 1, 0)
    own = lambda n: n
    nxt = lambda n: jnp.minimum(n + 1, nb - 1)
    return pl.pallas_call(
        _attn_b_kernel,
        grid=(b, nb),
        in_specs=[pl.BlockSpec(memory_space=pltpu.SMEM),
                  pl.BlockSpec((None, BAND, MIX_WIDTH), lambda bi, n: (bi, n, 0)),
                  kv(prev), kv(own), kv(nxt), kv(prev), kv(own), kv(nxt)],
        out_specs=pl.BlockSpec((None, BAND, MIX_WIDTH), lambda bi, n: (bi, n, 0)),
        out_shape=jax.ShapeDtypeStruct((b, s, MIX_WIDTH), BF16),
        compiler_params=_params(),
        name="attn_b",
    )(sinks, qb, kb, kb, kb, vb, vb, vb)


def _merge_kernel(x_ref, g_ref, oa_ref, ob_ref, oc_ref, wg_ref, bg_ref, wb_ref, wo_ref, o_ref):
    x = x_ref[...]
    d = x.shape[1]
    h = _rms(x, g_ref[...]).astype(BF16)
    merged = jnp.zeros(x.shape, F32)
    for n, br_ref in enumerate((oa_ref, ob_ref, oc_ref)):
        z = _dot(h, wg_ref[:, n * d:(n + 1) * d]) + bg_ref[:, n * d:(n + 1) * d]
        gate = 1.0 / (1.0 + jnp.exp(-z))
        merged = merged + gate * _dot(br_ref[...], wb_ref[n])
    o_ref[...] = x + _dot(merged.astype(BF16), wo_ref[...])


def _merge(x, g, oa, ob, oc, wg, bg, wb, wo):
    b, s, d = x.shape
    tm = TOKEN_TILE
    row = lambda c: pl.BlockSpec((None, tm, c), lambda bi, i: (bi, i, 0))
    return pl.pallas_call(
        _merge_kernel,
        grid=(b, s // tm),
        in_specs=[row(d), _full(g.shape), row(MIX_WIDTH), row(MIX_WIDTH), row(MIX_WIDTH),
                  _full(wg.shape), _full(bg.shape), _full(wb.shape), _full(wo.shape)],
        out_specs=row(d),
        out_shape=jax.ShapeDtypeStruct(x.shape, F32),
        compiler_params=_params(),
        name="merge",
    )(x, g, oa, ob, oc, wg, bg, wb, wo)


def _prep_w_in(w_in):
    dup = lambda w: jnp.concatenate([w[:, 0:64], w[:, 0:64], w[:, 64:128], w[:, 64:128]], axis=1)
    kr = w_in[:, 2944:3008]
    return jnp.concatenate([
        w_in[:, 0:2048], dup(w_in[:, 2048:2176]), dup(w_in[:, 2176:2304]), w_in[:, 2304:2944],
        kr, kr[:, 32:64], kr[:, 0:32]], axis=1).astype(BF16)


def _prep_w_uq(w_uq):
    pe = w_uq[:, :, C_NOPE:]
    half = C_ROPE // 2
    w = jnp.concatenate([w_uq, pe[:, :, half:], pe[:, :, :half]], axis=2)
    return w.reshape(w_uq.shape[0], C_HEADS * 2 * LANES).astype(BF16)


def _alibi_key_table(c_alibi):
    term = c_alibi[None, :] * jnp.arange(K_TILE, dtype=F32)[:, None]
    pieces = []
    for _ in range(ALIBI_TERMS):
        piece = term.astype(BF16)
        pieces.append(piece)
        term = term - piece.astype(F32)
    tab = jnp.stack(pieces, axis=-1)
    tab = jnp.pad(tab, ((0, 0), (0, 0), (0, LANES - ALIBI_TERMS)))
    return tab.reshape(K_TILE, -1)


def kernel(x, ffn1_norm, ffn1_w13, ffn1_w2, mix_norm, w_in, w_gate, b_gate, a_lambda, a_subln,
           b_sinks, c_q_norm, c_w_uq, c_kv_norm, c_w_ukv, w_branch, w_out, ffn2_norm, ffn2_w13,
           ffn2_w2, final_norm):
    b, s, d = x.shape
    depth = w_in.shape[0]
    assert s % K_TILE == 0 and K_TILE == TOKEN_TILE and K_TILE % A_Q_TILE == 0 and d % LANES == 0

    pos = jnp.arange(s, dtype=F32)
    inv_freq = ROPE_THETA ** (-jnp.arange(0, C_ROPE, 2, dtype=F32) / C_ROPE)
    ang = pos[:, None] * inv_freq[None, :]
    cos, sin = jnp.cos(ang), jnp.sin(ang)
    pad = jnp.zeros((s, LANES - C_ROPE), F32)
    cos2 = jnp.concatenate([cos, cos, pad], axis=1)
    sin2 = jnp.concatenate([-sin, sin, pad], axis=1)
    c_alibi = 2.0 ** (-8.0 * jnp.arange(1, A_HEADS + 1, dtype=F32) / A_HEADS) * LOG2E
    alibi = _alibi_key_table(c_alibi)
    row = lambda v: v.reshape(1, -1)
    gf = row(final_norm)

    for l in range(depth):
        x = _ffn(x, row(ffn1_norm[l]), ffn1_w13[l].astype(BF16), ffn1_w2[l].astype(BF16), gf, False)

        wkv = c_w_ukv[l]
        qa, ka, vat, qb, kb, vb, qc, kc, vct = _mix_in(
            x, row(mix_norm[l]), _prep_w_in(w_in[l]), row(c_q_norm[l]), _prep_w_uq(c_w_uq[l]),
            row(c_kv_norm[l]),
            wkv[:, :, :C_NOPE].reshape(C_KV_RANK, -1).astype(BF16),
            wkv[:, :, C_NOPE:].reshape(C_KV_RANK, -1).astype(BF16),
            cos2, sin2, alibi)

        lam_init = jnp.full((1, 1), 0.8 - 0.6 * math.exp(-0.3 * l), F32)
        oa = _attn_a(c_alibi, qa, ka, vat, a_lambda[l], lam_init, a_subln[l].reshape(-1, 1))
        ob = _attn_b(b_sinks[l], qb, kb, vb)
        oc = _attn_c(qc, kc, vct)

        x = _merge(x, row(mix_norm[l]), oa, ob, oc, w_gate[l].astype(BF16), row(b_gate[l]),
                   w_branch[l].astype(BF16), w_out[l].astype(BF16))
        x = _ffn(x, row(ffn2_norm[l]), ffn2_w13[l].astype(BF16), ffn2_w2[l].astype(BF16), gf,
                 l == depth - 1)
    return x
```

```python
import functools
import math

import jax
import jax.numpy as jnp
from jax import lax
from jax.experimental import pallas as pl
from jax.experimental.pallas import tpu as pltpu

BF16 = jnp.bfloat16
F32 = jnp.float32

EPS = 1e-6
NEG_INF = -1e30
LOG2E = math.log2(math.e)
ROPE_THETA = 10000.0

A_HEADS = 4
A_QK_DIM = 64
B_Q_HEADS = 8
B_KV_HEADS = 2
B_HEAD_DIM = 64
BAND = 128
C_HEADS = 4
C_Q_RANK = 384
C_KV_RANK = 256
C_NOPE = 128
C_ROPE = 64
MIX_WIDTH = 512
LANES = 128

VMEM_LIMIT = 52 * 1024 * 1024

TOKEN_TILE = 512
FFN_CHUNK = 1408
K_TILE = 512
Q_TILE = 512
A_Q_TILE = 256
V_ROWS = LANES + 16
ACC_ROWS = LANES + 8
ALIBI_TERMS = 3


def _params(**flags):
    return pltpu.CompilerParams(vmem_limit_bytes=VMEM_LIMIT, flags=flags or None)


def _rms(x, g):
    return x * lax.rsqrt(jnp.mean(x * x, axis=-1, keepdims=True) + EPS) * g


def _dot(a, b):
    return jnp.dot(a, b, preferred_element_type=F32)


def _dot_nt(a, b):
    return lax.dot_general(a, b, (((1,), (1,)), ((), ())), preferred_element_type=F32)


def _full(shape):
    return pl.BlockSpec(shape, lambda *_: (0,) * len(shape))


def _ffn_kernel(x_ref, g_ref, w13_ref, w2_ref, gf_ref, o_ref, *, final):
    x = x_ref[...]
    h = _rms(x, g_ref[...]).astype(BF16)
    d_ff = w2_ref.shape[0]
    acc = jnp.zeros(x.shape, F32)
    for f in range(d_ff // FFN_CHUNK):
        lo = f * FFN_CHUNK
        a = _dot(h, w13_ref[:, lo:lo + FFN_CHUNK])
        g = _dot(h, w13_ref[:, d_ff + lo:d_ff + lo + FFN_CHUNK])
        act = (a * (1.0 / (1.0 + jnp.exp(-a))) * g).astype(BF16)
        acc = acc + _dot(act, w2_ref[lo:lo + FFN_CHUNK, :])
    y = x + 0.5 * acc
    if final:
        y = _rms(y, gf_ref[...])
    o_ref[...] = y


def _ffn(x, g, w13, w2, gf, final):
    b, s, d = x.shape
    tm = TOKEN_TILE
    xspec = pl.BlockSpec((None, tm, d), lambda bi, i: (bi, i, 0))
    return pl.pallas_call(
        functools.partial(_ffn_kernel, final=final),
        grid=(b, s // tm),
        in_specs=[xspec, _full(g.shape), _full(w13.shape), _full(w2.shape), _full(gf.shape)],
        out_specs=xspec,
        out_shape=jax.ShapeDtypeStruct(x.shape, F32),
        compiler_params=_params(),
        name="ffn",
    )(x, g, w13, w2, gf)


def _store_values_t(vt_ref, v, heads):
    vt = v.T.astype(BF16)
    ones = jnp.ones((V_ROWS - LANES, vt.shape[1]), BF16)
    for hd in range(heads):
        vt_ref[hd * V_ROWS:hd * V_ROWS + LANES, :] = vt[hd * LANES:(hd + 1) * LANES]
        vt_ref[hd * V_ROWS + LANES:(hd + 1) * V_ROWS, :] = ones


def _mix_in_kernel(x_ref, g_ref, w_ref, cqg_ref, wq_ref, ckvg_ref, wkn_ref, wv_ref,
                   cos_ref, sin_ref, alibi_ref,
                   qa_ref, ka_ref, vat_ref, qb_ref, kb_ref, vb_ref, qc_ref, kc_ref, vct_ref):
    h = _rms(x_ref[...], g_ref[...]).astype(BF16)
    cos2 = cos_ref[...]
    sin2 = sin_ref[...]

    za = _dot(h, w_ref[:, 0:1536])
    qa_ref[...] = (za[:, 0:512] * (A_QK_DIM ** -0.5 * LOG2E)).astype(BF16)
    for hd in range(A_HEADS):
        ka_ref[:, 2 * hd * LANES:(2 * hd + 1) * LANES] = (
            za[:, 512 + hd * LANES:512 + (hd + 1) * LANES].astype(BF16))
        ka_ref[:, (2 * hd + 1) * LANES:(2 * hd + 2) * LANES] = alibi_ref[:, hd * LANES:(hd + 1) * LANES]
    _store_values_t(vat_ref, za[:, 1024:1536], A_HEADS)

    zb = _dot(h, w_ref[:, 1536:2560])
    qb_ref[...] = (zb[:, 0:512] * (B_HEAD_DIM ** -0.5)).astype(BF16)
    kb_ref[...] = zb[:, 512:768].astype(BF16)
    vb_ref[...] = zb[:, 768:1024].astype(BF16)

    zc = _dot(h, w_ref[:, 2560:3328])
    c_scale = (C_NOPE + C_ROPE) ** -0.5 * LOG2E
    cqn = _rms(zc[:, 0:C_Q_RANK], cqg_ref[...]).astype(BF16)
    q = _dot(cqn, wq_ref[...])
    for hd in range(C_HEADS):
        lo = hd * 2 * LANES
        qc_ref[:, lo:lo + LANES] = (q[:, lo:lo + LANES] * c_scale).astype(BF16)
        qp = q[:, lo + LANES:lo + 2 * LANES]
        qpr = qp * cos2 + pltpu.roll(qp, 64, 1) * sin2
        qc_ref[:, lo + LANES:lo + 2 * LANES] = (qpr * c_scale).astype(BF16)
    ckvn = _rms(zc[:, C_Q_RANK:C_Q_RANK + C_KV_RANK], ckvg_ref[...]).astype(BF16)
    kc_ref[:, 0:512] = _dot(ckvn, wkn_ref[...]).astype(BF16)
    kr = zc[:, C_Q_RANK + C_KV_RANK:]
    kc_ref[:, 512:640] = (kr * cos2 + pltpu.roll(kr, 64, 1) * sin2).astype(BF16)
    _store_values_t(vct_ref, _dot(ckvn, wv_ref[...]), C_HEADS)


def _mix_in(x, g, w, cqg, wq, ckvg, wkn, wv, cos2, sin2, alibi):
    b, s, d = x.shape
    tm = TOKEN_TILE
    row = lambda c: pl.BlockSpec((None, tm, c), lambda bi, i: (bi, i, 0))
    colT = pl.BlockSpec((None, 4 * V_ROWS, tm), lambda bi, i: (bi, 0, i))
    tab = pl.BlockSpec((tm, LANES), lambda bi, i: (i, 0))
    sd = lambda c: jax.ShapeDtypeStruct((b, s, c), BF16)
    sdT = jax.ShapeDtypeStruct((b, 4 * V_ROWS, s), BF16)
    return pl.pallas_call(
        _mix_in_kernel,
        grid=(b, s // tm),
        in_specs=[row(d), _full(g.shape), _full(w.shape), _full(cqg.shape), _full(wq.shape),
                  _full(ckvg.shape), _full(wkn.shape), _full(wv.shape), tab, tab, _full(alibi.shape)],
        out_specs=[row(512), row(1024), colT, row(512), row(256), row(256), row(1024), row(640), colT],
        out_shape=[sd(512), sd(1024), sdT, sd(512), sd(256), sd(256), sd(1024), sd(640), sdT],
        compiler_params=_params(),
        name="mix_in",
    )(x, g, w, cqg, wq, ckvg, wkn, wv, cos2, sin2, alibi)


OVERFLOW_GUARD = 1e37
PROBE_KEYS = 128


def _value_product(vt, p):
    return _dot(vt, p)[:ACC_ROWS]


def _tile_stats(st, vt):
    m_t = jnp.max(st, axis=0, keepdims=True)
    return m_t, _value_product(vt, jnp.exp2(st - m_t).astype(BF16))


def _flash_fixed_reference(n, qk_fn, v_fn, ref_fn, acc, p_refs):
    def weights(t):
        p_refs[t % 2][...] = jnp.exp2(qk_fn(t) - ref_fn(t)).astype(BF16)

    if n:
        weights(0)
    for t in range(n):
        if t + 1 < n:
            weights(t + 1)
        acc = acc + _value_product(v_fn(t), p_refs[t % 2][...])
    return acc


def _flash_running_max(n, score_fn, v_fn, width):
    def body(kt, carry):
        m, acc = carry
        m_t, pv_t = _tile_stats(score_fn(kt), v_fn(kt))
        m_new = jnp.maximum(m, m_t)
        return m_new, jnp.exp2(m - m_new) * acc + jnp.exp2(m_t - m_new) * pv_t

    init = (jnp.full((1, width), -jnp.inf, F32), jnp.zeros((ACC_ROWS, width), F32))
    return lax.fori_loop(0, n, body, init)[1]


def _denominator_overflowed(acc):
    l = acc[LANES:LANES + 1]
    return jnp.max(jnp.where(l < OVERFLOW_GUARD, 0.0, 1.0)) > 0.5


def _attn_a_kernel(c_ref, q_ref, k_ref, vt_ref, lam_ref, lam_init_ref, g_ref, o_ref,
                   qq_ref, p0_ref, p1_ref):
    hd = pl.program_id(1)
    qi = pl.program_id(2)
    tq = q_ref.shape[0]
    n_kt = k_ref.shape[0] // K_TILE
    c = c_ref[hd]

    q = q_ref[...]
    lane = lax.broadcasted_iota(jnp.int32, q.shape, 1)
    zero = jnp.zeros_like(q)
    qm = jnp.concatenate([jnp.where(lane < A_QK_DIM, q, zero),
                          jnp.where(lane >= A_QK_DIM, q, zero)], axis=0)
    sel = (lax.broadcasted_iota(jnp.int32, qm.shape, 1) < ALIBI_TERMS).astype(BF16)
    for slot, sign in enumerate((-1.0, 1.0, 0.0)):
        qq_ref[slot, :, 0:LANES] = qm
        qq_ref[slot, :, LANES:2 * LANES] = sel * sign

    def keys(kt):
        return k_ref[pl.ds(pl.multiple_of(kt * K_TILE, K_TILE), K_TILE), :]

    def values(kt):
        return vt_ref[:, pl.ds(pl.multiple_of(kt * K_TILE, K_TILE), K_TILE)]

    q0 = qi * tq
    rel = (lax.broadcasted_iota(jnp.int32, (K_TILE, tq), 0)
           - lax.broadcasted_iota(jnp.int32, (K_TILE, tq), 1))

    def biased_scores(kt):
        dist = jnp.abs(rel + (kt * K_TILE - q0)).astype(F32) * c
        return _dot_nt(keys(kt), qq_ref[2]) - jnp.concatenate([dist, dist], axis=1)

    def finish(acc):
        lp = lam_ref[...]
        lam_init = lam_init_ref[...]
        lam = (jnp.exp(jnp.sum(lp[0:1] * lp[1:2], axis=1, keepdims=True))
               - jnp.exp(jnp.sum(lp[2:3] * lp[3:4], axis=1, keepdims=True)) + lam_init)
        inv_l = 1.0 / acc[LANES:LANES + 1]
        o = acc[:LANES, :tq] * inv_l[:, :tq] - lam * (acc[:LANES, tq:] * inv_l[:, tq:])
        o = o * lax.rsqrt(jnp.mean(o * o, axis=0, keepdims=True) + EPS) * g_ref[...]
        o = o * (1.0 - lam_init)
        o_ref[...] = o.T.astype(BF16)

    own = k_ref[pl.ds(pl.multiple_of(q0, tq), tq), :]
    dist = jnp.abs(rel[:tq]).astype(F32) * c
    m0 = jnp.max(_dot_nt(own, qq_ref[2]) - jnp.concatenate([dist, dist], axis=1),
                 axis=0, keepdims=True)

    kd = q0 // K_TILE
    q_row = lax.broadcasted_iota(jnp.int32, (1, tq), 1).astype(F32) * c
    q_row = jnp.concatenate([q_row, q_row], axis=1)

    def key_tile(t):
        t = jnp.int32(t - 1)
        kt = t + (t >= kd).astype(jnp.int32)
        return kt, (kt < kd).astype(jnp.int32)

    def qk_fn(t):
        if t == 0:
            return biased_scores(kd)
        kt, before = key_tile(t)
        return _dot_nt(keys(kt), qq_ref[before])

    def v_fn(t):
        return values(kd if t == 0 else key_tile(t)[0])

    def ref_fn(t):
        if t == 0:
            return m0
        kt, before = key_tile(t)
        sgn = (2 * before - 1).astype(F32)
        return m0 - ((kt * K_TILE - q0).astype(F32) * c - q_row) * sgn

    acc = _flash_fixed_reference(n_kt, qk_fn, v_fn, ref_fn, jnp.zeros((ACC_ROWS, 2 * tq), F32),
                                 (p0_ref, p1_ref))
    finish(acc)

    @pl.when(_denominator_overflowed(acc))
    def _():
        finish(_flash_running_max(n_kt, biased_scores, values, 2 * tq))


def _attn_a(c_alibi, qa, ka, vat, lam_p, lam_init, g_col):
    b, s, _ = qa.shape
    tq = A_Q_TILE
    return pl.pallas_call(
        _attn_a_kernel,
        grid=(b, A_HEADS, s // tq),
        in_specs=[
            pl.BlockSpec(memory_space=pltpu.SMEM),
            pl.BlockSpec((None, tq, LANES), lambda bi, h, i: (bi, i, h)),
            pl.BlockSpec((None, s, 2 * LANES), lambda bi, h, i: (bi, 0, h)),
            pl.BlockSpec((None, V_ROWS, s), lambda bi, h, i: (bi, h, 0)),
            _full(lam_p.shape), _full(lam_init.shape), _full(g_col.shape),
        ],
        out_specs=pl.BlockSpec((None, tq, LANES), lambda bi, h, i: (bi, i, h)),
        out_shape=jax.ShapeDtypeStruct((b, s, MIX_WIDTH), BF16),
        scratch_shapes=[pltpu.VMEM((3, 2 * tq, 2 * LANES), BF16),
                        pltpu.VMEM((K_TILE, 2 * tq), BF16), pltpu.VMEM((K_TILE, 2 * tq), BF16)],
        compiler_params=_params(),
        name="attn_a",
    )(c_alibi, qa, ka, vat, lam_p, lam_init, g_col)


def _attn_c_kernel(q_ref, kn_ref, kpe_ref, vt_ref, o_ref, p0_ref, p1_ref):
    q = q_ref[...]
    tq = q.shape[0]
    n_kt = kn_ref.shape[0] // K_TILE

    def scores(kt):
        k0 = pl.multiple_of(kt * K_TILE, K_TILE)
        k = jnp.concatenate([kn_ref[pl.ds(k0, K_TILE), :], kpe_ref[pl.ds(k0, K_TILE), :]], axis=1)
        return _dot_nt(k, q)

    def values(kt):
        return vt_ref[:, pl.ds(pl.multiple_of(kt * K_TILE, K_TILE), K_TILE)]

    def finish(acc):
        o_ref[...] = (acc[:LANES] * (1.0 / acc[LANES:LANES + 1])).T.astype(BF16)

    probe = jnp.concatenate([kn_ref[0:PROBE_KEYS, :], kpe_ref[0:PROBE_KEYS, :]], axis=1)
    m0 = jnp.max(_dot_nt(probe, q), axis=0, keepdims=True)
    acc = _flash_fixed_reference(n_kt, lambda t: scores(jnp.int32(t)), lambda t: values(jnp.int32(t)),
                                 lambda t: m0, jnp.zeros((ACC_ROWS, tq), F32), (p0_ref, p1_ref))
    finish(acc)

    @pl.when(_denominator_overflowed(acc))
    def _():
        finish(_flash_running_max(n_kt, scores, values, tq))


def _attn_c(qc, kc, vct):
    b, s, _ = qc.shape
    tq = Q_TILE
    return pl.pallas_call(
        _attn_c_kernel,
        grid=(b, C_HEADS, s // tq),
        in_specs=[
            pl.BlockSpec((None, tq, 2 * LANES), lambda bi, h, i: (bi, i, h)),
            pl.BlockSpec((None, s, LANES), lambda bi, h, i: (bi, 0, h)),
            pl.BlockSpec((None, s, LANES), lambda bi, h, i: (bi, 0, C_HEADS)),
            pl.BlockSpec((None, V_ROWS, s), lambda bi, h, i: (bi, h, 0)),
        ],
        out_specs=pl.BlockSpec((None, tq, LANES), lambda bi, h, i: (bi, i, h)),
        out_shape=jax.ShapeDtypeStruct((b, s, MIX_WIDTH), BF16),
        scratch_shapes=[pltpu.VMEM((K_TILE, tq), BF16), pltpu.VMEM((K_TILE, tq), BF16)],
        compiler_params=_params(),
        name="attn_c",
    )(qc, kc, kc, vct)


def _attn_b_kernel(sinks_ref, q_ref, kp_ref, ko_ref, kx_ref, vp_ref, vo_ref, vx_ref, o_ref):
    n = pl.program_id(1)
    nb = pl.num_programs(1)
    qi = lax.broadcasted_iota(jnp.int32, (BAND, 3 * BAND), 0)
    kj = lax.broadcasted_iota(jnp.int32, (BAND, 3 * BAND), 1) - BAND
    absrel = jnp.abs(kj - qi)
    kpos = n * BAND + kj
    valid = (absrel <= BAND) & (kpos >= 0) & (kpos < nb * BAND)
    absrel_f = absrel.astype(F32)
    half = lax.broadcasted_iota(jnp.int32, (BAND, LANES), 1) // B_HEAD_DIM
    group = B_Q_HEADS // B_KV_HEADS

    outs = []
    for h in range(B_Q_HEADS):
        c, p, hk = h // 2, h % 2, h // group
        qh = q_ref[:, c * LANES:(c + 1) * LANES]
        qh = jnp.where(half == p, qh, jnp.zeros_like(qh))
        ksl = slice(hk * LANES, (hk + 1) * LANES)
        k3 = jnp.concatenate([kp_ref[:, ksl], ko_ref[:, ksl], kx_ref[:, ksl]], axis=0)
        v3 = jnp.concatenate([vp_ref[:, ksl], vo_ref[:, ksl], vx_ref[:, ksl]], axis=0)
        sc = _dot_nt(qh, k3)
        sc = jnp.where(valid, sc - (2.0 ** -(h + 1)) * absrel_f, NEG_INF)
        sink = sinks_ref[h]
        m = jnp.maximum(jnp.max(sc, axis=1, keepdims=True), sink)
        e = jnp.exp(sc - m)
        den = jnp.sum(e, axis=1, keepdims=True) + jnp.exp(sink - m)
        outs.append(_dot((e * (1.0 / den)).astype(BF16), v3))
    for c in range(B_Q_HEADS // 2):
        o_ref[:, c * LANES:(c + 1) * LANES] = jnp.where(half == 0, outs[2 * c], outs[2 * c + 1]).astype(BF16)


def _attn_b(sinks, qb, kb, vb):
    b, s, _ = qb.shape
    nb = s // BAND
    kv = lambda f: pl.BlockSpec((None, BAND, 2 * LANES), lambda bi, n: (bi, f(n), 0))
    prev = lambda n: jnp.maximum(n - 1, 0)
    own = lambda n: n
    nxt = lambda n: jnp.minimum(n + 1, nb - 1)
    return pl.pallas_call(
        _attn_b_kernel,
        grid=(b, nb),
        in_specs=[pl.BlockSpec(memory_space=pltpu.SMEM),
                  pl.BlockSpec((None, BAND, MIX_WIDTH), lambda bi, n: (bi, n, 0)),
                  kv(prev), kv(own), kv(nxt), kv(prev), kv(own), kv(nxt)],
        out_specs=pl.BlockSpec((None, BAND, MIX_WIDTH), lambda bi, n: (bi, n, 0)),
        out_shape=jax.ShapeDtypeStruct((b, s, MIX_WIDTH), BF16),
        compiler_params=_params(),
        name="attn_b",
    )(sinks, qb, kb, kb, kb, vb, vb, vb)


def _merge_kernel(x_ref, g_ref, oa_ref, ob_ref, oc_ref, wg_ref, bg_ref, wb_ref, wo_ref, o_ref):
    x = x_ref[...]
    d = x.shape[1]
    h = _rms(x, g_ref[...]).astype(BF16)
    merged = jnp.zeros(x.shape, F32)
    for n, br_ref in enumerate((oa_ref, ob_ref, oc_ref)):
        z = _dot(h, wg_ref[:, n * d:(n + 1) * d]) + bg_ref[:, n * d:(n + 1) * d]
        gate = 1.0 / (1.0 + jnp.exp(-z))
        merged = merged + gate * _dot(br_ref[...], wb_ref[n])
    o_ref[...] = x + _dot(merged.astype(BF16), wo_ref[...])


def _merge(x, g, oa, ob, oc, wg, bg, wb, wo):
    b, s, d = x.shape
    tm = TOKEN_TILE
    row = lambda c: pl.BlockSpec((None, tm, c), lambda bi, i: (bi, i, 0))
    return pl.pallas_call(
        _merge_kernel,
        grid=(b, s // tm),
        in_specs=[row(d), _full(g.shape), row(MIX_WIDTH), row(MIX_WIDTH), row(MIX_WIDTH),
                  _full(wg.shape), _full(bg.shape), _full(wb.shape), _full(wo.shape)],
        out_specs=row(d),
        out_shape=jax.ShapeDtypeStruct(x.shape, F32),
        compiler_params=_params(),
        name="merge",
    )(x, g, oa, ob, oc, wg, bg, wb, wo)


def _prep_w_in(w_in):
    dup = lambda w: jnp.concatenate([w[:, 0:64], w[:, 0:64], w[:, 64:128], w[:, 64:128]], axis=1)
    kr = w_in[:, 2944:3008]
    return jnp.concatenate([
        w_in[:, 0:2048], dup(w_in[:, 2048:2176]), dup(w_in[:, 2176:2304]), w_in[:, 2304:2944],
        kr, kr[:, 32:64], kr[:, 0:32]], axis=1).astype(BF16)


def _prep_w_uq(w_uq):
    pe = w_uq[:, :, C_NOPE:]
    half = C_ROPE // 2
    w = jnp.concatenate([w_uq, pe[:, :, half:], pe[:, :, :half]], axis=2)
    return w.reshape(w_uq.shape[0], C_HEADS * 2 * LANES).astype(BF16)


def _alibi_key_table(c_alibi):
    term = c_alibi[None, :] * jnp.arange(K_TILE, dtype=F32)[:, None]
    pieces = []
    for _ in range(ALIBI_TERMS):
        piece = term.astype(BF16)
        pieces.append(piece)
        term = term - piece.astype(F32)
    tab = jnp.stack(pieces, axis=-1)
    tab = jnp.pad(tab, ((0, 0), (0, 0), (0, LANES - ALIBI_TERMS)))
    return tab.reshape(K_TILE, -1)


def kernel(x, ffn1_norm, ffn1_w13, ffn1_w2, mix_norm, w_in, w_gate, b_gate, a_lambda, a_subln,
           b_sinks, c_q_norm, c_w_uq, c_kv_norm, c_w_ukv, w_branch, w_out, ffn2_norm, ffn2_w13,
           ffn2_w2, final_norm):
    b, s, d = x.shape
    depth = w_in.shape[0]
    assert s % K_TILE == 0 and K_TILE == TOKEN_TILE and K_TILE % A_Q_TILE == 0 and d % LANES == 0

    pos = jnp.arange(s, dtype=F32)
    inv_freq = ROPE_THETA ** (-jnp.arange(0, C_ROPE, 2, dtype=F32) / C_ROPE)
    ang = pos[:, None] * inv_freq[None, :]
    cos, sin = jnp.cos(ang), jnp.sin(ang)
    pad = jnp.zeros((s, LANES - C_ROPE), F32)
    cos2 = jnp.concatenate([cos, cos, pad], axis=1)
    sin2 = jnp.concatenate([-sin, sin, pad], axis=1)
    c_alibi = 2.0 ** (-8.0 * jnp.arange(1, A_HEADS + 1, dtype=F32) / A_HEADS) * LOG2E
    alibi = _alibi_key_table(c_alibi)
    row = lambda v: v.reshape(1, -1)
    gf = row(final_norm)

    for l in range(depth):
        x = _ffn(x, row(ffn1_norm[l]), ffn1_w13[l].astype(BF16), ffn1_w2[l].astype(BF16), gf, False)

        wkv = c_w_ukv[l]
        qa, ka, vat, qb, kb, vb, qc, kc, vct = _mix_in(
            x, row(mix_norm[l]), _prep_w_in(w_in[l]), row(c_q_norm[l]), _prep_w_uq(c_w_uq[l]),
            row(c_kv_norm[l]),
            wkv[:, :, :C_NOPE].reshape(C_KV_RANK, -1).astype(BF16),
            wkv[:, :, C_NOPE:].reshape(C_KV_RANK, -1).astype(BF16),
            cos2, sin2, alibi)

        lam_init = jnp.full((1, 1), 0.8 - 0.6 * math.exp(-0.3 * l), F32)
        oa = _attn_a(c_alibi, qa, ka, vat, a_lambda[l], lam_init, a_subln[l].reshape(-1, 1))
        ob = _attn_b(b_sinks[l], qb, kb, vb)
        oc = _attn_c(qc, kc, vct)

        x = _merge(x, row(mix_norm[l]), oa, ob, oc, w_gate[l].astype(BF16), row(b_gate[l]),
                   w_branch[l].astype(BF16), w_out[l].astype(BF16))
        x = _ffn(x, row(ffn2_norm[l]), ffn2_w13[l].astype(BF16), ffn2_w2[l].astype(BF16), gf,
                 l == depth - 1)
    return x
```

```python
import functools
import math

import jax
import jax.numpy as jnp
from jax import lax
from jax.experimental import pallas as pl
from jax.experimental.pallas import tpu as pltpu

BF16 = jnp.bfloat16
F32 = jnp.float32

EPS = 1e-6
NEG_INF = -1e30
LOG2E = math.log2(math.e)
ROPE_THETA = 10000.0

A_HEADS = 4
A_QK_DIM = 64
B_Q_HEADS = 8
B_KV_HEADS = 2
B_HEAD_DIM = 64
BAND = 128
C_HEADS = 4
C_Q_RANK = 384
C_KV_RANK = 256
C_NOPE = 128
C_ROPE = 64
MIX_WIDTH = 512
LANES = 128

VMEM_LIMIT = 52 * 1024 * 1024

TOKEN_TILE = 512
FFN_CHUNK = 1408
K_TILE = 512
Q_TILE = 512
A_Q_TILE = 512
V_ROWS = LANES + 16
ACC_ROWS = LANES + 8
ALIBI_TERMS = 3


def _params(**flags):
    return pltpu.CompilerParams(vmem_limit_bytes=VMEM_LIMIT, flags=flags or None)


def _rms(x, g):
    return x * lax.rsqrt(jnp.mean(x * x, axis=-1, keepdims=True) + EPS) * g


def _dot(a, b):
    return jnp.dot(a, b, preferred_element_type=F32)


def _dot_nt(a, b):
    return lax.dot_general(a, b, (((1,), (1,)), ((), ())), preferred_element_type=F32)


def _full(shape):
    return pl.BlockSpec(shape, lambda *_: (0,) * len(shape))


def _ffn_kernel(x_ref, g_ref, w13_ref, w2_ref, gf_ref, o_ref, *, final):
    x = x_ref[...]
    h = _rms(x, g_ref[...]).astype(BF16)
    d_ff = w2_ref.shape[0]
    acc = jnp.zeros(x.shape, F32)
    for f in range(d_ff // FFN_CHUNK):
        lo = f * FFN_CHUNK
        a = _dot(h, w13_ref[:, lo:lo + FFN_CHUNK])
        g = _dot(h, w13_ref[:, d_ff + lo:d_ff + lo + FFN_CHUNK])
        act = (a * (1.0 / (1.0 + jnp.exp(-a))) * g).astype(BF16)
        acc = acc + _dot(act, w2_ref[lo:lo + FFN_CHUNK, :])
    y = x + 0.5 * acc
    if final:
        y = _rms(y, gf_ref[...])
    o_ref[...] = y


def _ffn(x, g, w13, w2, gf, final):
    b, s, d = x.shape
    tm = TOKEN_TILE
    xspec = pl.BlockSpec((None, tm, d), lambda bi, i: (bi, i, 0))
    return pl.pallas_call(
        functools.partial(_ffn_kernel, final=final),
        grid=(b, s // tm),
        in_specs=[xspec, _full(g.shape), _full(w13.shape), _full(w2.shape), _full(gf.shape)],
        out_specs=xspec,
        out_shape=jax.ShapeDtypeStruct(x.shape, F32),
        compiler_params=_params(),
        name="ffn",
    )(x, g, w13, w2, gf)


def _store_values_t(vt_ref, v, heads):
    vt = v.T.astype(BF16)
    ones = jnp.ones((V_ROWS - LANES, vt.shape[1]), BF16)
    for hd in range(heads):
        vt_ref[hd * V_ROWS:hd * V_ROWS + LANES, :] = vt[hd * LANES:(hd + 1) * LANES]
        vt_ref[hd * V_ROWS + LANES:(hd + 1) * V_ROWS, :] = ones


def _mix_in_kernel(x_ref, g_ref, w_ref, cqg_ref, wq_ref, ckvg_ref, wkn_ref, wv_ref,
                   cos_ref, sin_ref, alibi_ref,
                   qa_ref, ka_ref, vat_ref, qb_ref, kb_ref, vbt_ref, qc_ref, kc_ref, vct_ref):
    h = _rms(x_ref[...], g_ref[...]).astype(BF16)
    cos2 = cos_ref[...]
    sin2 = sin_ref[...]

    za = _dot(h, w_ref[:, 0:1536])
    qa_ref[...] = (za[:, 0:512] * (A_QK_DIM ** -0.5 * LOG2E)).astype(BF16)
    for hd in range(A_HEADS):
        ka_ref[:, 2 * hd * LANES:(2 * hd + 1) * LANES] = (
            za[:, 512 + hd * LANES:512 + (hd + 1) * LANES].astype(BF16))
        ka_ref[:, (2 * hd + 1) * LANES:(2 * hd + 2) * LANES] = alibi_ref[:, hd * LANES:(hd + 1) * LANES]
    _store_values_t(vat_ref, za[:, 1024:1536], A_HEADS)

    zb = _dot(h, w_ref[:, 1536:2560])
    qb_ref[...] = (zb[:, 0:512] * (B_HEAD_DIM ** -0.5)).astype(BF16)
    kb_ref[...] = zb[:, 512:768].astype(BF16)
    vbt_ref[...] = zb[:, 768:1024].T.astype(BF16)

    zc = _dot(h, w_ref[:, 2560:3328])
    c_scale = (C_NOPE + C_ROPE) ** -0.5 * LOG2E
    cqn = _rms(zc[:, 0:C_Q_RANK], cqg_ref[...]).astype(BF16)
    q = _dot(cqn, wq_ref[...])
    for hd in range(C_HEADS):
        lo = hd * 2 * LANES
        qc_ref[:, lo:lo + LANES] = (q[:, lo:lo + LANES] * c_scale).astype(BF16)
        qp = q[:, lo + LANES:lo + 2 * LANES]
        qpr = qp * cos2 + pltpu.roll(qp, 64, 1) * sin2
        qc_ref[:, lo + LANES:lo + 2 * LANES] = (qpr * c_scale).astype(BF16)
    ckvn = _rms(zc[:, C_Q_RANK:C_Q_RANK + C_KV_RANK], ckvg_ref[...]).astype(BF16)
    kc_ref[:, 0:512] = _dot(ckvn, wkn_ref[...]).astype(BF16)
    kr = zc[:, C_Q_RANK + C_KV_RANK:]
    kc_ref[:, 512:640] = (kr * cos2 + pltpu.roll(kr, 64, 1) * sin2).astype(BF16)
    _store_values_t(vct_ref, _dot(ckvn, wv_ref[...]), C_HEADS)


def _mix_in(x, g, w, cqg, wq, ckvg, wkn, wv, cos2, sin2, alibi):
    b, s, d = x.shape
    tm = TOKEN_TILE
    row = lambda c: pl.BlockSpec((None, tm, c), lambda bi, i: (bi, i, 0))
    colT = pl.BlockSpec((None, 4 * V_ROWS, tm), lambda bi, i: (bi, 0, i))
    tab = pl.BlockSpec((tm, LANES), lambda bi, i: (i, 0))
    sd = lambda c: jax.ShapeDtypeStruct((b, s, c), BF16)
    sdT = jax.ShapeDtypeStruct((b, 4 * V_ROWS, s), BF16)
    return pl.pallas_call(
        _mix_in_kernel,
        grid=(b, s // tm),
        in_specs=[row(d), _full(g.shape), _full(w.shape), _full(cqg.shape), _full(wq.shape),
                  _full(ckvg.shape), _full(wkn.shape), _full(wv.shape), tab, tab, _full(alibi.shape)],
        out_specs=[row(512), row(1024), colT, row(512), row(256),
                   pl.BlockSpec((None, 2 * LANES, tm), lambda bi, i: (bi, 0, i)),
                   row(1024), row(640), colT],
        out_shape=[sd(512), sd(1024), sdT, sd(512), sd(256),
                   jax.ShapeDtypeStruct((b, 2 * LANES, s), BF16), sd(1024), sd(640), sdT],
        compiler_params=_params(),
        name="mix_in",
    )(x, g, w, cqg, wq, ckvg, wkn, wv, cos2, sin2, alibi)


OVERFLOW_GUARD = 1e37
PROBE_KEYS = 128


def _value_product(vt, p):
    return _dot(vt, p)[:ACC_ROWS]


def _tile_stats(st, vt):
    m_t = jnp.max(st, axis=0, keepdims=True)
    return m_t, _value_product(vt, jnp.exp2(st - m_t).astype(BF16))


def _flash_fixed_reference(n, qk_fn, v_fn, ref_fn, acc, p_refs):
    def weights(t):
        p_refs[t % 2][...] = jnp.exp2(qk_fn(t) - ref_fn(t)).astype(BF16)

    if n:
        weights(0)
    for t in range(n):
        if t + 1 < n:
            weights(t + 1)
        acc = acc + _value_product(v_fn(t), p_refs[t % 2][...])
    return acc


def _flash_running_max(n, score_fn, v_fn, width):
    def body(kt, carry):
        m, acc = carry
        m_t, pv_t = _tile_stats(score_fn(kt), v_fn(kt))
        m_new = jnp.maximum(m, m_t)
        return m_new, jnp.exp2(m - m_new) * acc + jnp.exp2(m_t - m_new) * pv_t

    init = (jnp.full((1, width), -jnp.inf, F32), jnp.zeros((ACC_ROWS, width), F32))
    return lax.fori_loop(0, n, body, init)[1]


def _denominator_overflowed(acc):
    l = acc[LANES:LANES + 1]
    return jnp.max(jnp.where(l < OVERFLOW_GUARD, 0.0, 1.0)) > 0.5


def _attn_a_kernel(c_ref, q_ref, k_ref, vt_ref, lam_ref, lam_init_ref, g_ref, o_ref,
                   qq_ref, p0_ref, p1_ref):
    hd = pl.program_id(1)
    qi = pl.program_id(2)
    tq = q_ref.shape[0]
    n_kt = k_ref.shape[0] // K_TILE
    c = c_ref[hd]

    q = q_ref[...]
    lane = lax.broadcasted_iota(jnp.int32, q.shape, 1)
    zero = jnp.zeros_like(q)
    qm = jnp.concatenate([jnp.where(lane < A_QK_DIM, q, zero),
                          jnp.where(lane >= A_QK_DIM, q, zero)], axis=0)
    sel = (lax.broadcasted_iota(jnp.int32, qm.shape, 1) < ALIBI_TERMS).astype(BF16)
    for slot, sign in enumerate((-1.0, 1.0, 0.0)):
        qq_ref[slot, :, 0:LANES] = qm
        qq_ref[slot, :, LANES:2 * LANES] = sel * sign

    def keys(kt):
        return k_ref[pl.ds(pl.multiple_of(kt * K_TILE, K_TILE), K_TILE), :]

    def values(kt):
        return vt_ref[:, pl.ds(pl.multiple_of(kt * K_TILE, K_TILE), K_TILE)]

    q0 = qi * tq
    rel = (lax.broadcasted_iota(jnp.int32, (K_TILE, tq), 0)
           - lax.broadcasted_iota(jnp.int32, (K_TILE, tq), 1))

    def biased_scores(kt):
        dist = jnp.abs(rel + (kt * K_TILE - q0)).astype(F32) * c
        return _dot_nt(keys(kt), qq_ref[2]) - jnp.concatenate([dist, dist], axis=1)

    def finish(acc):
        lp = lam_ref[...]
        lam_init = lam_init_ref[...]
        lam = (jnp.exp(jnp.sum(lp[0:1] * lp[1:2], axis=1, keepdims=True))
               - jnp.exp(jnp.sum(lp[2:3] * lp[3:4], axis=1, keepdims=True)) + lam_init)
        inv_l = 1.0 / acc[LANES:LANES + 1]
        o = acc[:LANES, :tq] * inv_l[:, :tq] - lam * (acc[:LANES, tq:] * inv_l[:, tq:])
        o = o * lax.rsqrt(jnp.mean(o * o, axis=0, keepdims=True) + EPS) * g_ref[...]
        o = o * (1.0 - lam_init)
        o_ref[...] = o.T.astype(BF16)

    own = k_ref[pl.ds(pl.multiple_of(q0, tq), tq), :]
    dist = jnp.abs(rel[:tq]).astype(F32) * c
    m0 = jnp.max(_dot_nt(own, qq_ref[2]) - jnp.concatenate([dist, dist], axis=1),
                 axis=0, keepdims=True)

    kd = q0 // K_TILE
    q_row = lax.broadcasted_iota(jnp.int32, (1, tq), 1).astype(F32) * c
    q_row = jnp.concatenate([q_row, q_row], axis=1)

    def key_tile(t):
        t = jnp.int32(t - 1)
        kt = t + (t >= kd).astype(jnp.int32)
        return kt, (kt < kd).astype(jnp.int32)

    def qk_fn(t):
        if t == 0:
            return biased_scores(kd)
        kt, before = key_tile(t)
        return _dot_nt(keys(kt), qq_ref[before])

    def v_fn(t):
        return values(kd if t == 0 else key_tile(t)[0])

    def ref_fn(t):
        if t == 0:
            return m0
        kt, before = key_tile(t)
        sgn = (2 * before - 1).astype(F32)
        return m0 - ((kt * K_TILE - q0).astype(F32) * c - q_row) * sgn

    acc = _flash_fixed_reference(n_kt, qk_fn, v_fn, ref_fn, jnp.zeros((ACC_ROWS, 2 * tq), F32),
                                 (p0_ref, p1_ref))
    finish(acc)

    @pl.when(_denominator_overflowed(acc))
    def _():
        finish(_flash_running_max(n_kt, biased_scores, values, 2 * tq))


def _attn_a(c_alibi, qa, ka, vat, lam_p, lam_init, g_col):
    b, s, _ = qa.shape
    tq = A_Q_TILE
    return pl.pallas_call(
        _attn_a_kernel,
        grid=(b, A_HEADS, s // tq),
        in_specs=[
            pl.BlockSpec(memory_space=pltpu.SMEM),
            pl.BlockSpec((None, tq, LANES), lambda bi, h, i: (bi, i, h)),
            pl.BlockSpec((None, s, 2 * LANES), lambda bi, h, i: (bi, 0, h)),
            pl.BlockSpec((None, V_ROWS, s), lambda bi, h, i: (bi, h, 0)),
            _full(lam_p.shape), _full(lam_init.shape), _full(g_col.shape),
        ],
        out_specs=pl.BlockSpec((None, tq, LANES), lambda bi, h, i: (bi, i, h)),
        out_shape=jax.ShapeDtypeStruct((b, s, MIX_WIDTH), BF16),
        scratch_shapes=[pltpu.VMEM((3, 2 * tq, 2 * LANES), BF16),
                        pltpu.VMEM((K_TILE, 2 * tq), BF16), pltpu.VMEM((K_TILE, 2 * tq), BF16)],
        compiler_params=_params(),
        name="attn_a",
    )(c_alibi, qa, ka, vat, lam_p, lam_init, g_col)


def _attn_c_kernel(q_ref, kn_ref, kpe_ref, vt_ref, o_ref, p0_ref, p1_ref):
    q = q_ref[...]
    tq = q.shape[0]
    n_kt = kn_ref.shape[0] // K_TILE

    def scores(kt):
        k0 = pl.multiple_of(kt * K_TILE, K_TILE)
        k = jnp.concatenate([kn_ref[pl.ds(k0, K_TILE), :], kpe_ref[pl.ds(k0, K_TILE), :]], axis=1)
        return _dot_nt(k, q)

    def values(kt):
        return vt_ref[:, pl.ds(pl.multiple_of(kt * K_TILE, K_TILE), K_TILE)]

    def finish(acc):
        o_ref[...] = (acc[:LANES] * (1.0 / acc[LANES:LANES + 1])).T.astype(BF16)

    probe = jnp.concatenate([kn_ref[0:PROBE_KEYS, :], kpe_ref[0:PROBE_KEYS, :]], axis=1)
    m0 = jnp.max(_dot_nt(probe, q), axis=0, keepdims=True)
    acc = _flash_fixed_reference(n_kt, lambda t: scores(jnp.int32(t)), lambda t: values(jnp.int32(t)),
                                 lambda t: m0, jnp.zeros((ACC_ROWS, tq), F32), (p0_ref, p1_ref))
    finish(acc)

    @pl.when(_denominator_overflowed(acc))
    def _():
        finish(_flash_running_max(n_kt, scores, values, tq))


def _attn_c(qc, kc, vct):
    b, s, _ = qc.shape
    tq = Q_TILE
    return pl.pallas_call(
        _attn_c_kernel,
        grid=(b, C_HEADS, s // tq),
        in_specs=[
            pl.BlockSpec((None, tq, 2 * LANES), lambda bi, h, i: (bi, i, h)),
            pl.BlockSpec((None, s, LANES), lambda bi, h, i: (bi, 0, h)),
            pl.BlockSpec((None, s, LANES), lambda bi, h, i: (bi, 0, C_HEADS)),
            pl.BlockSpec((None, V_ROWS, s), lambda bi, h, i: (bi, h, 0)),
        ],
        out_specs=pl.BlockSpec((None, tq, LANES), lambda bi, h, i: (bi, i, h)),
        out_shape=jax.ShapeDtypeStruct((b, s, MIX_WIDTH), BF16),
        scratch_shapes=[pltpu.VMEM((K_TILE, tq), BF16), pltpu.VMEM((K_TILE, tq), BF16)],
        compiler_params=_params(),
        name="attn_c",
    )(qc, kc, kc, vct)


def _attn_b_kernel(sinks_ref, q_ref, kp_ref, ko_ref, kx_ref, vp_ref, vo_ref, vx_ref, o_ref):
    n = pl.program_id(1)
    nb = pl.num_programs(1)
    group = B_Q_HEADS // B_KV_HEADS
    kj = lax.broadcasted_iota(jnp.int32, (3 * BAND, group * BAND), 0) - BAND
    qi = lax.broadcasted_iota(jnp.int32, (3 * BAND, group * BAND), 1) % BAND
    absrel = jnp.abs(kj - qi)
    kpos = n * BAND + kj
    valid = (absrel <= BAND) & (kpos >= 0) & (kpos < nb * BAND)
    absrel_f = absrel.astype(F32)
    half = lax.broadcasted_iota(jnp.int32, (BAND, LANES), 1) // B_HEAD_DIM
    row_half = lax.broadcasted_iota(jnp.int32, (LANES, BAND), 0) // B_HEAD_DIM

    for hk in range(B_KV_HEADS):
        heads = range(hk * group, (hk + 1) * group)
        qg = []
        for h in heads:
            qh = q_ref[:, (h // 2) * LANES:(h // 2 + 1) * LANES]
            qg.append(jnp.where(half == h % 2, qh, jnp.zeros_like(qh)))
        ksl = slice(hk * LANES, (hk + 1) * LANES)
        k3 = jnp.concatenate([kp_ref[:, ksl], ko_ref[:, ksl], kx_ref[:, ksl]], axis=0)
        v3t = jnp.concatenate([vp_ref[ksl, :], vo_ref[ksl, :], vx_ref[ksl, :]], axis=1)
        sc = _dot_nt(k3, jnp.concatenate(qg, axis=0))
        slope = jnp.concatenate([jnp.full((1, BAND), 2.0 ** -(h + 1), F32) for h in heads], axis=1)
        sink = jnp.concatenate([jnp.full((1, BAND), sinks_ref[h], F32) for h in heads], axis=1)
        sc = jnp.where(valid, sc - slope * absrel_f, NEG_INF)
        m = jnp.maximum(jnp.max(sc, axis=0, keepdims=True), sink)
        e = jnp.exp(sc - m)
        den = jnp.sum(e, axis=0, keepdims=True) + jnp.exp(sink - m)
        ot = _dot(v3t, (e * (1.0 / den)).astype(BF16))
        for c in range(group // 2):
            pair = jnp.where(row_half == 0, ot[:, 2 * c * BAND:(2 * c + 1) * BAND],
                             ot[:, (2 * c + 1) * BAND:(2 * c + 2) * BAND])
            col = hk * (group // 2) + c
            o_ref[:, col * LANES:(col + 1) * LANES] = pair.T.astype(BF16)


def _attn_b(sinks, qb, kb, vbt):
    b, s, _ = qb.shape
    nb = s // BAND
    prev = lambda n: jnp.maximum(n - 1, 0)
    own = lambda n: n
    nxt = lambda n: jnp.minimum(n + 1, nb - 1)
    k = lambda f: pl.BlockSpec((None, BAND, 2 * LANES), lambda bi, n: (bi, f(n), 0))
    v = lambda f: pl.BlockSpec((None, 2 * LANES, BAND), lambda bi, n: (bi, 0, f(n)))
    return pl.pallas_call(
        _attn_b_kernel,
        grid=(b, nb),
        in_specs=[pl.BlockSpec(memory_space=pltpu.SMEM),
                  pl.BlockSpec((None, BAND, MIX_WIDTH), lambda bi, n: (bi, n, 0)),
                  k(prev), k(own), k(nxt), v(prev), v(own), v(nxt)],
        out_specs=pl.BlockSpec((None, BAND, MIX_WIDTH), lambda bi, n: (bi, n, 0)),
        out_shape=jax.ShapeDtypeStruct((b, s, MIX_WIDTH), BF16),
        compiler_params=_params(),
        name="attn_b",
    )(sinks, qb, kb, kb, kb, vbt, vbt, vbt)


def _merge_kernel(x_ref, g_ref, oa_ref, ob_ref, oc_ref, wg_ref, bg_ref, wb_ref, wo_ref, o_ref):
    x = x_ref[...]
    d = x.shape[1]
    h = _rms(x, g_ref[...]).astype(BF16)
    merged = jnp.zeros(x.shape, F32)
    for n, br_ref in enumerate((oa_ref, ob_ref, oc_ref)):
        z = _dot(h, wg_ref[:, n * d:(n + 1) * d]) + bg_ref[:, n * d:(n + 1) * d]
        gate = 1.0 / (1.0 + jnp.exp(-z))
        merged = merged + gate * _dot(br_ref[...], wb_ref[n])
    o_ref[...] = x + _dot(merged.astype(BF16), wo_ref[...])


def _merge(x, g, oa, ob, oc, wg, bg, wb, wo):
    b, s, d = x.shape
    tm = TOKEN_TILE
    row = lambda c: pl.BlockSpec((None, tm, c), lambda bi, i: (bi, i, 0))
    return pl.pallas_call(
        _merge_kernel,
        grid=(b, s // tm),
        in_specs=[row(d), _full(g.shape), row(MIX_WIDTH), row(MIX_WIDTH), row(MIX_WIDTH),
                  _full(wg.shape), _full(bg.shape), _full(wb.shape), _full(wo.shape)],
        out_specs=row(d),
        out_shape=jax.ShapeDtypeStruct(x.shape, F32),
        compiler_params=_params(),
        name="merge",
    )(x, g, oa, ob, oc, wg, bg, wb, wo)


def _prep_w_in(w_in):
    dup = lambda w: jnp.concatenate([w[:, 0:64], w[:, 0:64], w[:, 64:128], w[:, 64:128]], axis=1)
    kr = w_in[:, 2944:3008]
    return jnp.concatenate([
        w_in[:, 0:2048], dup(w_in[:, 2048:2176]), dup(w_in[:, 2176:2304]), w_in[:, 2304:2944],
        kr, kr[:, 32:64], kr[:, 0:32]], axis=1).astype(BF16)


def _prep_w_uq(w_uq):
    pe = w_uq[:, :, C_NOPE:]
    half = C_ROPE // 2
    w = jnp.concatenate([w_uq, pe[:, :, half:], pe[:, :, :half]], axis=2)
    return w.reshape(w_uq.shape[0], C_HEADS * 2 * LANES).astype(BF16)


def _alibi_key_table(c_alibi):
    term = c_alibi[None, :] * jnp.arange(K_TILE, dtype=F32)[:, None]
    pieces = []
    for _ in range(ALIBI_TERMS):
        piece = term.astype(BF16)
        pieces.append(piece)
        term = term - piece.astype(F32)
    tab = jnp.stack(pieces, axis=-1)
    tab = jnp.pad(tab, ((0, 0), (0, 0), (0, LANES - ALIBI_TERMS)))
    return tab.reshape(K_TILE, -1)


def kernel(x, ffn1_norm, ffn1_w13, ffn1_w2, mix_norm, w_in, w_gate, b_gate, a_lambda, a_subln,
           b_sinks, c_q_norm, c_w_uq, c_kv_norm, c_w_ukv, w_branch, w_out, ffn2_norm, ffn2_w13,
           ffn2_w2, final_norm):
    b, s, d = x.shape
    depth = w_in.shape[0]
    assert s % K_TILE == 0 and K_TILE == TOKEN_TILE and K_TILE % A_Q_TILE == 0 and d % LANES == 0

    pos = jnp.arange(s, dtype=F32)
    inv_freq = ROPE_THETA ** (-jnp.arange(0, C_ROPE, 2, dtype=F32) / C_ROPE)
    ang = pos[:, None] * inv_freq[None, :]
    cos, sin = jnp.cos(ang), jnp.sin(ang)
    pad = jnp.zeros((s, LANES - C_ROPE), F32)
    cos2 = jnp.concatenate([cos, cos, pad], axis=1)
    sin2 = jnp.concatenate([-sin, sin, pad], axis=1)
    c_alibi = 2.0 ** (-8.0 * jnp.arange(1, A_HEADS + 1, dtype=F32) / A_HEADS) * LOG2E
    alibi = _alibi_key_table(c_alibi)
    row = lambda v: v.reshape(1, -1)
    gf = row(final_norm)

    for l in range(depth):
        x = _ffn(x, row(ffn1_norm[l]), ffn1_w13[l].astype(BF16), ffn1_w2[l].astype(BF16), gf, False)

        wkv = c_w_ukv[l]
        qa, ka, vat, qb, kb, vb, qc, kc, vct = _mix_in(
            x, row(mix_norm[l]), _prep_w_in(w_in[l]), row(c_q_norm[l]), _prep_w_uq(c_w_uq[l]),
            row(c_kv_norm[l]),
            wkv[:, :, :C_NOPE].reshape(C_KV_RANK, -1).astype(BF16),
            wkv[:, :, C_NOPE:].reshape(C_KV_RANK, -1).astype(BF16),
            cos2, sin2, alibi)

        lam_init = jnp.full((1, 1), 0.8 - 0.6 * math.exp(-0.3 * l), F32)
        oa = _attn_a(c_alibi, qa, ka, vat, a_lambda[l], lam_init, a_subln[l].reshape(-1, 1))
        ob = _attn_b(b_sinks[l], qb, kb, vb)
        oc = _attn_c(qc, kc, vct)

        x = _merge(x, row(mix_norm[l]), oa, ob, oc, w_gate[l].astype(BF16), row(b_gate[l]),
                   w_branch[l].astype(BF16), w_out[l].astype(BF16))
        x = _ffn(x, row(ffn2_norm[l]), ffn2_w13[l].astype(BF16), ffn2_w2[l].astype(BF16), gf,
                 l == depth - 1)
    return x
```

```python
import functools
import math

import jax
import jax.numpy as jnp
from jax import lax
from jax.experimental import pallas as pl
from jax.experimental.pallas import tpu as pltpu

BF16 = jnp.bfloat16
F32 = jnp.float32

EPS = 1e-6
NEG_INF = -1e30
LOG2E = math.log2(math.e)
ROPE_THETA = 10000.0

A_HEADS = 4
A_QK_DIM = 64
B_Q_HEADS = 8
B_KV_HEADS = 2
B_HEAD_DIM = 64
BAND = 128
C_HEADS = 4
C_Q_RANK = 384
C_KV_RANK = 256
C_NOPE = 128
C_ROPE = 64
MIX_WIDTH = 512
LANES = 128

VMEM_LIMIT = 52 * 1024 * 1024

TOKEN_TILE = 512
MXU_DIM = 256
FFN_CHUNK = 6 * MXU_DIM
K_TILE = 512
Q_TILE = 1024
A_Q_TILE = 512
V_ROWS = LANES + 16
ACC_ROWS = LANES + 8
ALIBI_TERMS = 3


def _params(**flags):
    return pltpu.CompilerParams(vmem_limit_bytes=VMEM_LIMIT, flags=flags or None)


def _rms(x, g):
    return x * lax.rsqrt(jnp.mean(x * x, axis=-1, keepdims=True) + EPS) * g


def _dot(a, b):
    return jnp.dot(a, b, preferred_element_type=F32)


def _dot_nt(a, b):
    return lax.dot_general(a, b, (((1,), (1,)), ((), ())), preferred_element_type=F32)


def _full(shape):
    return pl.BlockSpec(shape, lambda *_: (0,) * len(shape))


def _ffn_kernel(x_ref, g_ref, w13_ref, w2_ref, gf_ref, o_ref, *, final):
    x = x_ref[...]
    h = _rms(x, g_ref[...]).astype(BF16)
    d_ff = w2_ref.shape[0]
    acc = jnp.zeros(x.shape, F32)
    for lo in range(0, d_ff, FFN_CHUNK):
        hi = min(lo + FFN_CHUNK, d_ff)
        a = _dot(h, w13_ref[:, lo:hi])
        g = _dot(h, w13_ref[:, d_ff + lo:d_ff + hi])
        act = (a * (1.0 / (1.0 + jnp.exp(-a))) * g).astype(BF16)
        acc = acc + _dot(act, w2_ref[lo:hi, :])
    y = x + 0.5 * acc
    if final:
        y = _rms(y, gf_ref[...])
    o_ref[...] = y


def _ffn(x, g, w13, w2, gf, final):
    b, s, d = x.shape
    tm = TOKEN_TILE
    xspec = pl.BlockSpec((None, tm, d), lambda bi, i: (bi, i, 0))
    return pl.pallas_call(
        functools.partial(_ffn_kernel, final=final),
        grid=(b, s // tm),
        in_specs=[xspec, _full(g.shape), _full(w13.shape), _full(w2.shape), _full(gf.shape)],
        out_specs=xspec,
        out_shape=jax.ShapeDtypeStruct(x.shape, F32),
        compiler_params=_params(),
        name="ffn",
    )(x, g, w13, w2, gf)


def _store_values_t(vt_ref, v, heads):
    vt = v.T.astype(BF16)
    ones = jnp.ones((V_ROWS - LANES, vt.shape[1]), BF16)
    for hd in range(heads):
        vt_ref[hd * V_ROWS:hd * V_ROWS + LANES, :] = vt[hd * LANES:(hd + 1) * LANES]
        vt_ref[hd * V_ROWS + LANES:(hd + 1) * V_ROWS, :] = ones


def _mix_in_kernel(x_ref, g_ref, w_ref, cqg_ref, wq_ref, ckvg_ref, wkn_ref, wv_ref,
                   cos_ref, sin_ref, alibi_ref,
                   qa_ref, ka_ref, vat_ref, qb_ref, kb_ref, vbt_ref, qc_ref, kc_ref, vct_ref):
    h = _rms(x_ref[...], g_ref[...]).astype(BF16)
    cos2 = cos_ref[...]
    sin2 = sin_ref[...]

    za = _dot(h, w_ref[:, 0:1536])
    qa_ref[...] = (za[:, 0:512] * (A_QK_DIM ** -0.5 * LOG2E)).astype(BF16)
    for hd in range(A_HEADS):
        ka_ref[:, 2 * hd * LANES:(2 * hd + 1) * LANES] = (
            za[:, 512 + hd * LANES:512 + (hd + 1) * LANES].astype(BF16))
        ka_ref[:, (2 * hd + 1) * LANES:(2 * hd + 2) * LANES] = alibi_ref[:, hd * LANES:(hd + 1) * LANES]
    _store_values_t(vat_ref, za[:, 1024:1536], A_HEADS)

    zb = _dot(h, w_ref[:, 1536:2560])
    qb_ref[...] = (zb[:, 0:512] * (B_HEAD_DIM ** -0.5)).astype(BF16)
    kb_ref[...] = zb[:, 512:768].astype(BF16)
    vbt_ref[...] = zb[:, 768:1024].T.astype(BF16)

    zc = _dot(h, w_ref[:, 2560:3328])
    c_scale = (C_NOPE + C_ROPE) ** -0.5 * LOG2E
    cqn = _rms(zc[:, 0:C_Q_RANK], cqg_ref[...]).astype(BF16)
    q = _dot(cqn, wq_ref[...])
    for hd in range(C_HEADS):
        lo = hd * 2 * LANES
        qc_ref[:, lo:lo + LANES] = (q[:, lo:lo + LANES] * c_scale).astype(BF16)
        qp = q[:, lo + LANES:lo + 2 * LANES]
        qpr = qp * cos2 + pltpu.roll(qp, 64, 1) * sin2
        qc_ref[:, lo + LANES:lo + 2 * LANES] = (qpr * c_scale).astype(BF16)
    ckvn = _rms(zc[:, C_Q_RANK:C_Q_RANK + C_KV_RANK], ckvg_ref[...]).astype(BF16)
    kc_ref[:, 0:512] = _dot(ckvn, wkn_ref[...]).astype(BF16)
    kr = zc[:, C_Q_RANK + C_KV_RANK:]
    kc_ref[:, 512:640] = (kr * cos2 + pltpu.roll(kr, 64, 1) * sin2).astype(BF16)
    _store_values_t(vct_ref, _dot(ckvn, wv_ref[...]), C_HEADS)


def _mix_in(x, g, w, cqg, wq, ckvg, wkn, wv, cos2, sin2, alibi):
    b, s, d = x.shape
    tm = TOKEN_TILE
    row = lambda c: pl.BlockSpec((None, tm, c), lambda bi, i: (bi, i, 0))
    colT = pl.BlockSpec((None, 4 * V_ROWS, tm), lambda bi, i: (bi, 0, i))
    tab = pl.BlockSpec((tm, LANES), lambda bi, i: (i, 0))
    sd = lambda c: jax.ShapeDtypeStruct((b, s, c), BF16)
    sdT = jax.ShapeDtypeStruct((b, 4 * V_ROWS, s), BF16)
    return pl.pallas_call(
        _mix_in_kernel,
        grid=(b, s // tm),
        in_specs=[row(d), _full(g.shape), _full(w.shape), _full(cqg.shape), _full(wq.shape),
                  _full(ckvg.shape), _full(wkn.shape), _full(wv.shape), tab, tab, _full(alibi.shape)],
        out_specs=[row(512), row(1024), colT, row(512), row(256),
                   pl.BlockSpec((None, 2 * LANES, tm), lambda bi, i: (bi, 0, i)),
                   row(1024), row(640), colT],
        out_shape=[sd(512), sd(1024), sdT, sd(512), sd(256),
                   jax.ShapeDtypeStruct((b, 2 * LANES, s), BF16), sd(1024), sd(640), sdT],
        compiler_params=_params(),
        name="mix_in",
    )(x, g, w, cqg, wq, ckvg, wkn, wv, cos2, sin2, alibi)


OVERFLOW_GUARD = 1e37
PROBE_KEYS = 128


def _value_product(vt, p):
    return _dot(vt, p)[:ACC_ROWS]


def _tile_stats(st, vt):
    m_t = jnp.max(st, axis=0, keepdims=True)
    return m_t, _value_product(vt, jnp.exp2(st - m_t).astype(BF16))


def _flash_fixed_reference(n, qk_fn, v_fn, ref_fn, acc, p_refs):
    def weights(t):
        p_refs[t % 2][...] = jnp.exp2(qk_fn(t) - ref_fn(t)).astype(BF16)

    if n:
        weights(0)
    for t in range(n):
        if t + 1 < n:
            weights(t + 1)
        acc = acc + _value_product(v_fn(t), p_refs[t % 2][...])
    return acc


def _flash_running_max(n, score_fn, v_fn, width):
    def body(kt, carry):
        m, acc = carry
        m_t, pv_t = _tile_stats(score_fn(kt), v_fn(kt))
        m_new = jnp.maximum(m, m_t)
        return m_new, jnp.exp2(m - m_new) * acc + jnp.exp2(m_t - m_new) * pv_t

    init = (jnp.full((1, width), -jnp.inf, F32), jnp.zeros((ACC_ROWS, width), F32))
    return lax.fori_loop(0, n, body, init)[1]


def _denominator_overflowed(acc):
    l = acc[LANES:LANES + 1]
    return jnp.max(jnp.where(l < OVERFLOW_GUARD, 0.0, 1.0)) > 0.5


def _attn_a_kernel(c_ref, q_ref, k_ref, vt_ref, lam_ref, lam_init_ref, g_ref, o_ref,
                   qq_ref, p0_ref, p1_ref):
    hd = pl.program_id(1)
    qi = pl.program_id(2)
    tq = q_ref.shape[0]
    n_kt = k_ref.shape[0] // K_TILE
    c = c_ref[hd]

    q = q_ref[...]
    lane = lax.broadcasted_iota(jnp.int32, q.shape, 1)
    zero = jnp.zeros_like(q)
    qm = jnp.concatenate([jnp.where(lane < A_QK_DIM, q, zero),
                          jnp.where(lane >= A_QK_DIM, q, zero)], axis=0)
    sel = (lax.broadcasted_iota(jnp.int32, qm.shape, 1) < ALIBI_TERMS).astype(BF16)
    for slot, sign in enumerate((-1.0, 1.0, 0.0)):
        qq_ref[slot, :, 0:LANES] = qm
        qq_ref[slot, :, LANES:2 * LANES] = sel * sign

    def keys(kt):
        return k_ref[pl.ds(pl.multiple_of(kt * K_TILE, K_TILE), K_TILE), :]

    def values(kt):
        return vt_ref[:, pl.ds(pl.multiple_of(kt * K_TILE, K_TILE), K_TILE)]

    q0 = qi * tq
    rel = (lax.broadcasted_iota(jnp.int32, (K_TILE, tq), 0)
           - lax.broadcasted_iota(jnp.int32, (K_TILE, tq), 1))

    def biased_scores(kt):
        dist = jnp.abs(rel + (kt * K_TILE - q0)).astype(F32) * c
        return _dot_nt(keys(kt), qq_ref[2]) - jnp.concatenate([dist, dist], axis=1)

    def finish(acc):
        lp = lam_ref[...]
        lam_init = lam_init_ref[...]
        lam = (jnp.exp(jnp.sum(lp[0:1] * lp[1:2], axis=1, keepdims=True))
               - jnp.exp(jnp.sum(lp[2:3] * lp[3:4], axis=1, keepdims=True)) + lam_init)
        inv_l = 1.0 / acc[LANES:LANES + 1]
        o = acc[:LANES, :tq] * inv_l[:, :tq] - lam * (acc[:LANES, tq:] * inv_l[:, tq:])
        o = o * lax.rsqrt(jnp.mean(o * o, axis=0, keepdims=True) + EPS) * g_ref[...]
        o = o * (1.0 - lam_init)
        o_ref[...] = o.T.astype(BF16)

    own = k_ref[pl.ds(pl.multiple_of(q0, tq), tq), :]
    dist = jnp.abs(rel[:tq]).astype(F32) * c
    m0 = jnp.max(_dot_nt(own, qq_ref[2]) - jnp.concatenate([dist, dist], axis=1),
                 axis=0, keepdims=True)

    kd = q0 // K_TILE
    q_row = lax.broadcasted_iota(jnp.int32, (1, tq), 1).astype(F32) * c
    q_row = jnp.concatenate([q_row, q_row], axis=1)

    def key_tile(t):
        t = jnp.int32(t - 1)
        kt = t + (t >= kd).astype(jnp.int32)
        return kt, (kt < kd).astype(jnp.int32)

    def qk_fn(t):
        if t == 0:
            return biased_scores(kd)
        kt, before = key_tile(t)
        return _dot_nt(keys(kt), qq_ref[before])

    def v_fn(t):
        return values(kd if t == 0 else key_tile(t)[0])

    def ref_fn(t):
        if t == 0:
            return m0
        kt, before = key_tile(t)
        sgn = (2 * before - 1).astype(F32)
        return m0 - ((kt * K_TILE - q0).astype(F32) * c - q_row) * sgn

    acc = _flash_fixed_reference(n_kt, qk_fn, v_fn, ref_fn, jnp.zeros((ACC_ROWS, 2 * tq), F32),
                                 (p0_ref, p1_ref))
    finish(acc)

    @pl.when(_denominator_overflowed(acc))
    def _():
        finish(_flash_running_max(n_kt, biased_scores, values, 2 * tq))


def _attn_a(c_alibi, qa, ka, vat, lam_p, lam_init, g_col):
    b, s, _ = qa.shape
    tq = A_Q_TILE
    return pl.pallas_call(
        _attn_a_kernel,
        grid=(b, A_HEADS, s // tq),
        in_specs=[
            pl.BlockSpec(memory_space=pltpu.SMEM),
            pl.BlockSpec((None, tq, LANES), lambda bi, h, i: (bi, i, h)),
            pl.BlockSpec((None, s, 2 * LANES), lambda bi, h, i: (bi, 0, h)),
            pl.BlockSpec((None, V_ROWS, s), lambda bi, h, i: (bi, h, 0)),
            _full(lam_p.shape), _full(lam_init.shape), _full(g_col.shape),
        ],
        out_specs=pl.BlockSpec((None, tq, LANES), lambda bi, h, i: (bi, i, h)),
        out_shape=jax.ShapeDtypeStruct((b, s, MIX_WIDTH), BF16),
        scratch_shapes=[pltpu.VMEM((3, 2 * tq, 2 * LANES), BF16),
                        pltpu.VMEM((K_TILE, 2 * tq), BF16), pltpu.VMEM((K_TILE, 2 * tq), BF16)],
        compiler_params=_params(),
        name="attn_a",
    )(c_alibi, qa, ka, vat, lam_p, lam_init, g_col)


def _attn_c_kernel(q_ref, kn_ref, kpe_ref, vt_ref, o_ref, p0_ref, p1_ref):
    q = q_ref[...]
    tq = q.shape[0]
    n_kt = kn_ref.shape[0] // K_TILE

    def scores(kt):
        k0 = pl.multiple_of(kt * K_TILE, K_TILE)
        k = jnp.concatenate([kn_ref[pl.ds(k0, K_TILE), :], kpe_ref[pl.ds(k0, K_TILE), :]], axis=1)
        return _dot_nt(k, q)

    def values(kt):
        return vt_ref[:, pl.ds(pl.multiple_of(kt * K_TILE, K_TILE), K_TILE)]

    def finish(acc):
        o_ref[...] = (acc[:LANES] * (1.0 / acc[LANES:LANES + 1])).T.astype(BF16)

    probe = jnp.concatenate([kn_ref[0:PROBE_KEYS, :], kpe_ref[0:PROBE_KEYS, :]], axis=1)
    m0 = jnp.max(_dot_nt(probe, q), axis=0, keepdims=True)
    acc = _flash_fixed_reference(n_kt, lambda t: scores(jnp.int32(t)), lambda t: values(jnp.int32(t)),
                                 lambda t: m0, jnp.zeros((ACC_ROWS, tq), F32), (p0_ref, p1_ref))
    finish(acc)

    @pl.when(_denominator_overflowed(acc))
    def _():
        finish(_flash_running_max(n_kt, scores, values, tq))


def _attn_c(qc, kc, vct):
    b, s, _ = qc.shape
    tq = Q_TILE
    return pl.pallas_call(
        _attn_c_kernel,
        grid=(b, C_HEADS, s // tq),
        in_specs=[
            pl.BlockSpec((None, tq, 2 * LANES), lambda bi, h, i: (bi, i, h)),
            pl.BlockSpec((None, s, LANES), lambda bi, h, i: (bi, 0, h)),
            pl.BlockSpec((None, s, LANES), lambda bi, h, i: (bi, 0, C_HEADS)),
            pl.BlockSpec((None, V_ROWS, s), lambda bi, h, i: (bi, h, 0)),
        ],
        out_specs=pl.BlockSpec((None, tq, LANES), lambda bi, h, i: (bi, i, h)),
        out_shape=jax.ShapeDtypeStruct((b, s, MIX_WIDTH), BF16),
        scratch_shapes=[pltpu.VMEM((K_TILE, tq), BF16), pltpu.VMEM((K_TILE, tq), BF16)],
        compiler_params=_params(),
        name="attn_c",
    )(qc, kc, kc, vct)


def _attn_b_kernel(sinks_ref, q_ref, kp_ref, ko_ref, kx_ref, vp_ref, vo_ref, vx_ref, o_ref):
    n = pl.program_id(1)
    nb = pl.num_programs(1)
    group = B_Q_HEADS // B_KV_HEADS
    kj = lax.broadcasted_iota(jnp.int32, (3 * BAND, group * BAND), 0) - BAND
    qi = lax.broadcasted_iota(jnp.int32, (3 * BAND, group * BAND), 1) % BAND
    absrel = jnp.abs(kj - qi)
    kpos = n * BAND + kj
    valid = (absrel <= BAND) & (kpos >= 0) & (kpos < nb * BAND)
    absrel_f = absrel.astype(F32)
    half = lax.broadcasted_iota(jnp.int32, (BAND, LANES), 1) // B_HEAD_DIM
    row_half = lax.broadcasted_iota(jnp.int32, (LANES, BAND), 0) // B_HEAD_DIM

    for hk in range(B_KV_HEADS):
        heads = range(hk * group, (hk + 1) * group)
        qg = []
        for h in heads:
            qh = q_ref[:, (h // 2) * LANES:(h // 2 + 1) * LANES]
            qg.append(jnp.where(half == h % 2, qh, jnp.zeros_like(qh)))
        ksl = slice(hk * LANES, (hk + 1) * LANES)
        k3 = jnp.concatenate([kp_ref[:, ksl], ko_ref[:, ksl], kx_ref[:, ksl]], axis=0)
        v3t = jnp.concatenate([vp_ref[ksl, :], vo_ref[ksl, :], vx_ref[ksl, :]], axis=1)
        sc = _dot_nt(k3, jnp.concatenate(qg, axis=0))
        slope = jnp.concatenate([jnp.full((1, BAND), 2.0 ** -(h + 1), F32) for h in heads], axis=1)
        sink = jnp.concatenate([jnp.full((1, BAND), sinks_ref[h], F32) for h in heads], axis=1)
        sc = jnp.where(valid, sc - slope * absrel_f, NEG_INF)
        m = jnp.maximum(jnp.max(sc, axis=0, keepdims=True), sink)
        e = jnp.exp(sc - m)
        den = jnp.sum(e, axis=0, keepdims=True) + jnp.exp(sink - m)
        ot = _dot(v3t, (e * (1.0 / den)).astype(BF16))
        for c in range(group // 2):
            pair = jnp.where(row_half == 0, ot[:, 2 * c * BAND:(2 * c + 1) * BAND],
                             ot[:, (2 * c + 1) * BAND:(2 * c + 2) * BAND])
            col = hk * (group // 2) + c
            o_ref[:, col * LANES:(col + 1) * LANES] = pair.T.astype(BF16)


def _attn_b(sinks, qb, kb, vbt):
    b, s, _ = qb.shape
    nb = s // BAND
    prev = lambda n: jnp.maximum(n - 1, 0)
    own = lambda n: n
    nxt = lambda n: jnp.minimum(n + 1, nb - 1)
    k = lambda f: pl.BlockSpec((None, BAND, 2 * LANES), lambda bi, n: (bi, f(n), 0))
    v = lambda f: pl.BlockSpec((None, 2 * LANES, BAND), lambda bi, n: (bi, 0, f(n)))
    return pl.pallas_call(
        _attn_b_kernel,
        grid=(b, nb),
        in_specs=[pl.BlockSpec(memory_space=pltpu.SMEM),
                  pl.BlockSpec((None, BAND, MIX_WIDTH), lambda bi, n: (bi, n, 0)),
                  k(prev), k(own), k(nxt), v(prev), v(own), v(nxt)],
        out_specs=pl.BlockSpec((None, BAND, MIX_WIDTH), lambda bi, n: (bi, n, 0)),
        out_shape=jax.ShapeDtypeStruct((b, s, MIX_WIDTH), BF16),
        compiler_params=_params(),
        name="attn_b",
    )(sinks, qb, kb, kb, kb, vbt, vbt, vbt)


def _merge_kernel(x_ref, g_ref, oa_ref, ob_ref, oc_ref, wg_ref, bg_ref, wb_ref, wo_ref, o_ref):
    x = x_ref[...]
    d = x.shape[1]
    h = _rms(x, g_ref[...]).astype(BF16)
    merged = jnp.zeros(x.shape, F32)
    for n, br_ref in enumerate((oa_ref, ob_ref, oc_ref)):
        z = _dot(h, wg_ref[:, n * d:(n + 1) * d]) + bg_ref[:, n * d:(n + 1) * d]
        gate = 1.0 / (1.0 + jnp.exp(-z))
        merged = merged + gate * _dot(br_ref[...], wb_ref[n])
    o_ref[...] = x + _dot(merged.astype(BF16), wo_ref[...])


def _merge(x, g, oa, ob, oc, wg, bg, wb, wo):
    b, s, d = x.shape
    tm = TOKEN_TILE
    row = lambda c: pl.BlockSpec((None, tm, c), lambda bi, i: (bi, i, 0))
    return pl.pallas_call(
        _merge_kernel,
        grid=(b, s // tm),
        in_specs=[row(d), _full(g.shape), row(MIX_WIDTH), row(MIX_WIDTH), row(MIX_WIDTH),
                  _full(wg.shape), _full(bg.shape), _full(wb.shape), _full(wo.shape)],
        out_specs=row(d),
        out_shape=jax.ShapeDtypeStruct(x.shape, F32),
        compiler_params=_params(),
        name="merge",
    )(x, g, oa, ob, oc, wg, bg, wb, wo)


def _prep_w_in(w_in):
    dup = lambda w: jnp.concatenate([w[:, 0:64], w[:, 0:64], w[:, 64:128], w[:, 64:128]], axis=1)
    kr = w_in[:, 2944:3008]
    return jnp.concatenate([
        w_in[:, 0:2048], dup(w_in[:, 2048:2176]), dup(w_in[:, 2176:2304]), w_in[:, 2304:2944],
        kr, kr[:, 32:64], kr[:, 0:32]], axis=1).astype(BF16)


def _prep_w_uq(w_uq):
    pe = w_uq[:, :, C_NOPE:]
    half = C_ROPE // 2
    w = jnp.concatenate([w_uq, pe[:, :, half:], pe[:, :, :half]], axis=2)
    return w.reshape(w_uq.shape[0], C_HEADS * 2 * LANES).astype(BF16)


def _alibi_key_table(c_alibi):
    term = c_alibi[None, :] * jnp.arange(K_TILE, dtype=F32)[:, None]
    pieces = []
    for _ in range(ALIBI_TERMS):
        piece = term.astype(BF16)
        pieces.append(piece)
        term = term - piece.astype(F32)
    tab = jnp.stack(pieces, axis=-1)
    tab = jnp.pad(tab, ((0, 0), (0, 0), (0, LANES - ALIBI_TERMS)))
    return tab.reshape(K_TILE, -1)


def kernel(x, ffn1_norm, ffn1_w13, ffn1_w2, mix_norm, w_in, w_gate, b_gate, a_lambda, a_subln,
           b_sinks, c_q_norm, c_w_uq, c_kv_norm, c_w_ukv, w_branch, w_out, ffn2_norm, ffn2_w13,
           ffn2_w2, final_norm):
    b, s, d = x.shape
    depth = w_in.shape[0]
    assert s % K_TILE == 0 and K_TILE == TOKEN_TILE and K_TILE % A_Q_TILE == 0 and d % LANES == 0

    pos = jnp.arange(s, dtype=F32)
    inv_freq = ROPE_THETA ** (-jnp.arange(0, C_ROPE, 2, dtype=F32) / C_ROPE)
    ang = pos[:, None] * inv_freq[None, :]
    cos, sin = jnp.cos(ang), jnp.sin(ang)
    pad = jnp.zeros((s, LANES - C_ROPE), F32)
    cos2 = jnp.concatenate([cos, cos, pad], axis=1)
    sin2 = jnp.concatenate([-sin, sin, pad], axis=1)
    c_alibi = 2.0 ** (-8.0 * jnp.arange(1, A_HEADS + 1, dtype=F32) / A_HEADS) * LOG2E
    alibi = _alibi_key_table(c_alibi)
    row = lambda v: v.reshape(1, -1)
    gf = row(final_norm)

    for l in range(depth):
        x = _ffn(x, row(ffn1_norm[l]), ffn1_w13[l].astype(BF16), ffn1_w2[l].astype(BF16), gf, False)

        wkv = c_w_ukv[l]
        qa, ka, vat, qb, kb, vb, qc, kc, vct = _mix_in(
            x, row(mix_norm[l]), _prep_w_in(w_in[l]), row(c_q_norm[l]), _prep_w_uq(c_w_uq[l]),
            row(c_kv_norm[l]),
            wkv[:, :, :C_NOPE].reshape(C_KV_RANK, -1).astype(BF16),
            wkv[:, :, C_NOPE:].reshape(C_KV_RANK, -1).astype(BF16),
            cos2, sin2, alibi)

        lam_init = jnp.full((1, 1), 0.8 - 0.6 * math.exp(-0.3 * l), F32)
        oa = _attn_a(c_alibi, qa, ka, vat, a_lambda[l], lam_init, a_subln[l].reshape(-1, 1))
        ob = _attn_b(b_sinks[l], qb, kb, vb)
        oc = _attn_c(qc, kc, vct)

        x = _merge(x, row(mix_norm[l]), oa, ob, oc, w_gate[l].astype(BF16), row(b_gate[l]),
                   w_branch[l].astype(BF16), w_out[l].astype(BF16))
        x = _ffn(x, row(ffn2_norm[l]), ffn2_w13[l].astype(BF16), ffn2_w2[l].astype(BF16), gf,
                 l == depth - 1)
    return x
```

```python
import functools
import math

import jax
import jax.numpy as jnp
from jax import lax
from jax.experimental import pallas as pl
from jax.experimental.pallas import tpu as pltpu

BF16 = jnp.bfloat16
F32 = jnp.float32

EPS = 1e-6
NEG_INF = -1e30
LOG2E = math.log2(math.e)
ROPE_THETA = 10000.0

A_HEADS = 4
A_QK_DIM = 64
B_Q_HEADS = 8
B_KV_HEADS = 2
B_HEAD_DIM = 64
BAND = 128
B_STEP_BLOCKS = 4
C_HEADS = 4
C_Q_RANK = 384
C_KV_RANK = 256
C_NOPE = 128
C_ROPE = 64
MIX_WIDTH = 512
LANES = 128

VMEM_LIMIT = 52 * 1024 * 1024

TOKEN_TILE = 512
MXU_DIM = 256
FFN_CHUNK = 6 * MXU_DIM
K_TILE = 512
C_K_TILE = 1024
Q_TILE = 1024
A_Q_TILE = 512
V_ROWS = LANES + 16
ACC_ROWS = LANES + 8
ALIBI_TERMS = 3


def _params(**flags):
    return pltpu.CompilerParams(vmem_limit_bytes=VMEM_LIMIT, flags=flags or None)


def _rms(x, g):
    return x * lax.rsqrt(jnp.mean(x * x, axis=-1, keepdims=True) + EPS) * g


def _dot(a, b):
    return jnp.dot(a, b, preferred_element_type=F32)


def _dot_nt(a, b):
    return lax.dot_general(a, b, (((1,), (1,)), ((), ())), preferred_element_type=F32)


def _full(shape):
    return pl.BlockSpec(shape, lambda *_: (0,) * len(shape))


def _ffn_kernel(x_ref, g_ref, w13_ref, w2_ref, gf_ref, o_ref, *, final):
    x = x_ref[...]
    h = _rms(x, g_ref[...]).astype(BF16)
    d_ff = w2_ref.shape[0]
    acc = jnp.zeros(x.shape, F32)
    for lo in range(0, d_ff, FFN_CHUNK):
        hi = min(lo + FFN_CHUNK, d_ff)
        a = _dot(h, w13_ref[:, lo:hi])
        g = _dot(h, w13_ref[:, d_ff + lo:d_ff + hi])
        act = (a * (1.0 / (1.0 + jnp.exp(-a))) * g).astype(BF16)
        acc = acc + _dot(act, w2_ref[lo:hi, :])
    y = x + 0.5 * acc
    if final:
        y = _rms(y, gf_ref[...])
    o_ref[...] = y


def _ffn(x, g, w13, w2, gf, final):
    b, s, d = x.shape
    tm = TOKEN_TILE
    xspec = pl.BlockSpec((None, tm, d), lambda bi, i: (bi, i, 0))
    return pl.pallas_call(
        functools.partial(_ffn_kernel, final=final),
        grid=(b, s // tm),
        in_specs=[xspec, _full(g.shape), _full(w13.shape), _full(w2.shape), _full(gf.shape)],
        out_specs=xspec,
        out_shape=jax.ShapeDtypeStruct(x.shape, F32),
        compiler_params=_params(),
        name="ffn",
    )(x, g, w13, w2, gf)


def _store_values_t(vt_ref, v, heads):
    vt = v.T.astype(BF16)
    ones = jnp.ones((V_ROWS - LANES, vt.shape[1]), BF16)
    for hd in range(heads):
        vt_ref[hd * V_ROWS:hd * V_ROWS + LANES, :] = vt[hd * LANES:(hd + 1) * LANES]
        vt_ref[hd * V_ROWS + LANES:(hd + 1) * V_ROWS, :] = ones


def _mix_in_kernel(x_ref, g_ref, w_ref, cqg_ref, wq_ref, ckvg_ref, wkn_ref, wv_ref,
                   cos_ref, sin_ref, alibi_ref,
                   qa_ref, ka_ref, vat_ref, qb_ref, kb_ref, vbt_ref, qc_ref, kc_ref, vct_ref):
    h = _rms(x_ref[...], g_ref[...]).astype(BF16)
    cos2 = cos_ref[...]
    sin2 = sin_ref[...]

    za = _dot(h, w_ref[:, 0:1536])
    qa_ref[...] = (za[:, 0:512] * (A_QK_DIM ** -0.5 * LOG2E)).astype(BF16)
    for hd in range(A_HEADS):
        ka_ref[:, 2 * hd * LANES:(2 * hd + 1) * LANES] = (
            za[:, 512 + hd * LANES:512 + (hd + 1) * LANES].astype(BF16))
        ka_ref[:, (2 * hd + 1) * LANES:(2 * hd + 2) * LANES] = alibi_ref[:, hd * LANES:(hd + 1) * LANES]
    _store_values_t(vat_ref, za[:, 1024:1536], A_HEADS)

    zb = _dot(h, w_ref[:, 1536:2560])
    qb_ref[...] = (zb[:, 0:512] * (B_HEAD_DIM ** -0.5)).astype(BF16)
    kb_ref[...] = zb[:, 512:768].astype(BF16)
    vbt_ref[...] = zb[:, 768:1024].T.astype(BF16)

    zc = _dot(h, w_ref[:, 2560:3328])
    c_scale = (C_NOPE + C_ROPE) ** -0.5 * LOG2E
    cqn = _rms(zc[:, 0:C_Q_RANK], cqg_ref[...]).astype(BF16)
    q = _dot(cqn, wq_ref[...])
    for hd in range(C_HEADS):
        lo = hd * 2 * LANES
        qc_ref[:, lo:lo + LANES] = (q[:, lo:lo + LANES] * c_scale).astype(BF16)
        qp = q[:, lo + LANES:lo + 2 * LANES]
        qpr = qp * cos2 + pltpu.roll(qp, 64, 1) * sin2
        qc_ref[:, lo + LANES:lo + 2 * LANES] = (qpr * c_scale).astype(BF16)
    ckvn = _rms(zc[:, C_Q_RANK:C_Q_RANK + C_KV_RANK], ckvg_ref[...]).astype(BF16)
    kc_ref[:, 0:512] = _dot(ckvn, wkn_ref[...]).astype(BF16)
    kr = zc[:, C_Q_RANK + C_KV_RANK:]
    kc_ref[:, 512:640] = (kr * cos2 + pltpu.roll(kr, 64, 1) * sin2).astype(BF16)
    _store_values_t(vct_ref, _dot(ckvn, wv_ref[...]), C_HEADS)


def _mix_in(x, g, w, cqg, wq, ckvg, wkn, wv, cos2, sin2, alibi):
    b, s, d = x.shape
    tm = TOKEN_TILE
    row = lambda c: pl.BlockSpec((None, tm, c), lambda bi, i: (bi, i, 0))
    colT = pl.BlockSpec((None, 4 * V_ROWS, tm), lambda bi, i: (bi, 0, i))
    tab = pl.BlockSpec((tm, LANES), lambda bi, i: (i, 0))
    sd = lambda c: jax.ShapeDtypeStruct((b, s, c), BF16)
    sdT = jax.ShapeDtypeStruct((b, 4 * V_ROWS, s), BF16)
    return pl.pallas_call(
        _mix_in_kernel,
        grid=(b, s // tm),
        in_specs=[row(d), _full(g.shape), _full(w.shape), _full(cqg.shape), _full(wq.shape),
                  _full(ckvg.shape), _full(wkn.shape), _full(wv.shape), tab, tab,
                  pl.BlockSpec((tm, alibi.shape[1]), lambda bi, i: (i % (K_TILE // tm), 0))],
        out_specs=[row(512), row(1024), colT, row(512), row(256),
                   pl.BlockSpec((None, 2 * LANES, tm), lambda bi, i: (bi, 0, i)),
                   row(1024), row(640), colT],
        out_shape=[sd(512), sd(1024), sdT, sd(512), sd(256),
                   jax.ShapeDtypeStruct((b, 2 * LANES, s), BF16), sd(1024), sd(640), sdT],
        compiler_params=_params(),
        name="mix_in",
    )(x, g, w, cqg, wq, ckvg, wkn, wv, cos2, sin2, alibi)


OVERFLOW_GUARD = 1e37
PROBE_KEYS = 128


def _value_product(vt, p):
    return _dot(vt, p)[:ACC_ROWS]


def _tile_stats(st, vt):
    m_t = jnp.max(st, axis=0, keepdims=True)
    return m_t, _value_product(vt, jnp.exp2(st - m_t).astype(BF16))


def _flash_fixed_reference(n, qk_fn, v_fn, ref_fn, acc, p_refs):
    def weights(t):
        p_refs[t % 2][...] = jnp.exp2(qk_fn(t) - ref_fn(t)).astype(BF16)

    if n:
        weights(0)
    for t in range(n):
        if t + 1 < n:
            weights(t + 1)
        acc = acc + _value_product(v_fn(t), p_refs[t % 2][...])
    return acc


def _flash_running_max(n, score_fn, v_fn, width):
    def body(kt, carry):
        m, acc = carry
        m_t, pv_t = _tile_stats(score_fn(kt), v_fn(kt))
        m_new = jnp.maximum(m, m_t)
        return m_new, jnp.exp2(m - m_new) * acc + jnp.exp2(m_t - m_new) * pv_t

    init = (jnp.full((1, width), -jnp.inf, F32), jnp.zeros((ACC_ROWS, width), F32))
    return lax.fori_loop(0, n, body, init)[1]


def _denominator_overflowed(acc):
    l = acc[LANES:LANES + 1]
    return jnp.max(jnp.where(l < OVERFLOW_GUARD, 0.0, 1.0)) > 0.5


def _attn_a_kernel(c_ref, q_ref, k_ref, vt_ref, lam_ref, lam_init_ref, g_ref, o_ref,
                   qq_ref, p0_ref, p1_ref):
    hd = pl.program_id(1)
    qi = pl.program_id(2)
    tq = q_ref.shape[0]
    n_kt = k_ref.shape[0] // K_TILE
    c = c_ref[hd]

    q = q_ref[...]
    lane = lax.broadcasted_iota(jnp.int32, q.shape, 1)
    zero = jnp.zeros_like(q)
    qm = jnp.concatenate([jnp.where(lane < A_QK_DIM, q, zero),
                          jnp.where(lane >= A_QK_DIM, q, zero)], axis=0)
    sel = (lax.broadcasted_iota(jnp.int32, qm.shape, 1) < ALIBI_TERMS).astype(BF16)
    for slot, sign in enumerate((-1.0, 1.0, 0.0)):
        qq_ref[slot, :, 0:LANES] = qm
        qq_ref[slot, :, LANES:2 * LANES] = sel * sign

    def keys(kt):
        return k_ref[pl.ds(pl.multiple_of(kt * K_TILE, K_TILE), K_TILE), :]

    def values(kt):
        return vt_ref[:, pl.ds(pl.multiple_of(kt * K_TILE, K_TILE), K_TILE)]

    q0 = qi * tq
    rel = (lax.broadcasted_iota(jnp.int32, (K_TILE, tq), 0)
           - lax.broadcasted_iota(jnp.int32, (K_TILE, tq), 1))

    def biased_scores(kt):
        dist = jnp.abs(rel + (kt * K_TILE - q0)).astype(F32) * c
        return _dot_nt(keys(kt), qq_ref[2]) - jnp.concatenate([dist, dist], axis=1)

    def finish(acc):
        lp = lam_ref[...]
        lam_init = lam_init_ref[...]
        lam = (jnp.exp(jnp.sum(lp[0:1] * lp[1:2], axis=1, keepdims=True))
               - jnp.exp(jnp.sum(lp[2:3] * lp[3:4], axis=1, keepdims=True)) + lam_init)
        inv_l = 1.0 / acc[LANES:LANES + 1]
        o = acc[:LANES, :tq] * inv_l[:, :tq] - lam * (acc[:LANES, tq:] * inv_l[:, tq:])
        o = o * lax.rsqrt(jnp.mean(o * o, axis=0, keepdims=True) + EPS) * g_ref[...]
        o = o * (1.0 - lam_init)
        o_ref[...] = o.T.astype(BF16)

    own = k_ref[pl.ds(pl.multiple_of(q0, tq), tq), :]
    dist = jnp.abs(rel[:tq]).astype(F32) * c
    m0 = jnp.max(_dot_nt(own, qq_ref[2]) - jnp.concatenate([dist, dist], axis=1),
                 axis=0, keepdims=True)

    kd = q0 // K_TILE
    q_row = lax.broadcasted_iota(jnp.int32, (1, tq), 1).astype(F32) * c
    q_row = jnp.concatenate([q_row, q_row], axis=1)

    def key_tile(t):
        t = jnp.int32(t - 1)
        kt = t + (t >= kd).astype(jnp.int32)
        return kt, (kt < kd).astype(jnp.int32)

    def qk_fn(t):
        if t == 0:
            return biased_scores(kd)
        kt, before = key_tile(t)
        return _dot_nt(keys(kt), qq_ref[before])

    def v_fn(t):
        return values(kd if t == 0 else key_tile(t)[0])

    def ref_fn(t):
        if t == 0:
            return m0
        kt, before = key_tile(t)
        sgn = (2 * before - 1).astype(F32)
        return m0 - ((kt * K_TILE - q0).astype(F32) * c - q_row) * sgn

    acc = _flash_fixed_reference(n_kt, qk_fn, v_fn, ref_fn, jnp.zeros((ACC_ROWS, 2 * tq), F32),
                                 (p0_ref, p1_ref))
    finish(acc)

    @pl.when(_denominator_overflowed(acc))
    def _():
        finish(_flash_running_max(n_kt, biased_scores, values, 2 * tq))


def _attn_a(c_alibi, qa, ka, vat, lam_p, lam_init, g_col):
    b, s, _ = qa.shape
    tq = A_Q_TILE
    return pl.pallas_call(
        _attn_a_kernel,
        grid=(b, A_HEADS, s // tq),
        in_specs=[
            pl.BlockSpec(memory_space=pltpu.SMEM),
            pl.BlockSpec((None, tq, LANES), lambda bi, h, i: (bi, i, h)),
            pl.BlockSpec((None, s, 2 * LANES), lambda bi, h, i: (bi, 0, h)),
            pl.BlockSpec((None, V_ROWS, s), lambda bi, h, i: (bi, h, 0)),
            _full(lam_p.shape), _full(lam_init.shape), _full(g_col.shape),
        ],
        out_specs=pl.BlockSpec((None, tq, LANES), lambda bi, h, i: (bi, i, h)),
        out_shape=jax.ShapeDtypeStruct((b, s, MIX_WIDTH), BF16),
        scratch_shapes=[pltpu.VMEM((3, 2 * tq, 2 * LANES), BF16),
                        pltpu.VMEM((K_TILE, 2 * tq), BF16), pltpu.VMEM((K_TILE, 2 * tq), BF16)],
        compiler_params=_params(),
        name="attn_a",
    )(c_alibi, qa, ka, vat, lam_p, lam_init, g_col)


def _attn_c_kernel(q_ref, kn_ref, kpe_ref, vt_ref, o_ref, p0_ref, p1_ref):
    q = q_ref[...]
    tq = q.shape[0]
    tk = C_K_TILE
    n_kt = kn_ref.shape[0] // tk

    def scores(kt):
        k0 = pl.multiple_of(kt * tk, tk)
        k = jnp.concatenate([kn_ref[pl.ds(k0, tk), :], kpe_ref[pl.ds(k0, tk), :]], axis=1)
        return _dot_nt(k, q)

    def values(kt):
        return vt_ref[:, pl.ds(pl.multiple_of(kt * tk, tk), tk)]

    def finish(acc):
        o_ref[...] = (acc[:LANES] * (1.0 / acc[LANES:LANES + 1])).T.astype(BF16)

    probe = jnp.concatenate([kn_ref[0:PROBE_KEYS, :], kpe_ref[0:PROBE_KEYS, :]], axis=1)
    m0 = jnp.max(_dot_nt(probe, q), axis=0, keepdims=True)
    acc = _flash_fixed_reference(n_kt, lambda t: scores(jnp.int32(t)), lambda t: values(jnp.int32(t)),
                                 lambda t: m0, jnp.zeros((ACC_ROWS, tq), F32), (p0_ref, p1_ref))
    finish(acc)

    @pl.when(_denominator_overflowed(acc))
    def _():
        finish(_flash_running_max(n_kt, scores, values, tq))


def _attn_c(qc, kc, vct):
    b, s, _ = qc.shape
    tq = Q_TILE
    return pl.pallas_call(
        _attn_c_kernel,
        grid=(b, C_HEADS, s // tq),
        in_specs=[
            pl.BlockSpec((None, tq, 2 * LANES), lambda bi, h, i: (bi, i, h)),
            pl.BlockSpec((None, s, LANES), lambda bi, h, i: (bi, 0, h)),
            pl.BlockSpec((None, s, LANES), lambda bi, h, i: (bi, 0, C_HEADS)),
            pl.BlockSpec((None, V_ROWS, s), lambda bi, h, i: (bi, h, 0)),
        ],
        out_specs=pl.BlockSpec((None, tq, LANES), lambda bi, h, i: (bi, i, h)),
        out_shape=jax.ShapeDtypeStruct((b, s, MIX_WIDTH), BF16),
        scratch_shapes=[pltpu.VMEM((C_K_TILE, tq), BF16), pltpu.VMEM((C_K_TILE, tq), BF16)],
        compiler_params=_params(),
        name="attn_c",
    )(qc, kc, kc, vct)


def _attn_b_kernel(sinks_ref, q_ref, kp_ref, ko_ref, kx_ref, vp_ref, vo_ref, vx_ref, bias_ref, o_ref):
    first = pl.program_id(1) * B_STEP_BLOCKS
    nb = pl.num_programs(1) * B_STEP_BLOCKS
    group = B_Q_HEADS // B_KV_HEADS
    key_row = lax.broadcasted_iota(jnp.int32, (3 * BAND, 1), 0) - BAND
    half = lax.broadcasted_iota(jnp.int32, (BAND, LANES), 1) // B_HEAD_DIM
    row_half = lax.broadcasted_iota(jnp.int32, (LANES, BAND), 0) // B_HEAD_DIM

    for hk in range(B_KV_HEADS):
        heads = range(hk * group, (hk + 1) * group)
        ksl = slice(hk * LANES, (hk + 1) * LANES)
        k_all = jnp.concatenate([kp_ref[:, ksl], ko_ref[:, ksl], kx_ref[:, ksl]], axis=0)
        vt_all = jnp.concatenate([vp_ref[ksl, :], vo_ref[ksl, :], vx_ref[ksl, :]], axis=1)
        sink = jnp.concatenate([jnp.full((1, BAND), sinks_ref[h], F32) for h in heads], axis=1)
        for u in range(B_STEP_BLOCKS):
            rows = slice(u * BAND, (u + 1) * BAND)
            qg = []
            for h in heads:
                qh = q_ref[rows, (h // 2) * LANES:(h // 2 + 1) * LANES]
                qg.append(jnp.where(half == h % 2, qh, jnp.zeros_like(qh)))
            kpos = (first + u) * BAND + key_row
            edge = jnp.where((kpos >= 0) & (kpos < nb * BAND), 0.0, NEG_INF)
            sc = _dot_nt(k_all[u * BAND:(u + 3) * BAND], jnp.concatenate(qg, axis=0))
            sc = jnp.maximum(sc + bias_ref[hk] + edge, NEG_INF)
            m = jnp.maximum(jnp.max(sc, axis=0, keepdims=True), sink)
            e = jnp.exp(sc - m)
            den = jnp.sum(e, axis=0, keepdims=True) + jnp.exp(sink - m)
            ot = _dot(vt_all[:, u * BAND:(u + 3) * BAND], e.astype(BF16)) * (1.0 / den)
            for c in range(group // 2):
                pair = jnp.where(row_half == 0, ot[:, 2 * c * BAND:(2 * c + 1) * BAND],
                                 ot[:, (2 * c + 1) * BAND:(2 * c + 2) * BAND])
                col = hk * (group // 2) + c
                o_ref[rows, col * LANES:(col + 1) * LANES] = pair.T.astype(BF16)


def _band_bias_table():
    kj = jnp.arange(3 * BAND)[:, None] - BAND
    dist = jnp.abs(kj - jnp.arange(BAND)[None, :])
    per_head = [jnp.where(dist <= BAND, -(2.0 ** -(h + 1)) * dist.astype(F32), NEG_INF)
                for h in range(B_Q_HEADS)]
    group = B_Q_HEADS // B_KV_HEADS
    return jnp.stack([jnp.concatenate(per_head[g * group:(g + 1) * group], axis=1)
                      for g in range(B_KV_HEADS)])


def _attn_b(sinks, qb, kb, vbt, bias):
    b, s, _ = qb.shape
    nb = s // BAND
    nq = B_STEP_BLOCKS
    rows = nq * BAND
    prev = lambda j: jnp.maximum(j * nq - 1, 0)
    nxt = lambda j: jnp.minimum(j * nq + nq, nb - 1)
    k_edge = lambda f: pl.BlockSpec((None, BAND, 2 * LANES), lambda bi, j: (bi, f(j), 0))
    v_edge = lambda f: pl.BlockSpec((None, 2 * LANES, BAND), lambda bi, j: (bi, 0, f(j)))
    k_own = pl.BlockSpec((None, rows, 2 * LANES), lambda bi, j: (bi, j, 0))
    v_own = pl.BlockSpec((None, 2 * LANES, rows), lambda bi, j: (bi, 0, j))
    return pl.pallas_call(
        _attn_b_kernel,
        grid=(b, nb // nq),
        in_specs=[pl.BlockSpec(memory_space=pltpu.SMEM),
                  pl.BlockSpec((None, rows, MIX_WIDTH), lambda bi, j: (bi, j, 0)),
                  k_edge(prev), k_own, k_edge(nxt), v_edge(prev), v_own, v_edge(nxt),
                  _full(bias.shape)],
        out_specs=pl.BlockSpec((None, rows, MIX_WIDTH), lambda bi, j: (bi, j, 0)),
        out_shape=jax.ShapeDtypeStruct((b, s, MIX_WIDTH), BF16),
        compiler_params=_params(),
        name="attn_b",
    )(sinks, qb, kb, kb, kb, vbt, vbt, vbt, bias)


def _merge_kernel(x_ref, g_ref, oa_ref, ob_ref, oc_ref, wg_ref, bg_ref, wb_ref, wo_ref, o_ref):
    x = x_ref[...]
    d = x.shape[1]
    h = _rms(x, g_ref[...]).astype(BF16)
    merged = jnp.zeros(x.shape, F32)
    for n, br_ref in enumerate((oa_ref, ob_ref, oc_ref)):
        z = _dot(h, wg_ref[:, n * d:(n + 1) * d]) + bg_ref[:, n * d:(n + 1) * d]
        gate = 1.0 / (1.0 + jnp.exp(-z))
        merged = merged + gate * _dot(br_ref[...], wb_ref[n])
    o_ref[...] = x + _dot(merged.astype(BF16), wo_ref[...])


def _merge(x, g, oa, ob, oc, wg, bg, wb, wo):
    b, s, d = x.shape
    tm = TOKEN_TILE
    row = lambda c: pl.BlockSpec((None, tm, c), lambda bi, i: (bi, i, 0))
    return pl.pallas_call(
        _merge_kernel,
        grid=(b, s // tm),
        in_specs=[row(d), _full(g.shape), row(MIX_WIDTH), row(MIX_WIDTH), row(MIX_WIDTH),
                  _full(wg.shape), _full(bg.shape), _full(wb.shape), _full(wo.shape)],
        out_specs=row(d),
        out_shape=jax.ShapeDtypeStruct(x.shape, F32),
        compiler_params=_params(),
        name="merge",
    )(x, g, oa, ob, oc, wg, bg, wb, wo)


def _prep_w_in(w_in):
    dup = lambda w: jnp.concatenate([w[:, 0:64], w[:, 0:64], w[:, 64:128], w[:, 64:128]], axis=1)
    kr = w_in[:, 2944:3008]
    return jnp.concatenate([
        w_in[:, 0:2048], dup(w_in[:, 2048:2176]), dup(w_in[:, 2176:2304]), w_in[:, 2304:2944],
        kr, kr[:, 32:64], kr[:, 0:32]], axis=1).astype(BF16)


def _prep_w_uq(w_uq):
    pe = w_uq[:, :, C_NOPE:]
    half = C_ROPE // 2
    w = jnp.concatenate([w_uq, pe[:, :, half:], pe[:, :, :half]], axis=2)
    return w.reshape(w_uq.shape[0], C_HEADS * 2 * LANES).astype(BF16)


def _alibi_key_table(c_alibi):
    term = c_alibi[None, :] * jnp.arange(K_TILE, dtype=F32)[:, None]
    pieces = []
    for _ in range(ALIBI_TERMS):
        piece = term.astype(BF16)
        pieces.append(piece)
        term = term - piece.astype(F32)
    tab = jnp.stack(pieces, axis=-1)
    tab = jnp.pad(tab, ((0, 0), (0, 0), (0, LANES - ALIBI_TERMS)))
    return tab.reshape(K_TILE, -1)


def kernel(x, ffn1_norm, ffn1_w13, ffn1_w2, mix_norm, w_in, w_gate, b_gate, a_lambda, a_subln,
           b_sinks, c_q_norm, c_w_uq, c_kv_norm, c_w_ukv, w_branch, w_out, ffn2_norm, ffn2_w13,
           ffn2_w2, final_norm):
    b, s, d = x.shape
    depth = w_in.shape[0]
    assert s % K_TILE == 0 and s % C_K_TILE == 0 and K_TILE % TOKEN_TILE == 0 and K_TILE % A_Q_TILE == 0 and d % LANES == 0

    pos = jnp.arange(s, dtype=F32)
    inv_freq = ROPE_THETA ** (-jnp.arange(0, C_ROPE, 2, dtype=F32) / C_ROPE)
    ang = pos[:, None] * inv_freq[None, :]
    cos, sin = jnp.cos(ang), jnp.sin(ang)
    pad = jnp.zeros((s, LANES - C_ROPE), F32)
    cos2 = jnp.concatenate([cos, cos, pad], axis=1)
    sin2 = jnp.concatenate([-sin, sin, pad], axis=1)
    c_alibi = 2.0 ** (-8.0 * jnp.arange(1, A_HEADS + 1, dtype=F32) / A_HEADS) * LOG2E
    alibi = _alibi_key_table(c_alibi)
    band_bias = _band_bias_table()
    row = lambda v: v.reshape(1, -1)
    gf = row(final_norm)

    for l in range(depth):
        x = _ffn(x, row(ffn1_norm[l]), ffn1_w13[l].astype(BF16), ffn1_w2[l].astype(BF16), gf, False)

        wkv = c_w_ukv[l]
        qa, ka, vat, qb, kb, vb, qc, kc, vct = _mix_in(
            x, row(mix_norm[l]), _prep_w_in(w_in[l]), row(c_q_norm[l]), _prep_w_uq(c_w_uq[l]),
            row(c_kv_norm[l]),
            wkv[:, :, :C_NOPE].reshape(C_KV_RANK, -1).astype(BF16),
            wkv[:, :, C_NOPE:].reshape(C_KV_RANK, -1).astype(BF16),
            cos2, sin2, alibi)

        lam_init = jnp.full((1, 1), 0.8 - 0.6 * math.exp(-0.3 * l), F32)
        oa = _attn_a(c_alibi, qa, ka, vat, a_lambda[l], lam_init, a_subln[l].reshape(-1, 1))
        ob = _attn_b(b_sinks[l], qb, kb, vb, band_bias)
        oc = _attn_c(qc, kc, vct)

        x = _merge(x, row(mix_norm[l]), oa, ob, oc, w_gate[l].astype(BF16), row(b_gate[l]),
                   w_branch[l].astype(BF16), w_out[l].astype(BF16))
        x = _ffn(x, row(ffn2_norm[l]), ffn2_w13[l].astype(BF16), ffn2_w2[l].astype(BF16), gf,
                 l == depth - 1)
    return x
```

```python
import functools
import math

import jax
import jax.numpy as jnp
from jax import lax
from jax.experimental import pallas as pl
from jax.experimental.pallas import tpu as pltpu

BF16 = jnp.bfloat16
F32 = jnp.float32

EPS = 1e-6
NEG_INF = -1e30
LOG2E = math.log2(math.e)
ROPE_THETA = 10000.0

A_HEADS = 4
A_QK_DIM = 64
B_Q_HEADS = 8
B_KV_HEADS = 2
B_HEAD_DIM = 64
BAND = 128
B_STEP_BLOCKS = 4
C_HEADS = 4
C_Q_RANK = 384
C_KV_RANK = 256
C_NOPE = 128
C_ROPE = 64
MIX_WIDTH = 512
LANES = 128

VMEM_LIMIT = 52 * 1024 * 1024

TOKEN_TILE = 512
MXU_DIM = 256
FFN_CHUNK = 6 * MXU_DIM
K_TILE = 512
C_K_TILE = 1024
Q_TILE = 1024
A_Q_TILE = 512
V_ROWS = LANES + 16
ACC_ROWS = LANES + 8
ALIBI_TERMS = 3


def _params(**flags):
    return pltpu.CompilerParams(vmem_limit_bytes=VMEM_LIMIT, flags=flags or None)


def _rms(x, g):
    return x * lax.rsqrt(jnp.mean(x * x, axis=-1, keepdims=True) + EPS) * g


def _dot(a, b):
    return jnp.dot(a, b, preferred_element_type=F32)


def _dot_nt(a, b):
    return lax.dot_general(a, b, (((1,), (1,)), ((), ())), preferred_element_type=F32)


def _full(shape):
    return pl.BlockSpec(shape, lambda *_: (0,) * len(shape))


def _layer(stack, l):
    return pl.BlockSpec((None,) + stack.shape[1:], lambda *_: (l,) + (0,) * (stack.ndim - 1),
                        pipeline_mode=pl.Buffered(1))


def _ffn_kernel(x_ref, g_ref, w13_ref, w2_ref, gf_ref, o_ref, *, final):
    x = x_ref[...]
    h = _rms(x, g_ref[...]).astype(BF16)
    d_ff = w2_ref.shape[0]
    acc = jnp.zeros(x.shape, F32)
    for lo in range(0, d_ff, FFN_CHUNK):
        hi = min(lo + FFN_CHUNK, d_ff)
        a = _dot(h, w13_ref[:, lo:hi])
        g = _dot(h, w13_ref[:, d_ff + lo:d_ff + hi])
        act = (a * (1.0 / (1.0 + jnp.exp(-a))) * g).astype(BF16)
        acc = acc + _dot(act, w2_ref[lo:hi, :])
    y = x + 0.5 * acc
    if final:
        y = _rms(y, gf_ref[...])
    o_ref[...] = y


def _ffn(x, g, w13, w2, gf, l, final):
    b, s, d = x.shape
    tm = TOKEN_TILE
    xspec = pl.BlockSpec((None, tm, d), lambda bi, i: (bi, i, 0))
    return pl.pallas_call(
        functools.partial(_ffn_kernel, final=final),
        grid=(b, s // tm),
        in_specs=[xspec, _full(g.shape), _layer(w13, l), _layer(w2, l), _full(gf.shape)],
        out_specs=xspec,
        out_shape=jax.ShapeDtypeStruct(x.shape, F32),
        compiler_params=_params(),
        name="ffn",
    )(x, g, w13, w2, gf)


def _store_values_t(vt_ref, v, heads):
    vt = v.T.astype(BF16)
    ones = jnp.ones((V_ROWS - LANES, vt.shape[1]), BF16)
    for hd in range(heads):
        vt_ref[hd * V_ROWS:hd * V_ROWS + LANES, :] = vt[hd * LANES:(hd + 1) * LANES]
        vt_ref[hd * V_ROWS + LANES:(hd + 1) * V_ROWS, :] = ones


def _mix_in_kernel(x_ref, g_ref, w_ref, cqg_ref, wq_ref, ckvg_ref, wkn_ref, wv_ref,
                   cos_ref, sin_ref, alibi_ref,
                   qa_ref, ka_ref, vat_ref, qb_ref, kb_ref, vbt_ref, qc_ref, kc_ref, vct_ref):
    h = _rms(x_ref[...], g_ref[...]).astype(BF16)
    cos2 = cos_ref[...]
    sin2 = sin_ref[...]

    za = _dot(h, w_ref[:, 0:1536])
    qa_ref[...] = (za[:, 0:512] * (A_QK_DIM ** -0.5 * LOG2E)).astype(BF16)
    for hd in range(A_HEADS):
        ka_ref[:, 2 * hd * LANES:(2 * hd + 1) * LANES] = (
            za[:, 512 + hd * LANES:512 + (hd + 1) * LANES].astype(BF16))
        ka_ref[:, (2 * hd + 1) * LANES:(2 * hd + 2) * LANES] = alibi_ref[:, hd * LANES:(hd + 1) * LANES]
    _store_values_t(vat_ref, za[:, 1024:1536], A_HEADS)

    zb = _dot(h, w_ref[:, 1536:2560])
    qb_ref[...] = (zb[:, 0:512] * (B_HEAD_DIM ** -0.5)).astype(BF16)
    kb_ref[...] = zb[:, 512:768].astype(BF16)
    vbt_ref[...] = zb[:, 768:1024].T.astype(BF16)

    zc = _dot(h, w_ref[:, 2560:3328])
    c_scale = (C_NOPE + C_ROPE) ** -0.5 * LOG2E
    cqn = _rms(zc[:, 0:C_Q_RANK], cqg_ref[...]).astype(BF16)
    q = _dot(cqn, wq_ref[...])
    for hd in range(C_HEADS):
        lo = hd * 2 * LANES
        qc_ref[:, lo:lo + LANES] = (q[:, lo:lo + LANES] * c_scale).astype(BF16)
        qp = q[:, lo + LANES:lo + 2 * LANES]
        qpr = qp * cos2 + pltpu.roll(qp, 64, 1) * sin2
        qc_ref[:, lo + LANES:lo + 2 * LANES] = (qpr * c_scale).astype(BF16)
    ckvn = _rms(zc[:, C_Q_RANK:C_Q_RANK + C_KV_RANK], ckvg_ref[...]).astype(BF16)
    kc_ref[:, 0:512] = _dot(ckvn, wkn_ref[...]).astype(BF16)
    kr = zc[:, C_Q_RANK + C_KV_RANK:]
    kc_ref[:, 512:640] = (kr * cos2 + pltpu.roll(kr, 64, 1) * sin2).astype(BF16)
    _store_values_t(vct_ref, _dot(ckvn, wv_ref[...]), C_HEADS)


def _mix_in(x, g, w, cqg, wq, ckvg, wkn, wv, cos2, sin2, alibi, l):
    b, s, d = x.shape
    tm = TOKEN_TILE
    row = lambda c: pl.BlockSpec((None, tm, c), lambda bi, i: (bi, i, 0))
    colT = pl.BlockSpec((None, 4 * V_ROWS, tm), lambda bi, i: (bi, 0, i))
    tab = pl.BlockSpec((tm, LANES), lambda bi, i: (i, 0))
    sd = lambda c: jax.ShapeDtypeStruct((b, s, c), BF16)
    sdT = jax.ShapeDtypeStruct((b, 4 * V_ROWS, s), BF16)
    return pl.pallas_call(
        _mix_in_kernel,
        grid=(b, s // tm),
        in_specs=[row(d), _full(g.shape), _layer(w, l), _full(cqg.shape), _layer(wq, l),
                  _full(ckvg.shape), _layer(wkn, l), _layer(wv, l), tab, tab,
                  pl.BlockSpec((tm, alibi.shape[1]), lambda bi, i: (i % (K_TILE // tm), 0))],
        out_specs=[row(512), row(1024), colT, row(512), row(256),
                   pl.BlockSpec((None, 2 * LANES, tm), lambda bi, i: (bi, 0, i)),
                   row(1024), row(640), colT],
        out_shape=[sd(512), sd(1024), sdT, sd(512), sd(256),
                   jax.ShapeDtypeStruct((b, 2 * LANES, s), BF16), sd(1024), sd(640), sdT],
        compiler_params=_params(),
        name="mix_in",
    )(x, g, w, cqg, wq, ckvg, wkn, wv, cos2, sin2, alibi)


OVERFLOW_GUARD = 1e37
PROBE_KEYS = 128


def _value_product(vt, p):
    return _dot(vt, p)[:ACC_ROWS]


def _tile_stats(st, vt):
    m_t = jnp.max(st, axis=0, keepdims=True)
    return m_t, _value_product(vt, jnp.exp2(st - m_t).astype(BF16))


def _flash_fixed_reference(n, qk_fn, v_fn, ref_fn, acc, p_refs):
    def weights(t):
        p_refs[t % 2][...] = jnp.exp2(qk_fn(t) - ref_fn(t)).astype(BF16)

    if n:
        weights(0)
    for t in range(n):
        if t + 1 < n:
            weights(t + 1)
        acc = acc + _value_product(v_fn(t), p_refs[t % 2][...])
    return acc


def _flash_running_max(n, score_fn, v_fn, width):
    def body(kt, carry):
        m, acc = carry
        m_t, pv_t = _tile_stats(score_fn(kt), v_fn(kt))
        m_new = jnp.maximum(m, m_t)
        return m_new, jnp.exp2(m - m_new) * acc + jnp.exp2(m_t - m_new) * pv_t

    init = (jnp.full((1, width), -jnp.inf, F32), jnp.zeros((ACC_ROWS, width), F32))
    return lax.fori_loop(0, n, body, init)[1]


def _denominator_overflowed(acc):
    l = acc[LANES:LANES + 1]
    return jnp.max(jnp.where(l < OVERFLOW_GUARD, 0.0, 1.0)) > 0.5


def _attn_a_kernel(c_ref, q_ref, k_ref, vt_ref, lam_ref, lam_init_ref, g_ref, o_ref,
                   qq_ref, p0_ref, p1_ref):
    hd = pl.program_id(1)
    qi = pl.program_id(2)
    tq = q_ref.shape[0]
    n_kt = k_ref.shape[0] // K_TILE
    c = c_ref[hd]

    q = q_ref[...]
    lane = lax.broadcasted_iota(jnp.int32, q.shape, 1)
    zero = jnp.zeros_like(q)
    qm = jnp.concatenate([jnp.where(lane < A_QK_DIM, q, zero),
                          jnp.where(lane >= A_QK_DIM, q, zero)], axis=0)
    sel = (lax.broadcasted_iota(jnp.int32, qm.shape, 1) < ALIBI_TERMS).astype(BF16)
    for slot, sign in enumerate((-1.0, 1.0, 0.0)):
        qq_ref[slot, :, 0:LANES] = qm
        qq_ref[slot, :, LANES:2 * LANES] = sel * sign

    def keys(kt):
        return k_ref[pl.ds(pl.multiple_of(kt * K_TILE, K_TILE), K_TILE), :]

    def values(kt):
        return vt_ref[:, pl.ds(pl.multiple_of(kt * K_TILE, K_TILE), K_TILE)]

    q0 = qi * tq
    rel = (lax.broadcasted_iota(jnp.int32, (K_TILE, tq), 0)
           - lax.broadcasted_iota(jnp.int32, (K_TILE, tq), 1))

    def biased_scores(kt):
        dist = jnp.abs(rel + (kt * K_TILE - q0)).astype(F32) * c
        return _dot_nt(keys(kt), qq_ref[2]) - jnp.concatenate([dist, dist], axis=1)

    def finish(acc):
        lp = lam_ref[...]
        lam_init = lam_init_ref[...]
        lam = (jnp.exp(jnp.sum(lp[0:1] * lp[1:2], axis=1, keepdims=True))
               - jnp.exp(jnp.sum(lp[2:3] * lp[3:4], axis=1, keepdims=True)) + lam_init)
        inv_l = 1.0 / acc[LANES:LANES + 1]
        o = acc[:LANES, :tq] * inv_l[:, :tq] - lam * (acc[:LANES, tq:] * inv_l[:, tq:])
        o = o * lax.rsqrt(jnp.mean(o * o, axis=0, keepdims=True) + EPS) * g_ref[...]
        o = o * (1.0 - lam_init)
        o_ref[...] = o.T.astype(BF16)

    own = k_ref[pl.ds(pl.multiple_of(q0, tq), tq), :]
    dist = jnp.abs(rel[:tq]).astype(F32) * c
    m0 = jnp.max(_dot_nt(own, qq_ref[2]) - jnp.concatenate([dist, dist], axis=1),
                 axis=0, keepdims=True)

    kd = q0 // K_TILE
    q_row = lax.broadcasted_iota(jnp.int32, (1, tq), 1).astype(F32) * c
    q_row = jnp.concatenate([q_row, q_row], axis=1)

    def key_tile(t):
        t = jnp.int32(t - 1)
        kt = t + (t >= kd).astype(jnp.int32)
        return kt, (kt < kd).astype(jnp.int32)

    def qk_fn(t):
        if t == 0:
            return biased_scores(kd)
        kt, before = key_tile(t)
        return _dot_nt(keys(kt), qq_ref[before])

    def v_fn(t):
        return values(kd if t == 0 else key_tile(t)[0])

    def ref_fn(t):
        if t == 0:
            return m0
        kt, before = key_tile(t)
        sgn = (2 * before - 1).astype(F32)
        return m0 - ((kt * K_TILE - q0).astype(F32) * c - q_row) * sgn

    acc = _flash_fixed_reference(n_kt, qk_fn, v_fn, ref_fn, jnp.zeros((ACC_ROWS, 2 * tq), F32),
                                 (p0_ref, p1_ref))
    finish(acc)

    @pl.when(_denominator_overflowed(acc))
    def _():
        finish(_flash_running_max(n_kt, biased_scores, values, 2 * tq))


def _attn_a(c_alibi, qa, ka, vat, lam_p, lam_init, g_col):
    b, s, _ = qa.shape
    tq = A_Q_TILE
    return pl.pallas_call(
        _attn_a_kernel,
        grid=(b, A_HEADS, s // tq),
        in_specs=[
            pl.BlockSpec(memory_space=pltpu.SMEM),
            pl.BlockSpec((None, tq, LANES), lambda bi, h, i: (bi, i, h)),
            pl.BlockSpec((None, s, 2 * LANES), lambda bi, h, i: (bi, 0, h)),
            pl.BlockSpec((None, V_ROWS, s), lambda bi, h, i: (bi, h, 0)),
            _full(lam_p.shape), _full(lam_init.shape), _full(g_col.shape),
        ],
        out_specs=pl.BlockSpec((None, tq, LANES), lambda bi, h, i: (bi, i, h)),
        out_shape=jax.ShapeDtypeStruct((b, s, MIX_WIDTH), BF16),
        scratch_shapes=[pltpu.VMEM((3, 2 * tq, 2 * LANES), BF16),
                        pltpu.VMEM((K_TILE, 2 * tq), BF16), pltpu.VMEM((K_TILE, 2 * tq), BF16)],
        compiler_params=_params(),
        name="attn_a",
    )(c_alibi, qa, ka, vat, lam_p, lam_init, g_col)


def _attn_c_kernel(q_ref, kn_ref, kpe_ref, vt_ref, o_ref, p0_ref, p1_ref):
    q = q_ref[...]
    tq = q.shape[0]
    tk = C_K_TILE
    n_kt = kn_ref.shape[0] // tk

    def scores(kt):
        k0 = pl.multiple_of(kt * tk, tk)
        k = jnp.concatenate([kn_ref[pl.ds(k0, tk), :], kpe_ref[pl.ds(k0, tk), :]], axis=1)
        return _dot_nt(k, q)

    def values(kt):
        return vt_ref[:, pl.ds(pl.multiple_of(kt * tk, tk), tk)]

    def finish(acc):
        o_ref[...] = (acc[:LANES] * (1.0 / acc[LANES:LANES + 1])).T.astype(BF16)

    probe = jnp.concatenate([kn_ref[0:PROBE_KEYS, :], kpe_ref[0:PROBE_KEYS, :]], axis=1)
    m0 = jnp.max(_dot_nt(probe, q), axis=0, keepdims=True)
    acc = _flash_fixed_reference(n_kt, lambda t: scores(jnp.int32(t)), lambda t: values(jnp.int32(t)),
                                 lambda t: m0, jnp.zeros((ACC_ROWS, tq), F32), (p0_ref, p1_ref))
    finish(acc)

    @pl.when(_denominator_overflowed(acc))
    def _():
        finish(_flash_running_max(n_kt, scores, values, tq))


def _attn_c(qc, kc, vct):
    b, s, _ = qc.shape
    tq = Q_TILE
    return pl.pallas_call(
        _attn_c_kernel,
        grid=(b, C_HEADS, s // tq),
        in_specs=[
            pl.BlockSpec((None, tq, 2 * LANES), lambda bi, h, i: (bi, i, h)),
            pl.BlockSpec((None, s, LANES), lambda bi, h, i: (bi, 0, h)),
            pl.BlockSpec((None, s, LANES), lambda bi, h, i: (bi, 0, C_HEADS)),
            pl.BlockSpec((None, V_ROWS, s), lambda bi, h, i: (bi, h, 0)),
        ],
        out_specs=pl.BlockSpec((None, tq, LANES), lambda bi, h, i: (bi, i, h)),
        out_shape=jax.ShapeDtypeStruct((b, s, MIX_WIDTH), BF16),
        scratch_shapes=[pltpu.VMEM((C_K_TILE, tq), BF16), pltpu.VMEM((C_K_TILE, tq), BF16)],
        compiler_params=_params(),
        name="attn_c",
    )(qc, kc, kc, vct)


def _attn_b_kernel(sinks_ref, q_ref, kp_ref, ko_ref, kx_ref, vp_ref, vo_ref, vx_ref, bias_ref, o_ref):
    first = pl.program_id(1) * B_STEP_BLOCKS
    nb = pl.num_programs(1) * B_STEP_BLOCKS
    group = B_Q_HEADS // B_KV_HEADS
    key_row = lax.broadcasted_iota(jnp.int32, (3 * BAND, 1), 0) - BAND
    half = lax.broadcasted_iota(jnp.int32, (BAND, LANES), 1) // B_HEAD_DIM
    row_half = lax.broadcasted_iota(jnp.int32, (LANES, BAND), 0) // B_HEAD_DIM

    for hk in range(B_KV_HEADS):
        heads = range(hk * group, (hk + 1) * group)
        ksl = slice(hk * LANES, (hk + 1) * LANES)
        k_all = jnp.concatenate([kp_ref[:, ksl], ko_ref[:, ksl], kx_ref[:, ksl]], axis=0)
        vt_all = jnp.concatenate([vp_ref[ksl, :], vo_ref[ksl, :], vx_ref[ksl, :]], axis=1)
        sink = jnp.concatenate([jnp.full((1, BAND), sinks_ref[h], F32) for h in heads], axis=1)
        for u in range(B_STEP_BLOCKS):
            rows = slice(u * BAND, (u + 1) * BAND)
            qg = []
            for h in heads:
                qh = q_ref[rows, (h // 2) * LANES:(h // 2 + 1) * LANES]
                qg.append(jnp.where(half == h % 2, qh, jnp.zeros_like(qh)))
            kpos = (first + u) * BAND + key_row
            edge = jnp.where((kpos >= 0) & (kpos < nb * BAND), 0.0, NEG_INF)
            sc = _dot_nt(k_all[u * BAND:(u + 3) * BAND], jnp.concatenate(qg, axis=0))
            sc = jnp.maximum(sc + bias_ref[hk] + edge, NEG_INF)
            m = jnp.maximum(jnp.max(sc, axis=0, keepdims=True), sink)
            e = jnp.exp(sc - m)
            den = jnp.sum(e, axis=0, keepdims=True) + jnp.exp(sink - m)
            ot = _dot(vt_all[:, u * BAND:(u + 3) * BAND], e.astype(BF16)) * (1.0 / den)
            for c in range(group // 2):
                pair = jnp.where(row_half == 0, ot[:, 2 * c * BAND:(2 * c + 1) * BAND],
                                 ot[:, (2 * c + 1) * BAND:(2 * c + 2) * BAND])
                col = hk * (group // 2) + c
                o_ref[rows, col * LANES:(col + 1) * LANES] = pair.T.astype(BF16)


def _band_bias_table():
    kj = jnp.arange(3 * BAND)[:, None] - BAND
    dist = jnp.abs(kj - jnp.arange(BAND)[None, :])
    per_head = [jnp.where(dist <= BAND, -(2.0 ** -(h + 1)) * dist.astype(F32), NEG_INF)
                for h in range(B_Q_HEADS)]
    group = B_Q_HEADS // B_KV_HEADS
    return jnp.stack([jnp.concatenate(per_head[g * group:(g + 1) * group], axis=1)
                      for g in range(B_KV_HEADS)])


def _attn_b(sinks, qb, kb, vbt, bias):
    b, s, _ = qb.shape
    nb = s // BAND
    nq = B_STEP_BLOCKS
    rows = nq * BAND
    prev = lambda j: jnp.maximum(j * nq - 1, 0)
    nxt = lambda j: jnp.minimum(j * nq + nq, nb - 1)
    k_edge = lambda f: pl.BlockSpec((None, BAND, 2 * LANES), lambda bi, j: (bi, f(j), 0))
    v_edge = lambda f: pl.BlockSpec((None, 2 * LANES, BAND), lambda bi, j: (bi, 0, f(j)))
    k_own = pl.BlockSpec((None, rows, 2 * LANES), lambda bi, j: (bi, j, 0))
    v_own = pl.BlockSpec((None, 2 * LANES, rows), lambda bi, j: (bi, 0, j))
    return pl.pallas_call(
        _attn_b_kernel,
        grid=(b, nb // nq),
        in_specs=[pl.BlockSpec(memory_space=pltpu.SMEM),
                  pl.BlockSpec((None, rows, MIX_WIDTH), lambda bi, j: (bi, j, 0)),
                  k_edge(prev), k_own, k_edge(nxt), v_edge(prev), v_own, v_edge(nxt),
                  _full(bias.shape)],
        out_specs=pl.BlockSpec((None, rows, MIX_WIDTH), lambda bi, j: (bi, j, 0)),
        out_shape=jax.ShapeDtypeStruct((b, s, MIX_WIDTH), BF16),
        compiler_params=_params(),
        name="attn_b",
    )(sinks, qb, kb, kb, kb, vbt, vbt, vbt, bias)


def _merge_kernel(x_ref, g_ref, oa_ref, ob_ref, oc_ref, wg_ref, bg_ref, wb_ref, wo_ref, o_ref):
    x = x_ref[...]
    d = x.shape[1]
    h = _rms(x, g_ref[...]).astype(BF16)
    merged = jnp.zeros(x.shape, F32)
    for n, br_ref in enumerate((oa_ref, ob_ref, oc_ref)):
        z = _dot(h, wg_ref[:, n * d:(n + 1) * d]) + bg_ref[:, n * d:(n + 1) * d]
        gate = 1.0 / (1.0 + jnp.exp(-z))
        merged = merged + gate * _dot(br_ref[...], wb_ref[n])
    o_ref[...] = x + _dot(merged.astype(BF16), wo_ref[...])


def _merge(x, g, oa, ob, oc, wg, bg, wb, wo, l):
    b, s, d = x.shape
    tm = TOKEN_TILE
    row = lambda c: pl.BlockSpec((None, tm, c), lambda bi, i: (bi, i, 0))
    return pl.pallas_call(
        _merge_kernel,
        grid=(b, s // tm),
        in_specs=[row(d), _full(g.shape), row(MIX_WIDTH), row(MIX_WIDTH), row(MIX_WIDTH),
                  _layer(wg, l), _full(bg.shape), _layer(wb, l), _layer(wo, l)],
        out_specs=row(d),
        out_shape=jax.ShapeDtypeStruct(x.shape, F32),
        compiler_params=_params(),
        name="merge",
    )(x, g, oa, ob, oc, wg, bg, wb, wo)


def _prep_w_in(w_in):
    dup = lambda w: jnp.concatenate([w[..., 0:64], w[..., 0:64], w[..., 64:128], w[..., 64:128]], axis=-1)
    kr = w_in[..., 2944:3008]
    return jnp.concatenate([
        w_in[..., 0:2048], dup(w_in[..., 2048:2176]), dup(w_in[..., 2176:2304]), w_in[..., 2304:2944],
        kr, kr[..., 32:64], kr[..., 0:32]], axis=-1).astype(BF16)


def _prep_w_uq(w_uq):
    pe = w_uq[..., C_NOPE:]
    half = C_ROPE // 2
    w = jnp.concatenate([w_uq, pe[..., half:], pe[..., :half]], axis=-1)
    return w.reshape(w_uq.shape[0], w_uq.shape[1], C_HEADS * 2 * LANES).astype(BF16)


def _alibi_key_table(c_alibi):
    term = c_alibi[None, :] * jnp.arange(K_TILE, dtype=F32)[:, None]
    pieces = []
    for _ in range(ALIBI_TERMS):
        piece = term.astype(BF16)
        pieces.append(piece)
        term = term - piece.astype(F32)
    tab = jnp.stack(pieces, axis=-1)
    tab = jnp.pad(tab, ((0, 0), (0, 0), (0, LANES - ALIBI_TERMS)))
    return tab.reshape(K_TILE, -1)


def kernel(x, ffn1_norm, ffn1_w13, ffn1_w2, mix_norm, w_in, w_gate, b_gate, a_lambda, a_subln,
           b_sinks, c_q_norm, c_w_uq, c_kv_norm, c_w_ukv, w_branch, w_out, ffn2_norm, ffn2_w13,
           ffn2_w2, final_norm):
    b, s, d = x.shape
    depth = w_in.shape[0]
    assert s % K_TILE == 0 and s % C_K_TILE == 0 and K_TILE % TOKEN_TILE == 0 and K_TILE % A_Q_TILE == 0 and d % LANES == 0

    pos = jnp.arange(s, dtype=F32)
    inv_freq = ROPE_THETA ** (-jnp.arange(0, C_ROPE, 2, dtype=F32) / C_ROPE)
    ang = pos[:, None] * inv_freq[None, :]
    cos, sin = jnp.cos(ang), jnp.sin(ang)
    pad = jnp.zeros((s, LANES - C_ROPE), F32)
    cos2 = jnp.concatenate([cos, cos, pad], axis=1)
    sin2 = jnp.concatenate([-sin, sin, pad], axis=1)
    c_alibi = 2.0 ** (-8.0 * jnp.arange(1, A_HEADS + 1, dtype=F32) / A_HEADS) * LOG2E
    alibi = _alibi_key_table(c_alibi)
    band_bias = _band_bias_table()
    row = lambda v: v.reshape(1, -1)
    gf = row(final_norm)

    bf = lambda w: w.astype(BF16)
    w13_1, w2_1, w13_2, w2_2 = bf(ffn1_w13), bf(ffn1_w2), bf(ffn2_w13), bf(ffn2_w2)
    w_in_x, w_uq_x = _prep_w_in(w_in), _prep_w_uq(c_w_uq)
    w_kn = bf(c_w_ukv[..., :C_NOPE].reshape(depth, C_KV_RANK, -1))
    w_v = bf(c_w_ukv[..., C_NOPE:].reshape(depth, C_KV_RANK, -1))
    w_g, w_b, w_o = bf(w_gate), bf(w_branch), bf(w_out)

    for l in range(depth):
        x = _ffn(x, row(ffn1_norm[l]), w13_1, w2_1, gf, l, False)

        qa, ka, vat, qb, kb, vb, qc, kc, vct = _mix_in(
            x, row(mix_norm[l]), w_in_x, row(c_q_norm[l]), w_uq_x, row(c_kv_norm[l]), w_kn, w_v,
            cos2, sin2, alibi, l)

        lam_init = jnp.full((1, 1), 0.8 - 0.6 * math.exp(-0.3 * l), F32)
        oa = _attn_a(c_alibi, qa, ka, vat, a_lambda[l], lam_init, a_subln[l].reshape(-1, 1))
        ob = _attn_b(b_sinks[l], qb, kb, vb, band_bias)
        oc = _attn_c(qc, kc, vct)

        x = _merge(x, row(mix_norm[l]), oa, ob, oc, w_g, row(b_gate[l]), w_b, w_o, l)
        x = _ffn(x, row(ffn2_norm[l]), w13_2, w2_2, gf, l, l == depth - 1)
    return x
```

```python
import functools
import math

import jax
import jax.numpy as jnp
from jax import lax
from jax.experimental import pallas as pl
from jax.experimental.pallas import tpu as pltpu

BF16 = jnp.bfloat16
F32 = jnp.float32

EPS = 1e-6
NEG_INF = -1e30
LOG2E = math.log2(math.e)
ROPE_THETA = 10000.0

A_HEADS = 4
A_QK_DIM = 64
B_Q_HEADS = 8
B_KV_HEADS = 2
B_HEAD_DIM = 64
BAND = 128
B_STEP_BLOCKS = 4
C_HEADS = 4
C_Q_RANK = 384
C_KV_RANK = 256
C_NOPE = 128
C_ROPE = 64
MIX_WIDTH = 512
LANES = 128

VMEM_LIMIT = 52 * 1024 * 1024

TOKEN_TILE = 512
MXU_DIM = 256
FFN_CHUNK = 6 * MXU_DIM
K_TILE = 512
C_K_TILE = 1024
Q_TILE = 2048
A_Q_TILE = 512
V_ROWS = LANES + 16
ACC_ROWS = LANES + 8
ALIBI_TERMS = 3


def _params(**flags):
    return pltpu.CompilerParams(vmem_limit_bytes=VMEM_LIMIT, flags=flags or None)


def _rms(x, g):
    return x * lax.rsqrt(jnp.mean(x * x, axis=-1, keepdims=True) + EPS) * g


def _dot(a, b):
    return jnp.dot(a, b, preferred_element_type=F32)


def _dot_nt(a, b):
    return lax.dot_general(a, b, (((1,), (1,)), ((), ())), preferred_element_type=F32)


def _full(shape):
    return pl.BlockSpec(shape, lambda *_: (0,) * len(shape))


def _layer(stack, l):
    return pl.BlockSpec((None,) + stack.shape[1:], lambda *_: (l,) + (0,) * (stack.ndim - 1),
                        pipeline_mode=pl.Buffered(1))


def _ffn_kernel(x_ref, g_ref, w13_ref, w2_ref, gf_ref, o_ref, *, final):
    x = x_ref[...]
    h = _rms(x, g_ref[...]).astype(BF16)
    d_ff = w2_ref.shape[0]
    acc = jnp.zeros(x.shape, F32)
    for lo in range(0, d_ff, FFN_CHUNK):
        hi = min(lo + FFN_CHUNK, d_ff)
        a = _dot(h, w13_ref[:, lo:hi])
        g = _dot(h, w13_ref[:, d_ff + lo:d_ff + hi])
        act = (a * (1.0 / (1.0 + jnp.exp(-a))) * g).astype(BF16)
        acc = acc + _dot(act, w2_ref[lo:hi, :])
    y = x + 0.5 * acc
    if final:
        y = _rms(y, gf_ref[...])
    o_ref[...] = y


def _ffn(x, g, w13, w2, gf, l, final):
    b, s, d = x.shape
    tm = TOKEN_TILE
    xspec = pl.BlockSpec((None, tm, d), lambda bi, i: (bi, i, 0))
    return pl.pallas_call(
        functools.partial(_ffn_kernel, final=final),
        grid=(b, s // tm),
        in_specs=[xspec, _full(g.shape), _layer(w13, l), _layer(w2, l), _full(gf.shape)],
        out_specs=xspec,
        out_shape=jax.ShapeDtypeStruct(x.shape, F32),
        compiler_params=_params(),
        name="ffn",
    )(x, g, w13, w2, gf)


def _store_values_t(vt_ref, v, heads):
    vt = v.T.astype(BF16)
    ones = jnp.ones((V_ROWS - LANES, vt.shape[1]), BF16)
    for hd in range(heads):
        vt_ref[hd * V_ROWS:hd * V_ROWS + LANES, :] = vt[hd * LANES:(hd + 1) * LANES]
        vt_ref[hd * V_ROWS + LANES:(hd + 1) * V_ROWS, :] = ones


def _mix_in_kernel(x_ref, g_ref, w_ref, cqg_ref, wq_ref, ckvg_ref, wkn_ref, wv_ref,
                   cos_ref, sin_ref, alibi_ref,
                   qa_ref, ka_ref, vat_ref, qb_ref, kb_ref, vbt_ref, qc_ref, kc_ref, vct_ref):
    h = _rms(x_ref[...], g_ref[...]).astype(BF16)
    cos2 = cos_ref[...]
    sin2 = sin_ref[...]

    za = _dot(h, w_ref[:, 0:1536])
    qa_ref[...] = (za[:, 0:512] * (A_QK_DIM ** -0.5 * LOG2E)).astype(BF16)
    for hd in range(A_HEADS):
        ka_ref[:, 2 * hd * LANES:(2 * hd + 1) * LANES] = (
            za[:, 512 + hd * LANES:512 + (hd + 1) * LANES].astype(BF16))
        ka_ref[:, (2 * hd + 1) * LANES:(2 * hd + 2) * LANES] = alibi_ref[:, hd * LANES:(hd + 1) * LANES]
    _store_values_t(vat_ref, za[:, 1024:1536], A_HEADS)

    zb = _dot(h, w_ref[:, 1536:2560])
    qb_ref[...] = (zb[:, 0:512] * (B_HEAD_DIM ** -0.5 * LOG2E)).astype(BF16)
    kb_ref[...] = zb[:, 512:768].astype(BF16)
    vbt_ref[...] = zb[:, 768:1024].T.astype(BF16)

    zc = _dot(h, w_ref[:, 2560:3328])
    c_scale = (C_NOPE + C_ROPE) ** -0.5 * LOG2E
    cqn = _rms(zc[:, 0:C_Q_RANK], cqg_ref[...]).astype(BF16)
    q = _dot(cqn, wq_ref[...])
    for hd in range(C_HEADS):
        lo = hd * 2 * LANES
        qc_ref[:, lo:lo + LANES] = (q[:, lo:lo + LANES] * c_scale).astype(BF16)
        qp = q[:, lo + LANES:lo + 2 * LANES]
        qpr = qp * cos2 + pltpu.roll(qp, 64, 1) * sin2
        qc_ref[:, lo + LANES:lo + 2 * LANES] = (qpr * c_scale).astype(BF16)
    ckvn = _rms(zc[:, C_Q_RANK:C_Q_RANK + C_KV_RANK], ckvg_ref[...]).astype(BF16)
    kc_ref[:, 0:512] = _dot(ckvn, wkn_ref[...]).astype(BF16)
    kr = zc[:, C_Q_RANK + C_KV_RANK:]
    kc_ref[:, 512:640] = (kr * cos2 + pltpu.roll(kr, 64, 1) * sin2).astype(BF16)
    _store_values_t(vct_ref, _dot(ckvn, wv_ref[...]), C_HEADS)


def _mix_in(x, g, w, cqg, wq, ckvg, wkn, wv, cos2, sin2, alibi, l):
    b, s, d = x.shape
    tm = TOKEN_TILE
    row = lambda c: pl.BlockSpec((None, tm, c), lambda bi, i: (bi, i, 0))
    colT = pl.BlockSpec((None, 4 * V_ROWS, tm), lambda bi, i: (bi, 0, i))
    tab = pl.BlockSpec((tm, LANES), lambda bi, i: (i, 0))
    sd = lambda c: jax.ShapeDtypeStruct((b, s, c), BF16)
    sdT = jax.ShapeDtypeStruct((b, 4 * V_ROWS, s), BF16)
    return pl.pallas_call(
        _mix_in_kernel,
        grid=(b, s // tm),
        in_specs=[row(d), _full(g.shape), _layer(w, l), _full(cqg.shape), _layer(wq, l),
                  _full(ckvg.shape), _layer(wkn, l), _layer(wv, l), tab, tab,
                  pl.BlockSpec((tm, alibi.shape[1]), lambda bi, i: (i % (K_TILE // tm), 0))],
        out_specs=[row(512), row(1024), colT, row(512), row(256),
                   pl.BlockSpec((None, 2 * LANES, tm), lambda bi, i: (bi, 0, i)),
                   row(1024), row(640), colT],
        out_shape=[sd(512), sd(1024), sdT, sd(512), sd(256),
                   jax.ShapeDtypeStruct((b, 2 * LANES, s), BF16), sd(1024), sd(640), sdT],
        compiler_params=_params(),
        name="mix_in",
    )(x, g, w, cqg, wq, ckvg, wkn, wv, cos2, sin2, alibi)


OVERFLOW_GUARD = 1e37
PROBE_KEYS = 128
P_BUFFERS = 2


def _value_product(vt, p):
    return _dot(vt, p)[:ACC_ROWS]


def _tile_stats(st, vt):
    m_t = jnp.max(st, axis=0, keepdims=True)
    return m_t, _value_product(vt, jnp.exp2(st - m_t).astype(BF16))


def _flash_fixed_reference(n, qk_fn, v_fn, ref_fn, acc, p_refs):
    def weights(t):
        p_refs[t % len(p_refs)][...] = jnp.exp2(qk_fn(t) - ref_fn(t)).astype(BF16)

    if n:
        weights(0)
    for t in range(n):
        if t + 1 < n:
            weights(t + 1)
        acc = acc + _value_product(v_fn(t), p_refs[t % len(p_refs)][...])
    return acc


def _flash_running_max(n, score_fn, v_fn, width):
    def body(kt, carry):
        m, acc = carry
        m_t, pv_t = _tile_stats(score_fn(kt), v_fn(kt))
        m_new = jnp.maximum(m, m_t)
        return m_new, jnp.exp2(m - m_new) * acc + jnp.exp2(m_t - m_new) * pv_t

    init = (jnp.full((1, width), -jnp.inf, F32), jnp.zeros((ACC_ROWS, width), F32))
    return lax.fori_loop(0, n, body, init)[1]


def _denominator_overflowed(acc):
    l = acc[LANES:LANES + 1]
    return jnp.max(jnp.where(l < OVERFLOW_GUARD, 0.0, 1.0)) > 0.5


def _attn_a_kernel(c_ref, q_ref, k_ref, vt_ref, lam_ref, lam_init_ref, g_ref, o_ref,
                   qq_ref, *p_refs):
    hd = pl.program_id(1)
    qi = pl.program_id(2)
    tq = q_ref.shape[0]
    n_kt = k_ref.shape[0] // K_TILE
    c = c_ref[hd]

    q = q_ref[...]
    lane = lax.broadcasted_iota(jnp.int32, q.shape, 1)
    zero = jnp.zeros_like(q)
    qm = jnp.concatenate([jnp.where(lane < A_QK_DIM, q, zero),
                          jnp.where(lane >= A_QK_DIM, q, zero)], axis=0)
    sel = (lax.broadcasted_iota(jnp.int32, qm.shape, 1) < ALIBI_TERMS).astype(BF16)
    for slot, sign in enumerate((-1.0, 1.0, 0.0)):
        qq_ref[slot, :, 0:LANES] = qm
        qq_ref[slot, :, LANES:2 * LANES] = sel * sign

    def keys(kt):
        return k_ref[pl.ds(pl.multiple_of(kt * K_TILE, K_TILE), K_TILE), :]

    def values(kt):
        return vt_ref[:, pl.ds(pl.multiple_of(kt * K_TILE, K_TILE), K_TILE)]

    q0 = qi * tq
    rel = (lax.broadcasted_iota(jnp.int32, (K_TILE, tq), 0)
           - lax.broadcasted_iota(jnp.int32, (K_TILE, tq), 1))

    def biased_scores(kt):
        dist = jnp.abs(rel + (kt * K_TILE - q0)).astype(F32) * c
        return _dot_nt(keys(kt), qq_ref[2]) - jnp.concatenate([dist, dist], axis=1)

    def finish(acc):
        lp = lam_ref[...]
        lam_init = lam_init_ref[...]
        lam = (jnp.exp(jnp.sum(lp[0:1] * lp[1:2], axis=1, keepdims=True))
               - jnp.exp(jnp.sum(lp[2:3] * lp[3:4], axis=1, keepdims=True)) + lam_init)
        inv_l = 1.0 / acc[LANES:LANES + 1]
        o = acc[:LANES, :tq] * inv_l[:, :tq] - lam * (acc[:LANES, tq:] * inv_l[:, tq:])
        o = o * lax.rsqrt(jnp.mean(o * o, axis=0, keepdims=True) + EPS) * g_ref[...]
        o = o * (1.0 - lam_init)
        o_ref[...] = o.T.astype(BF16)

    own = k_ref[pl.ds(pl.multiple_of(q0, tq), tq), :]
    dist = jnp.abs(rel[:tq]).astype(F32) * c
    m0 = jnp.max(_dot_nt(own, qq_ref[2]) - jnp.concatenate([dist, dist], axis=1),
                 axis=0, keepdims=True)

    kd = q0 // K_TILE
    q_row = lax.broadcasted_iota(jnp.int32, (1, tq), 1).astype(F32) * c
    q_row = jnp.concatenate([q_row, q_row], axis=1)

    def key_tile(t):
        t = jnp.int32(t - 1)
        kt = t + (t >= kd).astype(jnp.int32)
        return kt, (kt < kd).astype(jnp.int32)

    def qk_fn(t):
        if t == 0:
            return biased_scores(kd)
        kt, before = key_tile(t)
        return _dot_nt(keys(kt), qq_ref[before])

    def v_fn(t):
        return values(kd if t == 0 else key_tile(t)[0])

    def ref_fn(t):
        if t == 0:
            return m0
        kt, before = key_tile(t)
        sgn = (2 * before - 1).astype(F32)
        return m0 - ((kt * K_TILE - q0).astype(F32) * c - q_row) * sgn

    acc = _flash_fixed_reference(n_kt, qk_fn, v_fn, ref_fn, jnp.zeros((ACC_ROWS, 2 * tq), F32),
                                 p_refs)
    finish(acc)

    @pl.when(_denominator_overflowed(acc))
    def _():
        finish(_flash_running_max(n_kt, biased_scores, values, 2 * tq))


def _attn_a(c_alibi, qa, ka, vat, lam_p, lam_init, g_col):
    b, s, _ = qa.shape
    tq = A_Q_TILE
    return pl.pallas_call(
        _attn_a_kernel,
        grid=(b, A_HEADS, s // tq),
        in_specs=[
            pl.BlockSpec(memory_space=pltpu.SMEM),
            pl.BlockSpec((None, tq, LANES), lambda bi, h, i: (bi, i, h)),
            pl.BlockSpec((None, s, 2 * LANES), lambda bi, h, i: (bi, 0, h)),
            pl.BlockSpec((None, V_ROWS, s), lambda bi, h, i: (bi, h, 0)),
            _full(lam_p.shape), _full(lam_init.shape), _full(g_col.shape),
        ],
        out_specs=pl.BlockSpec((None, tq, LANES), lambda bi, h, i: (bi, i, h)),
        out_shape=jax.ShapeDtypeStruct((b, s, MIX_WIDTH), BF16),
        scratch_shapes=[pltpu.VMEM((3, 2 * tq, 2 * LANES), BF16),
                       ] + [pltpu.VMEM((K_TILE, 2 * tq), BF16)] * P_BUFFERS,
        compiler_params=_params(),
        name="attn_a",
    )(c_alibi, qa, ka, vat, lam_p, lam_init, g_col)


def _attn_c_kernel(q_ref, kn_ref, kpe_ref, vt_ref, o_ref, *p_refs):
    q = q_ref[...]
    tq = q.shape[0]
    tk = C_K_TILE
    n_kt = kn_ref.shape[0] // tk

    def scores(kt):
        k0 = pl.multiple_of(kt * tk, tk)
        k = jnp.concatenate([kn_ref[pl.ds(k0, tk), :], kpe_ref[pl.ds(k0, tk), :]], axis=1)
        return _dot_nt(k, q)

    def values(kt):
        return vt_ref[:, pl.ds(pl.multiple_of(kt * tk, tk), tk)]

    def finish(acc):
        o_ref[...] = (acc[:LANES] * (1.0 / acc[LANES:LANES + 1])).T.astype(BF16)

    probe = jnp.concatenate([kn_ref[0:PROBE_KEYS, :], kpe_ref[0:PROBE_KEYS, :]], axis=1)
    m0 = jnp.max(_dot_nt(probe, q), axis=0, keepdims=True)
    acc = _flash_fixed_reference(n_kt, lambda t: scores(jnp.int32(t)), lambda t: values(jnp.int32(t)),
                                 lambda t: m0, jnp.zeros((ACC_ROWS, tq), F32), p_refs)
    finish(acc)

    @pl.when(_denominator_overflowed(acc))
    def _():
        finish(_flash_running_max(n_kt, scores, values, tq))


def _attn_c(qc, kc, vct):
    b, s, _ = qc.shape
    tq = Q_TILE
    return pl.pallas_call(
        _attn_c_kernel,
        grid=(b, C_HEADS, s // tq),
        in_specs=[
            pl.BlockSpec((None, tq, 2 * LANES), lambda bi, h, i: (bi, i, h)),
            pl.BlockSpec((None, s, LANES), lambda bi, h, i: (bi, 0, h)),
            pl.BlockSpec((None, s, LANES), lambda bi, h, i: (bi, 0, C_HEADS)),
            pl.BlockSpec((None, V_ROWS, s), lambda bi, h, i: (bi, h, 0)),
        ],
        out_specs=pl.BlockSpec((None, tq, LANES), lambda bi, h, i: (bi, i, h)),
        out_shape=jax.ShapeDtypeStruct((b, s, MIX_WIDTH), BF16),
        scratch_shapes=[pltpu.VMEM((C_K_TILE, tq), BF16)] * P_BUFFERS,
        compiler_params=_params(),
        name="attn_c",
    )(qc, kc, kc, vct)


def _attn_b_kernel(sinks_ref, q_ref, kp_ref, ko_ref, kx_ref, vp_ref, vo_ref, vx_ref, bias_ref, o_ref):
    first = pl.program_id(1) * B_STEP_BLOCKS
    nb = pl.num_programs(1) * B_STEP_BLOCKS
    group = B_Q_HEADS // B_KV_HEADS
    key_row = lax.broadcasted_iota(jnp.int32, (3 * BAND, 1), 0) - BAND
    half = lax.broadcasted_iota(jnp.int32, (BAND, LANES), 1) // B_HEAD_DIM
    row_half = lax.broadcasted_iota(jnp.int32, (LANES, BAND), 0) // B_HEAD_DIM

    for hk in range(B_KV_HEADS):
        heads = range(hk * group, (hk + 1) * group)
        ksl = slice(hk * LANES, (hk + 1) * LANES)
        k_all = jnp.concatenate([kp_ref[:, ksl], ko_ref[:, ksl], kx_ref[:, ksl]], axis=0)
        vt_all = jnp.concatenate([vp_ref[ksl, :], vo_ref[ksl, :], vx_ref[ksl, :]], axis=1)
        sink = jnp.concatenate([jnp.full((1, BAND), sinks_ref[h] * LOG2E, F32) for h in heads], axis=1)
        for u in range(B_STEP_BLOCKS):
            rows = slice(u * BAND, (u + 1) * BAND)
            qg = []
            for h in heads:
                qh = q_ref[rows, (h // 2) * LANES:(h // 2 + 1) * LANES]
                qg.append(jnp.where(half == h % 2, qh, jnp.zeros_like(qh)))
            kpos = (first + u) * BAND + key_row
            edge = jnp.where((kpos >= 0) & (kpos < nb * BAND), 0.0, NEG_INF)
            sc = _dot_nt(k_all[u * BAND:(u + 3) * BAND], jnp.concatenate(qg, axis=0))
            sc = jnp.maximum(sc + bias_ref[hk] + edge, NEG_INF)
            m = jnp.maximum(jnp.max(sc, axis=0, keepdims=True), sink)
            e = jnp.exp2(sc - m)
            den = jnp.sum(e, axis=0, keepdims=True) + jnp.exp2(sink - m)
            ot = _dot(vt_all[:, u * BAND:(u + 3) * BAND], e.astype(BF16)) * (1.0 / den)
            for c in range(group // 2):
                pair = jnp.where(row_half == 0, ot[:, 2 * c * BAND:(2 * c + 1) * BAND],
                                 ot[:, (2 * c + 1) * BAND:(2 * c + 2) * BAND])
                col = hk * (group // 2) + c
                o_ref[rows, col * LANES:(col + 1) * LANES] = pair.T.astype(BF16)


def _band_bias_table():
    kj = jnp.arange(3 * BAND)[:, None] - BAND
    dist = jnp.abs(kj - jnp.arange(BAND)[None, :])
    per_head = [jnp.where(dist <= BAND, -(2.0 ** -(h + 1) * LOG2E) * dist.astype(F32), NEG_INF)
                for h in range(B_Q_HEADS)]
    group = B_Q_HEADS // B_KV_HEADS
    return jnp.stack([jnp.concatenate(per_head[g * group:(g + 1) * group], axis=1)
                      for g in range(B_KV_HEADS)])


def _attn_b(sinks, qb, kb, vbt, bias):
    b, s, _ = qb.shape
    nb = s // BAND
    nq = B_STEP_BLOCKS
    rows = nq * BAND
    prev = lambda j: jnp.maximum(j * nq - 1, 0)
    nxt = lambda j: jnp.minimum(j * nq + nq, nb - 1)
    k_edge = lambda f: pl.BlockSpec((None, BAND, 2 * LANES), lambda bi, j: (bi, f(j), 0))
    v_edge = lambda f: pl.BlockSpec((None, 2 * LANES, BAND), lambda bi, j: (bi, 0, f(j)))
    k_own = pl.BlockSpec((None, rows, 2 * LANES), lambda bi, j: (bi, j, 0))
    v_own = pl.BlockSpec((None, 2 * LANES, rows), lambda bi, j: (bi, 0, j))
    return pl.pallas_call(
        _attn_b_kernel,
        grid=(b, nb // nq),
        in_specs=[pl.BlockSpec(memory_space=pltpu.SMEM),
                  pl.BlockSpec((None, rows, MIX_WIDTH), lambda bi, j: (bi, j, 0)),
                  k_edge(prev), k_own, k_edge(nxt), v_edge(prev), v_own, v_edge(nxt),
                  _full(bias.shape)],
        out_specs=pl.BlockSpec((None, rows, MIX_WIDTH), lambda bi, j: (bi, j, 0)),
        out_shape=jax.ShapeDtypeStruct((b, s, MIX_WIDTH), BF16),
        compiler_params=_params(),
        name="attn_b",
    )(sinks, qb, kb, kb, kb, vbt, vbt, vbt, bias)


def _merge_kernel(x_ref, g_ref, oa_ref, ob_ref, oc_ref, wg_ref, bg_ref, wb_ref, wo_ref, o_ref):
    x = x_ref[...]
    d = x.shape[1]
    h = _rms(x, g_ref[...]).astype(BF16)
    merged = jnp.zeros(x.shape, F32)
    for n, br_ref in enumerate((oa_ref, ob_ref, oc_ref)):
        z = _dot(h, wg_ref[:, n * d:(n + 1) * d]) + bg_ref[:, n * d:(n + 1) * d]
        gate = 1.0 / (1.0 + jnp.exp(-z))
        merged = merged + gate * _dot(br_ref[...], wb_ref[n])
    o_ref[...] = x + _dot(merged.astype(BF16), wo_ref[...])


def _merge(x, g, oa, ob, oc, wg, bg, wb, wo, l):
    b, s, d = x.shape
    tm = TOKEN_TILE
    row = lambda c: pl.BlockSpec((None, tm, c), lambda bi, i: (bi, i, 0))
    return pl.pallas_call(
        _merge_kernel,
        grid=(b, s // tm),
        in_specs=[row(d), _full(g.shape), row(MIX_WIDTH), row(MIX_WIDTH), row(MIX_WIDTH),
                  _layer(wg, l), _full(bg.shape), _layer(wb, l), _layer(wo, l)],
        out_specs=row(d),
        out_shape=jax.ShapeDtypeStruct(x.shape, F32),
        compiler_params=_params(),
        name="merge",
    )(x, g, oa, ob, oc, wg, bg, wb, wo)


def _prep_w_in(w_in):
    dup = lambda w: jnp.concatenate([w[..., 0:64], w[..., 0:64], w[..., 64:128], w[..., 64:128]], axis=-1)
    kr = w_in[..., 2944:3008]
    return jnp.concatenate([
        w_in[..., 0:2048], dup(w_in[..., 2048:2176]), dup(w_in[..., 2176:2304]), w_in[..., 2304:2944],
        kr, kr[..., 32:64], kr[..., 0:32]], axis=-1).astype(BF16)


def _prep_w_uq(w_uq):
    pe = w_uq[..., C_NOPE:]
    half = C_ROPE // 2
    w = jnp.concatenate([w_uq, pe[..., half:], pe[..., :half]], axis=-1)
    return w.reshape(w_uq.shape[0], w_uq.shape[1], C_HEADS * 2 * LANES).astype(BF16)


def _alibi_key_table(c_alibi):
    term = c_alibi[None, :] * jnp.arange(K_TILE, dtype=F32)[:, None]
    pieces = []
    for _ in range(ALIBI_TERMS):
        piece = term.astype(BF16)
        pieces.append(piece)
        term = term - piece.astype(F32)
    tab = jnp.stack(pieces, axis=-1)
    tab = jnp.pad(tab, ((0, 0), (0, 0), (0, LANES - ALIBI_TERMS)))
    return tab.reshape(K_TILE, -1)


def kernel(x, ffn1_norm, ffn1_w13, ffn1_w2, mix_norm, w_in, w_gate, b_gate, a_lambda, a_subln,
           b_sinks, c_q_norm, c_w_uq, c_kv_norm, c_w_ukv, w_branch, w_out, ffn2_norm, ffn2_w13,
           ffn2_w2, final_norm):
    b, s, d = x.shape
    depth = w_in.shape[0]
    assert s % K_TILE == 0 and s % C_K_TILE == 0 and K_TILE % TOKEN_TILE == 0 and K_TILE % A_Q_TILE == 0 and d % LANES == 0

    pos = jnp.arange(s, dtype=F32)
    inv_freq = ROPE_THETA ** (-jnp.arange(0, C_ROPE, 2, dtype=F32) / C_ROPE)
    ang = pos[:, None] * inv_freq[None, :]
    cos, sin = jnp.cos(ang), jnp.sin(ang)
    pad = jnp.zeros((s, LANES - C_ROPE), F32)
    cos2 = jnp.concatenate([cos, cos, pad], axis=1)
    sin2 = jnp.concatenate([-sin, sin, pad], axis=1)
    c_alibi = 2.0 ** (-8.0 * jnp.arange(1, A_HEADS + 1, dtype=F32) / A_HEADS) * LOG2E
    alibi = _alibi_key_table(c_alibi)
    band_bias = _band_bias_table()
    row = lambda v: v.reshape(1, -1)
    gf = row(final_norm)

    bf = lambda w: w.astype(BF16)
    w13_1, w2_1, w13_2, w2_2 = bf(ffn1_w13), bf(ffn1_w2), bf(ffn2_w13), bf(ffn2_w2)
    w_in_x, w_uq_x = _prep_w_in(w_in), _prep_w_uq(c_w_uq)
    w_kn = bf(c_w_ukv[..., :C_NOPE].reshape(depth, C_KV_RANK, -1))
    w_v = bf(c_w_ukv[..., C_NOPE:].reshape(depth, C_KV_RANK, -1))
    w_g, w_b, w_o = bf(w_gate), bf(w_branch), bf(w_out)

    for l in range(depth):
        x = _ffn(x, row(ffn1_norm[l]), w13_1, w2_1, gf, l, False)

        qa, ka, vat, qb, kb, vb, qc, kc, vct = _mix_in(
            x, row(mix_norm[l]), w_in_x, row(c_q_norm[l]), w_uq_x, row(c_kv_norm[l]), w_kn, w_v,
            cos2, sin2, alibi, l)

        lam_init = jnp.full((1, 1), 0.8 - 0.6 * math.exp(-0.3 * l), F32)
        oa = _attn_a(c_alibi, qa, ka, vat, a_lambda[l], lam_init, a_subln[l].reshape(-1, 1))
        ob = _attn_b(b_sinks[l], qb, kb, vb, band_bias)
        oc = _attn_c(qc, kc, vct)

        x = _merge(x, row(mix_norm[l]), oa, ob, oc, w_g, row(b_gate[l]), w_b, w_o, l)
        x = _ffn(x, row(ffn2_norm[l]), w13_2, w2_2, gf, l, l == depth - 1)
    return x
```

```python
import functools
import math

import jax
import jax.numpy as jnp
from jax import lax
from jax.experimental import pallas as pl
from jax.experimental.pallas import tpu as pltpu

BF16 = jnp.bfloat16
F32 = jnp.float32

EPS = 1e-6
NEG_INF = -1e30
LOG2E = math.log2(math.e)
ROPE_THETA = 10000.0

A_HEADS = 4
A_QK_DIM = 64
B_Q_HEADS = 8
B_KV_HEADS = 2
B_HEAD_DIM = 64
BAND = 128
B_STEP_BLOCKS = 4
C_HEADS = 4
C_Q_RANK = 384
C_KV_RANK = 256
C_NOPE = 128
C_ROPE = 64
MIX_WIDTH = 512
LANES = 128

VMEM_LIMIT = 52 * 1024 * 1024

TOKEN_TILE = 512
MXU_DIM = 256
FFN_CHUNK = 6 * MXU_DIM
K_TILE = 512
C_K_TILE = 1024
Q_TILE = 2048
A_Q_TILE = K_TILE
V_ROWS = LANES + 16
ACC_ROWS = LANES + 8
ALIBI_TERMS = 3


def _params():
    return pltpu.CompilerParams(vmem_limit_bytes=VMEM_LIMIT)


def _rms(x, g):
    return x * lax.rsqrt(jnp.mean(x * x, axis=-1, keepdims=True) + EPS) * g


def _dot(a, b):
    return jnp.dot(a, b, preferred_element_type=F32)


def _dot_nt(a, b):
    return lax.dot_general(a, b, (((1,), (1,)), ((), ())), preferred_element_type=F32)


def _full(shape):
    return pl.BlockSpec(shape, lambda *_: (0,) * len(shape))


def _layer(stack, l):
    return pl.BlockSpec((None,) + stack.shape[1:], lambda *_: (l,) + (0,) * (stack.ndim - 1),
                        pipeline_mode=pl.Buffered(1))


def _ffn_kernel(x_ref, g_ref, w13_ref, w2_ref, gf_ref, o_ref, *, final):
    x = x_ref[...]
    h = _rms(x, g_ref[...]).astype(BF16)
    d_ff = w2_ref.shape[0]
    acc = jnp.zeros(x.shape, F32)
    for lo in range(0, d_ff, FFN_CHUNK):
        hi = min(lo + FFN_CHUNK, d_ff)
        a = _dot(h, w13_ref[:, lo:hi])
        g = _dot(h, w13_ref[:, d_ff + lo:d_ff + hi])
        act = (a * (1.0 / (1.0 + jnp.exp(-a))) * g).astype(BF16)
        acc = acc + _dot(act, w2_ref[lo:hi, :])
    y = x + 0.5 * acc
    if final:
        y = _rms(y, gf_ref[...])
    o_ref[...] = y


def _ffn(x, g, w13, w2, gf, l, final):
    b, s, d = x.shape
    tm = TOKEN_TILE
    xspec = pl.BlockSpec((None, tm, d), lambda bi, i: (bi, i, 0))
    return pl.pallas_call(
        functools.partial(_ffn_kernel, final=final),
        grid=(b, s // tm),
        in_specs=[xspec, _full(g.shape), _layer(w13, l), _layer(w2, l), _full(gf.shape)],
        out_specs=xspec,
        out_shape=jax.ShapeDtypeStruct(x.shape, F32),
        compiler_params=_params(),
        name="ffn",
    )(x, g, w13, w2, gf)


def _store_values_t(vt_ref, v, heads):
    vt = v.T.astype(BF16)
    ones = jnp.ones((V_ROWS - LANES, vt.shape[1]), BF16)
    for hd in range(heads):
        vt_ref[hd * V_ROWS:hd * V_ROWS + LANES, :] = vt[hd * LANES:(hd + 1) * LANES]
        vt_ref[hd * V_ROWS + LANES:(hd + 1) * V_ROWS, :] = ones


def _mix_in_kernel(x_ref, g_ref, w_ref, cqg_ref, wq_ref, ckvg_ref, wkn_ref, wv_ref,
                   cos_ref, sin_ref, alibi_ref,
                   qa_ref, ka_ref, vat_ref, qb_ref, kb_ref, vbt_ref, qc_ref, kc_ref, vct_ref):
    h = _rms(x_ref[...], g_ref[...]).astype(BF16)
    cos2 = cos_ref[...]
    sin2 = sin_ref[...]

    za = _dot(h, w_ref[:, 0:1536])
    qa_ref[...] = (za[:, 0:512] * (A_QK_DIM ** -0.5 * LOG2E)).astype(BF16)
    for hd in range(A_HEADS):
        ka_ref[:, 2 * hd * LANES:(2 * hd + 1) * LANES] = (
            za[:, 512 + hd * LANES:512 + (hd + 1) * LANES].astype(BF16))
        ka_ref[:, (2 * hd + 1) * LANES:(2 * hd + 2) * LANES] = alibi_ref[:, hd * LANES:(hd + 1) * LANES]
    _store_values_t(vat_ref, za[:, 1024:1536], A_HEADS)

    zb = _dot(h, w_ref[:, 1536:2560])
    qb_ref[...] = (zb[:, 0:512] * (B_HEAD_DIM ** -0.5 * LOG2E)).astype(BF16)
    kb_ref[...] = zb[:, 512:768].astype(BF16)
    vbt_ref[...] = zb[:, 768:1024].T.astype(BF16)

    zc = _dot(h, w_ref[:, 2560:3328])
    c_scale = (C_NOPE + C_ROPE) ** -0.5 * LOG2E
    cqn = _rms(zc[:, 0:C_Q_RANK], cqg_ref[...]).astype(BF16)
    q = _dot(cqn, wq_ref[...])
    for hd in range(C_HEADS):
        lo = hd * 2 * LANES
        qc_ref[:, lo:lo + LANES] = (q[:, lo:lo + LANES] * c_scale).astype(BF16)
        qp = q[:, lo + LANES:lo + 2 * LANES]
        qpr = qp * cos2 + pltpu.roll(qp, C_ROPE, 1) * sin2
        qc_ref[:, lo + LANES:lo + 2 * LANES] = (qpr * c_scale).astype(BF16)
    ckvn = _rms(zc[:, C_Q_RANK:C_Q_RANK + C_KV_RANK], ckvg_ref[...]).astype(BF16)
    kc_ref[:, 0:512] = _dot(ckvn, wkn_ref[...]).astype(BF16)
    kr = zc[:, C_Q_RANK + C_KV_RANK:]
    kc_ref[:, 512:640] = (kr * cos2 + pltpu.roll(kr, C_ROPE, 1) * sin2).astype(BF16)
    _store_values_t(vct_ref, _dot(ckvn, wv_ref[...]), C_HEADS)


def _mix_in(x, g, w, cqg, wq, ckvg, wkn, wv, cos2, sin2, alibi, l):
    b, s, d = x.shape
    tm = TOKEN_TILE
    row = lambda c: pl.BlockSpec((None, tm, c), lambda bi, i: (bi, i, 0))
    colT = pl.BlockSpec((None, 4 * V_ROWS, tm), lambda bi, i: (bi, 0, i))
    tab = pl.BlockSpec((tm, LANES), lambda bi, i: (i, 0))
    sd = lambda c: jax.ShapeDtypeStruct((b, s, c), BF16)
    sdT = jax.ShapeDtypeStruct((b, 4 * V_ROWS, s), BF16)
    return pl.pallas_call(
        _mix_in_kernel,
        grid=(b, s // tm),
        in_specs=[row(d), _full(g.shape), _layer(w, l), _full(cqg.shape), _layer(wq, l),
                  _full(ckvg.shape), _layer(wkn, l), _layer(wv, l), tab, tab,
                  pl.BlockSpec((tm, alibi.shape[1]), lambda bi, i: (i % (K_TILE // tm), 0))],
        out_specs=[row(512), row(1024), colT, row(512), row(256),
                   pl.BlockSpec((None, 2 * LANES, tm), lambda bi, i: (bi, 0, i)),
                   row(1024), row(640), colT],
        out_shape=[sd(512), sd(1024), sdT, sd(512), sd(256),
                   jax.ShapeDtypeStruct((b, 2 * LANES, s), BF16), sd(1024), sd(640), sdT],
        compiler_params=_params(),
        name="mix_in",
    )(x, g, w, cqg, wq, ckvg, wkn, wv, cos2, sin2, alibi)


OVERFLOW_GUARD = 2.0 ** 60
PROBE_KEYS = 128
P_BUFFERS = 2


def _value_product(vt, p):
    return _dot(vt, p)[:ACC_ROWS]


def _tile_stats(st, vt):
    m_t = jnp.max(st, axis=0, keepdims=True)
    return m_t, _value_product(vt, jnp.exp2(st - m_t).astype(BF16))


def _flash_fixed_reference(n, qk_fn, v_fn, ref_fn, acc, p_refs):
    def weights(t):
        p_refs[t % len(p_refs)][...] = jnp.exp2(qk_fn(t) - ref_fn(t)).astype(BF16)

    if n:
        weights(0)
    for t in range(n):
        if t + 1 < n:
            weights(t + 1)
        acc = acc + _value_product(v_fn(t), p_refs[t % len(p_refs)][...])
    return acc


def _flash_running_max(n, score_fn, v_fn, width):
    def body(kt, carry):
        m, acc = carry
        m_t, pv_t = _tile_stats(score_fn(kt), v_fn(kt))
        m_new = jnp.maximum(m, m_t)
        return m_new, jnp.exp2(m - m_new) * acc + jnp.exp2(m_t - m_new) * pv_t

    init = (jnp.full((1, width), -jnp.inf, F32), jnp.zeros((ACC_ROWS, width), F32))
    return lax.fori_loop(0, n, body, init)[1]


def _denominator_overflowed(acc):
    l = acc[LANES:LANES + 1]
    return jnp.max(jnp.where(l < OVERFLOW_GUARD, 0.0, 1.0)) > 0.5


def _attn_a_kernel(c_ref, q_ref, k_ref, vt_ref, lam_ref, lam_init_ref, g_ref, o_ref,
                   qq_ref, sd_ref, *p_refs):
    hd = pl.program_id(1)
    qi = pl.program_id(2)
    tq = q_ref.shape[0]
    assert tq == K_TILE
    n_kt = k_ref.shape[0] // K_TILE
    c = c_ref[hd]

    q = q_ref[...]
    lane = lax.broadcasted_iota(jnp.int32, q.shape, 1)
    zero = jnp.zeros_like(q)
    qm = jnp.concatenate([jnp.where(lane < A_QK_DIM, q, zero),
                          jnp.where(lane >= A_QK_DIM, q, zero)], axis=0)
    sel = (lax.broadcasted_iota(jnp.int32, qm.shape, 1) < ALIBI_TERMS).astype(BF16)
    for slot, sign in enumerate((-1.0, 1.0, 0.0)):
        qq_ref[slot, :, 0:LANES] = qm
        qq_ref[slot, :, LANES:2 * LANES] = sel * sign

    def keys(kt):
        return k_ref[pl.ds(pl.multiple_of(kt * K_TILE, K_TILE), K_TILE), :]

    def values(kt):
        return vt_ref[:, pl.ds(pl.multiple_of(kt * K_TILE, K_TILE), K_TILE)]

    q0 = qi * tq
    rel = (lax.broadcasted_iota(jnp.int32, (K_TILE, tq), 0)
           - lax.broadcasted_iota(jnp.int32, (K_TILE, tq), 1))

    def biased_scores(kt):
        dist = jnp.abs(rel + (kt * K_TILE - q0)).astype(F32) * c
        return _dot_nt(keys(kt), qq_ref[2]) - jnp.concatenate([dist, dist], axis=1)

    def finish(acc):
        lp = lam_ref[...]
        lam_init = lam_init_ref[...]
        lam = (jnp.exp(jnp.sum(lp[0:1] * lp[1:2], axis=1, keepdims=True))
               - jnp.exp(jnp.sum(lp[2:3] * lp[3:4], axis=1, keepdims=True)) + lam_init)
        inv_l = 1.0 / acc[LANES:LANES + 1]
        o = acc[:LANES, :tq] * inv_l[:, :tq] - lam * (acc[:LANES, tq:] * inv_l[:, tq:])
        o = o * lax.rsqrt(jnp.mean(o * o, axis=0, keepdims=True) + EPS) * g_ref[...]
        o = o * (1.0 - lam_init)
        o_ref[...] = o.T.astype(BF16)

    kd = qi
    sd_ref[...] = biased_scores(kd)
    m0 = jnp.max(sd_ref[...], axis=0, keepdims=True)
    q_row = lax.broadcasted_iota(jnp.int32, (1, tq), 1).astype(F32) * c
    q_row = jnp.concatenate([q_row, q_row], axis=1)

    def key_tile(t):
        t = jnp.int32(t - 1)
        kt = t + (t >= kd).astype(jnp.int32)
        return kt, (kt < kd).astype(jnp.int32)

    def qk_fn(t):
        if t == 0:
            return sd_ref[...]
        kt, before = key_tile(t)
        return _dot_nt(keys(kt), qq_ref[before])

    def v_fn(t):
        return values(kd if t == 0 else key_tile(t)[0])

    def ref_fn(t):
        if t == 0:
            return m0
        kt, before = key_tile(t)
        sgn = (2 * before - 1).astype(F32)
        return m0 - ((kt * K_TILE - q0).astype(F32) * c - q_row) * sgn

    acc = _flash_fixed_reference(n_kt, qk_fn, v_fn, ref_fn, jnp.zeros((ACC_ROWS, 2 * tq), F32),
                                 p_refs)
    finish(acc)

    @pl.when(_denominator_overflowed(acc))
    def _():
        finish(_flash_running_max(n_kt, biased_scores, values, 2 * tq))


def _attn_a(c_alibi, qa, ka, vat, lam_p, lam_init, g_col):
    b, s, _ = qa.shape
    tq = A_Q_TILE
    return pl.pallas_call(
        _attn_a_kernel,
        grid=(b, A_HEADS, s // tq),
        in_specs=[
            pl.BlockSpec(memory_space=pltpu.SMEM),
            pl.BlockSpec((None, tq, LANES), lambda bi, h, i: (bi, i, h)),
            pl.BlockSpec((None, s, 2 * LANES), lambda bi, h, i: (bi, 0, h)),
            pl.BlockSpec((None, V_ROWS, s), lambda bi, h, i: (bi, h, 0)),
            _full(lam_p.shape), _full(lam_init.shape), _full(g_col.shape),
        ],
        out_specs=pl.BlockSpec((None, tq, LANES), lambda bi, h, i: (bi, i, h)),
        out_shape=jax.ShapeDtypeStruct((b, s, MIX_WIDTH), BF16),
        scratch_shapes=[pltpu.VMEM((3, 2 * tq, 2 * LANES), BF16), pltpu.VMEM((K_TILE, 2 * tq), F32),
                       ] + [pltpu.VMEM((K_TILE, 2 * tq), BF16)] * P_BUFFERS,
        compiler_params=_params(),
        name="attn_a",
    )(c_alibi, qa, ka, vat, lam_p, lam_init, g_col)


def _attn_c_kernel(q_ref, kn_ref, kpe_ref, vt_ref, o_ref, *p_refs):
    q = q_ref[...]
    tq = q.shape[0]
    tk = C_K_TILE
    n_kt = kn_ref.shape[0] // tk

    def scores(kt):
        k0 = pl.multiple_of(kt * tk, tk)
        k = jnp.concatenate([kn_ref[pl.ds(k0, tk), :], kpe_ref[pl.ds(k0, tk), :]], axis=1)
        return _dot_nt(k, q)

    def values(kt):
        return vt_ref[:, pl.ds(pl.multiple_of(kt * tk, tk), tk)]

    def finish(acc):
        o_ref[...] = (acc[:LANES] * (1.0 / acc[LANES:LANES + 1])).T.astype(BF16)

    probe = jnp.concatenate([kn_ref[0:PROBE_KEYS, :], kpe_ref[0:PROBE_KEYS, :]], axis=1)
    m0 = jnp.max(_dot_nt(probe, q), axis=0, keepdims=True)
    acc = _flash_fixed_reference(n_kt, lambda t: scores(jnp.int32(t)), lambda t: values(jnp.int32(t)),
                                 lambda t: m0, jnp.zeros((ACC_ROWS, tq), F32), p_refs)
    finish(acc)

    @pl.when(_denominator_overflowed(acc))
    def _():
        finish(_flash_running_max(n_kt, scores, values, tq))


def _attn_c(qc, kc, vct):
    b, s, _ = qc.shape
    tq = Q_TILE
    return pl.pallas_call(
        _attn_c_kernel,
        grid=(b, C_HEADS, s // tq),
        in_specs=[
            pl.BlockSpec((None, tq, 2 * LANES), lambda bi, h, i: (bi, i, h)),
            pl.BlockSpec((None, s, LANES), lambda bi, h, i: (bi, 0, h)),
            pl.BlockSpec((None, s, LANES), lambda bi, h, i: (bi, 0, C_HEADS)),
            pl.BlockSpec((None, V_ROWS, s), lambda bi, h, i: (bi, h, 0)),
        ],
        out_specs=pl.BlockSpec((None, tq, LANES), lambda bi, h, i: (bi, i, h)),
        out_shape=jax.ShapeDtypeStruct((b, s, MIX_WIDTH), BF16),
        scratch_shapes=[pltpu.VMEM((C_K_TILE, tq), BF16)] * P_BUFFERS,
        compiler_params=_params(),
        name="attn_c",
    )(qc, kc, kc, vct)


def _attn_b_kernel(sinks_ref, q_ref, kp_ref, ko_ref, kx_ref, vp_ref, vo_ref, vx_ref, bias_ref, o_ref):
    first = pl.program_id(1) * B_STEP_BLOCKS
    nb = pl.num_programs(1) * B_STEP_BLOCKS
    group = B_Q_HEADS // B_KV_HEADS
    key_row = lax.broadcasted_iota(jnp.int32, (3 * BAND, 1), 0) - BAND
    half = lax.broadcasted_iota(jnp.int32, (BAND, LANES), 1) // B_HEAD_DIM
    row_half = lax.broadcasted_iota(jnp.int32, (LANES, BAND), 0) // B_HEAD_DIM

    for hk in range(B_KV_HEADS):
        heads = range(hk * group, (hk + 1) * group)
        ksl = slice(hk * LANES, (hk + 1) * LANES)
        k_all = jnp.concatenate([kp_ref[:, ksl], ko_ref[:, ksl], kx_ref[:, ksl]], axis=0)
        vt_all = jnp.concatenate([vp_ref[ksl, :], vo_ref[ksl, :], vx_ref[ksl, :]], axis=1)
        sink = jnp.concatenate([jnp.full((1, BAND), sinks_ref[h] * LOG2E, F32) for h in heads], axis=1)
        for u in range(B_STEP_BLOCKS):
            rows = slice(u * BAND, (u + 1) * BAND)
            qg = []
            for h in heads:
                qh = q_ref[rows, (h // 2) * LANES:(h // 2 + 1) * LANES]
                qg.append(jnp.where(half == h % 2, qh, jnp.zeros_like(qh)))
            kpos = (first + u) * BAND + key_row
            edge = jnp.where((kpos >= 0) & (kpos < nb * BAND), 0.0, NEG_INF)
            sc = _dot_nt(k_all[u * BAND:(u + 3) * BAND], jnp.concatenate(qg, axis=0))
            sc = jnp.maximum(sc + bias_ref[hk] + edge, NEG_INF)
            m = jnp.maximum(jnp.max(sc, axis=0, keepdims=True), sink)
            e = jnp.exp2(sc - m)
            den = jnp.sum(e, axis=0, keepdims=True) + jnp.exp2(sink - m)
            ot = _dot(vt_all[:, u * BAND:(u + 3) * BAND], e.astype(BF16)) * (1.0 / den)
            for c in range(group // 2):
                pair = jnp.where(row_half == 0, ot[:, 2 * c * BAND:(2 * c + 1) * BAND],
                                 ot[:, (2 * c + 1) * BAND:(2 * c + 2) * BAND])
                col = hk * (group // 2) + c
                o_ref[rows, col * LANES:(col + 1) * LANES] = pair.T.astype(BF16)


def _band_bias_table():
    kj = jnp.arange(3 * BAND)[:, None] - BAND
    dist = jnp.abs(kj - jnp.arange(BAND)[None, :])
    per_head = [jnp.where(dist <= BAND, -(2.0 ** -(h + 1) * LOG2E) * dist.astype(F32), NEG_INF)
                for h in range(B_Q_HEADS)]
    group = B_Q_HEADS // B_KV_HEADS
    return jnp.stack([jnp.concatenate(per_head[g * group:(g + 1) * group], axis=1)
                      for g in range(B_KV_HEADS)])


def _attn_b(sinks, qb, kb, vbt, bias):
    b, s, _ = qb.shape
    nb = s // BAND
    nq = B_STEP_BLOCKS
    rows = nq * BAND
    prev = lambda j: jnp.maximum(j * nq - 1, 0)
    nxt = lambda j: jnp.minimum(j * nq + nq, nb - 1)
    k_edge = lambda f: pl.BlockSpec((None, BAND, 2 * LANES), lambda bi, j: (bi, f(j), 0))
    v_edge = lambda f: pl.BlockSpec((None, 2 * LANES, BAND), lambda bi, j: (bi, 0, f(j)))
    k_own = pl.BlockSpec((None, rows, 2 * LANES), lambda bi, j: (bi, j, 0))
    v_own = pl.BlockSpec((None, 2 * LANES, rows), lambda bi, j: (bi, 0, j))
    return pl.pallas_call(
        _attn_b_kernel,
        grid=(b, nb // nq),
        in_specs=[pl.BlockSpec(memory_space=pltpu.SMEM),
                  pl.BlockSpec((None, rows, MIX_WIDTH), lambda bi, j: (bi, j, 0)),
                  k_edge(prev), k_own, k_edge(nxt), v_edge(prev), v_own, v_edge(nxt),
                  _full(bias.shape)],
        out_specs=pl.BlockSpec((None, rows, MIX_WIDTH), lambda bi, j: (bi, j, 0)),
        out_shape=jax.ShapeDtypeStruct((b, s, MIX_WIDTH), BF16),
        compiler_params=_params(),
        name="attn_b",
    )(sinks, qb, kb, kb, kb, vbt, vbt, vbt, bias)


def _merge_kernel(x_ref, g_ref, oa_ref, ob_ref, oc_ref, wg_ref, bg_ref, wb_ref, wo_ref, o_ref):
    x = x_ref[...]
    d = x.shape[1]
    h = _rms(x, g_ref[...]).astype(BF16)
    merged = jnp.zeros(x.shape, F32)
    for n, br_ref in enumerate((oa_ref, ob_ref, oc_ref)):
        z = _dot(h, wg_ref[:, n * d:(n + 1) * d]) + bg_ref[:, n * d:(n + 1) * d]
        gate = 1.0 / (1.0 + jnp.exp(-z))
        merged = merged + gate * _dot(br_ref[...], wb_ref[n])
    o_ref[...] = x + _dot(merged.astype(BF16), wo_ref[...])


def _merge(x, g, oa, ob, oc, wg, bg, wb, wo, l):
    b, s, d = x.shape
    tm = TOKEN_TILE
    row = lambda c: pl.BlockSpec((None, tm, c), lambda bi, i: (bi, i, 0))
    return pl.pallas_call(
        _merge_kernel,
        grid=(b, s // tm),
        in_specs=[row(d), _full(g.shape), row(MIX_WIDTH), row(MIX_WIDTH), row(MIX_WIDTH),
                  _layer(wg, l), _full(bg.shape), _layer(wb, l), _layer(wo, l)],
        out_specs=row(d),
        out_shape=jax.ShapeDtypeStruct(x.shape, F32),
        compiler_params=_params(),
        name="merge",
    )(x, g, oa, ob, oc, wg, bg, wb, wo)


def _prep_w_in(w_in):
    dup = lambda w: jnp.concatenate([w[..., 0:64], w[..., 0:64], w[..., 64:128], w[..., 64:128]], axis=-1)
    kr = w_in[..., 2944:3008]
    return jnp.concatenate([
        w_in[..., 0:2048], dup(w_in[..., 2048:2176]), dup(w_in[..., 2176:2304]), w_in[..., 2304:2944],
        kr, kr[..., 32:64], kr[..., 0:32]], axis=-1).astype(BF16)


def _prep_w_uq(w_uq):
    pe = w_uq[..., C_NOPE:]
    half = C_ROPE // 2
    w = jnp.concatenate([w_uq, pe[..., half:], pe[..., :half]], axis=-1)
    return w.reshape(w_uq.shape[0], w_uq.shape[1], C_HEADS * 2 * LANES).astype(BF16)


def _alibi_key_table(c_alibi):
    term = c_alibi[None, :] * jnp.arange(K_TILE, dtype=F32)[:, None]
    pieces = []
    for _ in range(ALIBI_TERMS):
        piece = term.astype(BF16)
        pieces.append(piece)
        term = term - piece.astype(F32)
    tab = jnp.stack(pieces, axis=-1)
    tab = jnp.pad(tab, ((0, 0), (0, 0), (0, LANES - ALIBI_TERMS)))
    return tab.reshape(K_TILE, -1)


def kernel(x, ffn1_norm, ffn1_w13, ffn1_w2, mix_norm, w_in, w_gate, b_gate, a_lambda, a_subln,
           b_sinks, c_q_norm, c_w_uq, c_kv_norm, c_w_ukv, w_branch, w_out, ffn2_norm, ffn2_w13,
           ffn2_w2, final_norm):
    b, s, d = x.shape
    depth = w_in.shape[0]
    assert s % K_TILE == 0 and s % C_K_TILE == 0 and K_TILE % TOKEN_TILE == 0 and d % LANES == 0

    pos = jnp.arange(s, dtype=F32)
    inv_freq = ROPE_THETA ** (-jnp.arange(0, C_ROPE, 2, dtype=F32) / C_ROPE)
    ang = pos[:, None] * inv_freq[None, :]
    cos, sin = jnp.cos(ang), jnp.sin(ang)
    pad = jnp.zeros((s, LANES - C_ROPE), F32)
    cos2 = jnp.concatenate([cos, cos, pad], axis=1)
    sin2 = jnp.concatenate([-sin, sin, pad], axis=1)
    c_alibi = 2.0 ** (-8.0 * jnp.arange(1, A_HEADS + 1, dtype=F32) / A_HEADS) * LOG2E
    alibi = _alibi_key_table(c_alibi)
    band_bias = _band_bias_table()
    row = lambda v: v.reshape(1, -1)
    gf = row(final_norm)

    bf = lambda w: w.astype(BF16)
    w13_1, w2_1, w13_2, w2_2 = bf(ffn1_w13), bf(ffn1_w2), bf(ffn2_w13), bf(ffn2_w2)
    w_in_x, w_uq_x = _prep_w_in(w_in), _prep_w_uq(c_w_uq)
    w_kn = bf(c_w_ukv[..., :C_NOPE].reshape(depth, C_KV_RANK, -1))
    w_v = bf(c_w_ukv[..., C_NOPE:].reshape(depth, C_KV_RANK, -1))
    w_g, w_b, w_o = bf(w_gate), bf(w_branch), bf(w_out)

    for l in range(depth):
        x = _ffn(x, row(ffn1_norm[l]), w13_1, w2_1, gf, l, False)

        qa, ka, vat, qb, kb, vb, qc, kc, vct = _mix_in(
            x, row(mix_norm[l]), w_in_x, row(c_q_norm[l]), w_uq_x, row(c_kv_norm[l]), w_kn, w_v,
            cos2, sin2, alibi, l)

        lam_init = jnp.full((1, 1), 0.8 - 0.6 * math.exp(-0.3 * l), F32)
        oa = _attn_a(c_alibi, qa, ka, vat, a_lambda[l], lam_init, a_subln[l].reshape(-1, 1))
        ob = _attn_b(b_sinks[l], qb, kb, vb, band_bias)
        oc = _attn_c(qc, kc, vct)

        x = _merge(x, row(mix_norm[l]), oa, ob, oc, w_g, row(b_gate[l]), w_b, w_o, l)
        x = _ffn(x, row(ffn2_norm[l]), w13_2, w2_2, gf, l, l == depth - 1)
    return x
```

```python
import functools
import math

import jax
import jax.numpy as jnp
from jax import lax
from jax.experimental import pallas as pl
from jax.experimental.pallas import tpu as pltpu

BF16 = jnp.bfloat16
F32 = jnp.float32

EPS = 1e-6
NEG_INF = -1e30
LOG2E = math.log2(math.e)
ROPE_THETA = 10000.0

A_HEADS = 4
A_QK_DIM = 64
B_Q_HEADS = 8
B_KV_HEADS = 2
B_HEAD_DIM = 64
BAND = 128
B_STEP_BLOCKS = 4
C_HEADS = 4
C_Q_RANK = 384
C_KV_RANK = 256
C_NOPE = 128
C_ROPE = 64
MIX_WIDTH = 512
LANES = 128

VMEM_LIMIT = 52 * 1024 * 1024

TOKEN_TILE = 512
MXU_DIM = 256
FFN_CHUNK = 6 * MXU_DIM
K_TILE = 512
C_K_TILE = 1024
Q_TILE = 2048
A_Q_TILE = K_TILE
V_ROWS = LANES + 16
ACC_ROWS = LANES + 8
ALIBI_TERMS = 3


def _params():
    return pltpu.CompilerParams(vmem_limit_bytes=VMEM_LIMIT)


def _rms(x, g):
    return x * lax.rsqrt(jnp.mean(x * x, axis=-1, keepdims=True) + EPS) * g


def _dot(a, b):
    return jnp.dot(a, b, preferred_element_type=F32)


def _dot_nt(a, b):
    return lax.dot_general(a, b, (((1,), (1,)), ((), ())), preferred_element_type=F32)


def _full(shape):
    return pl.BlockSpec(shape, lambda *_: (0,) * len(shape))


def _layer(stack, l):
    return pl.BlockSpec((None,) + stack.shape[1:], lambda *_: (l,) + (0,) * (stack.ndim - 1),
                        pipeline_mode=pl.Buffered(1))


def _ffn_kernel(x_ref, g_ref, w13_ref, w2_ref, gf_ref, o_ref, *, final):
    x = x_ref[...]
    h = _rms(x, g_ref[...]).astype(BF16)
    d_ff = w2_ref.shape[0]
    acc = jnp.zeros(x.shape, F32)
    for lo in range(0, d_ff, FFN_CHUNK):
        hi = min(lo + FFN_CHUNK, d_ff)
        a = _dot(h, w13_ref[:, lo:hi])
        g = _dot(h, w13_ref[:, d_ff + lo:d_ff + hi])
        act = (a * (1.0 / (1.0 + jnp.exp(-a))) * g).astype(BF16)
        acc = acc + _dot(act, w2_ref[lo:hi, :])
    y = x + 0.5 * acc
    if final:
        y = _rms(y, gf_ref[...])
    o_ref[...] = y


def _ffn(x, g, w13, w2, gf, l, final):
    b, s, d = x.shape
    tm = TOKEN_TILE
    xspec = pl.BlockSpec((None, tm, d), lambda bi, i: (bi, i, 0))
    return pl.pallas_call(
        functools.partial(_ffn_kernel, final=final),
        grid=(b, s // tm),
        in_specs=[xspec, _full(g.shape), _layer(w13, l), _layer(w2, l), _full(gf.shape)],
        out_specs=xspec,
        out_shape=jax.ShapeDtypeStruct(x.shape, F32),
        compiler_params=_params(),
        name="ffn",
    )(x, g, w13, w2, gf)


def _store_values_t(vt_ref, v, heads):
    vt = v.T.astype(BF16)
    ones = jnp.ones((V_ROWS - LANES, vt.shape[1]), BF16)
    for hd in range(heads):
        vt_ref[hd * V_ROWS:hd * V_ROWS + LANES, :] = vt[hd * LANES:(hd + 1) * LANES]
        vt_ref[hd * V_ROWS + LANES:(hd + 1) * V_ROWS, :] = ones


def _mix_in_kernel(x_ref, g_ref, w_ref, cqg_ref, wq_ref, ckvg_ref, wkn_ref, wv_ref,
                   cos_ref, sin_ref, alibi_ref,
                   qa_ref, ka_ref, vat_ref, qb_ref, kb_ref, vbt_ref, qc_ref, kc_ref, vct_ref):
    h = _rms(x_ref[...], g_ref[...]).astype(BF16)
    cos2 = cos_ref[...]
    sin2 = sin_ref[...]

    zc = _dot(h, w_ref[:, 2560:3328])
    cqn = _rms(zc[:, 0:C_Q_RANK], cqg_ref[...]).astype(BF16)
    ckvn = _rms(zc[:, C_Q_RANK:C_Q_RANK + C_KV_RANK], ckvg_ref[...]).astype(BF16)
    kr = zc[:, C_Q_RANK + C_KV_RANK:]
    kc_ref[:, 512:640] = (kr * cos2 + pltpu.roll(kr, C_ROPE, 1) * sin2).astype(BF16)

    za = _dot(h, w_ref[:, 0:1536])
    qa_ref[...] = (za[:, 0:512] * (A_QK_DIM ** -0.5 * LOG2E)).astype(BF16)
    for hd in range(A_HEADS):
        ka_ref[:, 2 * hd * LANES:(2 * hd + 1) * LANES] = (
            za[:, 512 + hd * LANES:512 + (hd + 1) * LANES].astype(BF16))
        ka_ref[:, (2 * hd + 1) * LANES:(2 * hd + 2) * LANES] = alibi_ref[:, hd * LANES:(hd + 1) * LANES]
    _store_values_t(vat_ref, za[:, 1024:1536], A_HEADS)

    c_scale = (C_NOPE + C_ROPE) ** -0.5 * LOG2E
    q = _dot(cqn, wq_ref[...])
    for hd in range(C_HEADS):
        lo = hd * 2 * LANES
        qc_ref[:, lo:lo + LANES] = (q[:, lo:lo + LANES] * c_scale).astype(BF16)
        qp = q[:, lo + LANES:lo + 2 * LANES]
        qpr = qp * cos2 + pltpu.roll(qp, C_ROPE, 1) * sin2
        qc_ref[:, lo + LANES:lo + 2 * LANES] = (qpr * c_scale).astype(BF16)
    kc_ref[:, 0:512] = _dot(ckvn, wkn_ref[...]).astype(BF16)
    _store_values_t(vct_ref, _dot(ckvn, wv_ref[...]), C_HEADS)

    zb = _dot(h, w_ref[:, 1536:2560])
    qb_ref[...] = (zb[:, 0:512] * (B_HEAD_DIM ** -0.5 * LOG2E)).astype(BF16)
    kb_ref[...] = zb[:, 512:768].astype(BF16)
    vbt_ref[...] = zb[:, 768:1024].T.astype(BF16)


def _mix_in(x, g, w, cqg, wq, ckvg, wkn, wv, cos2, sin2, alibi, l):
    b, s, d = x.shape
    tm = TOKEN_TILE
    row = lambda c: pl.BlockSpec((None, tm, c), lambda bi, i: (bi, i, 0))
    colT = pl.BlockSpec((None, 4 * V_ROWS, tm), lambda bi, i: (bi, 0, i))
    tab = pl.BlockSpec((tm, LANES), lambda bi, i: (i, 0))
    sd = lambda c: jax.ShapeDtypeStruct((b, s, c), BF16)
    sdT = jax.ShapeDtypeStruct((b, 4 * V_ROWS, s), BF16)
    return pl.pallas_call(
        _mix_in_kernel,
        grid=(b, s // tm),
        in_specs=[row(d), _full(g.shape), _layer(w, l), _full(cqg.shape), _layer(wq, l),
                  _full(ckvg.shape), _layer(wkn, l), _layer(wv, l), tab, tab,
                  pl.BlockSpec((tm, alibi.shape[1]), lambda bi, i: (i % (K_TILE // tm), 0))],
        out_specs=[row(512), row(1024), colT, row(512), row(256),
                   pl.BlockSpec((None, 2 * LANES, tm), lambda bi, i: (bi, 0, i)),
                   row(1024), row(640), colT],
        out_shape=[sd(512), sd(1024), sdT, sd(512), sd(256),
                   jax.ShapeDtypeStruct((b, 2 * LANES, s), BF16), sd(1024), sd(640), sdT],
        compiler_params=_params(),
        name="mix_in",
    )(x, g, w, cqg, wq, ckvg, wkn, wv, cos2, sin2, alibi)


OVERFLOW_GUARD = 2.0 ** 60
PROBE_KEYS = 128
P_BUFFERS = 2


def _value_product(vt, p):
    return _dot(vt, p)[:ACC_ROWS]


def _tile_stats(st, vt):
    m_t = jnp.max(st, axis=0, keepdims=True)
    return m_t, _value_product(vt, jnp.exp2(st - m_t).astype(BF16))


def _flash_fixed_reference(n, qk_fn, v_fn, ref_fn, acc, p_refs):
    def weights(t):
        p_refs[t % len(p_refs)][...] = jnp.exp2(qk_fn(t) - ref_fn(t)).astype(BF16)

    if n:
        weights(0)
    for t in range(n):
        if t + 1 < n:
            weights(t + 1)
        acc = acc + _value_product(v_fn(t), p_refs[t % len(p_refs)][...])
    return acc


def _flash_running_max(n, score_fn, v_fn, width):
    def body(kt, carry):
        m, acc = carry
        m_t, pv_t = _tile_stats(score_fn(kt), v_fn(kt))
        m_new = jnp.maximum(m, m_t)
        return m_new, jnp.exp2(m - m_new) * acc + jnp.exp2(m_t - m_new) * pv_t

    init = (jnp.full((1, width), -jnp.inf, F32), jnp.zeros((ACC_ROWS, width), F32))
    return lax.fori_loop(0, n, body, init)[1]


def _denominator_overflowed(acc):
    l = acc[LANES:LANES + 1]
    return jnp.max(jnp.where(l < OVERFLOW_GUARD, 0.0, 1.0)) > 0.5


def _attn_a_kernel(c_ref, q_ref, k_ref, vt_ref, lam_ref, lam_init_ref, g_ref, o_ref,
                   qq_ref, sd_ref, *p_refs):
    hd = pl.program_id(1)
    qi = pl.program_id(2)
    tq = q_ref.shape[0]
    assert tq == K_TILE
    n_kt = k_ref.shape[0] // K_TILE
    c = c_ref[hd]

    q = q_ref[...]
    lane = lax.broadcasted_iota(jnp.int32, q.shape, 1)
    zero = jnp.zeros_like(q)
    qm = jnp.concatenate([jnp.where(lane < A_QK_DIM, q, zero),
                          jnp.where(lane >= A_QK_DIM, q, zero)], axis=0)
    sel = (lax.broadcasted_iota(jnp.int32, qm.shape, 1) < ALIBI_TERMS).astype(BF16)
    for slot, sign in enumerate((-1.0, 1.0, 0.0)):
        qq_ref[slot, :, 0:LANES] = qm
        qq_ref[slot, :, LANES:2 * LANES] = sel * sign

    def keys(kt):
        return k_ref[pl.ds(pl.multiple_of(kt * K_TILE, K_TILE), K_TILE), :]

    def values(kt):
        return vt_ref[:, pl.ds(pl.multiple_of(kt * K_TILE, K_TILE), K_TILE)]

    q0 = qi * tq
    rel = (lax.broadcasted_iota(jnp.int32, (K_TILE, tq), 0)
           - lax.broadcasted_iota(jnp.int32, (K_TILE, tq), 1))

    def biased_scores(kt):
        dist = jnp.abs(rel + (kt * K_TILE - q0)).astype(F32) * c
        return _dot_nt(keys(kt), qq_ref[2]) - jnp.concatenate([dist, dist], axis=1)

    def finish(acc):
        lp = lam_ref[...]
        lam_init = lam_init_ref[...]
        lam = (jnp.exp(jnp.sum(lp[0:1] * lp[1:2], axis=1, keepdims=True))
               - jnp.exp(jnp.sum(lp[2:3] * lp[3:4], axis=1, keepdims=True)) + lam_init)
        inv_l = 1.0 / acc[LANES:LANES + 1]
        o = acc[:LANES, :tq] * inv_l[:, :tq] - lam * (acc[:LANES, tq:] * inv_l[:, tq:])
        o = o * lax.rsqrt(jnp.mean(o * o, axis=0, keepdims=True) + EPS) * g_ref[...]
        o = o * (1.0 - lam_init)
        o_ref[...] = o.T.astype(BF16)

    kd = qi
    sd_ref[...] = biased_scores(kd)
    m0 = jnp.max(sd_ref[...], axis=0, keepdims=True)
    q_row = lax.broadcasted_iota(jnp.int32, (1, tq), 1).astype(F32) * c
    q_row = jnp.concatenate([q_row, q_row], axis=1)

    def key_tile(t):
        t = jnp.int32(t - 1)
        kt = t + (t >= kd).astype(jnp.int32)
        return kt, (kt < kd).astype(jnp.int32)

    def qk_fn(t):
        if t == 0:
            return sd_ref[...]
        kt, before = key_tile(t)
        return _dot_nt(keys(kt), qq_ref[before])

    def v_fn(t):
        return values(kd if t == 0 else key_tile(t)[0])

    def ref_fn(t):
        if t == 0:
            return m0
        kt, before = key_tile(t)
        sgn = (2 * before - 1).astype(F32)
        return m0 - ((kt * K_TILE - q0).astype(F32) * c - q_row) * sgn

    acc = _flash_fixed_reference(n_kt, qk_fn, v_fn, ref_fn, jnp.zeros((ACC_ROWS, 2 * tq), F32),
                                 p_refs)
    finish(acc)

    @pl.when(_denominator_overflowed(acc))
    def _():
        finish(_flash_running_max(n_kt, biased_scores, values, 2 * tq))


def _attn_a(c_alibi, qa, ka, vat, lam_p, lam_init, g_col):
    b, s, _ = qa.shape
    tq = A_Q_TILE
    return pl.pallas_call(
        _attn_a_kernel,
        grid=(b, A_HEADS, s // tq),
        in_specs=[
            pl.BlockSpec(memory_space=pltpu.SMEM),
            pl.BlockSpec((None, tq, LANES), lambda bi, h, i: (bi, i, h)),
            pl.BlockSpec((None, s, 2 * LANES), lambda bi, h, i: (bi, 0, h)),
            pl.BlockSpec((None, V_ROWS, s), lambda bi, h, i: (bi, h, 0)),
            _full(lam_p.shape), _full(lam_init.shape), _full(g_col.shape),
        ],
        out_specs=pl.BlockSpec((None, tq, LANES), lambda bi, h, i: (bi, i, h)),
        out_shape=jax.ShapeDtypeStruct((b, s, MIX_WIDTH), BF16),
        scratch_shapes=[pltpu.VMEM((3, 2 * tq, 2 * LANES), BF16), pltpu.VMEM((K_TILE, 2 * tq), F32),
                       ] + [pltpu.VMEM((K_TILE, 2 * tq), BF16)] * P_BUFFERS,
        compiler_params=_params(),
        name="attn_a",
    )(c_alibi, qa, ka, vat, lam_p, lam_init, g_col)


def _attn_c_kernel(q_ref, kn_ref, kpe_ref, vt_ref, o_ref, *p_refs):
    q = q_ref[...]
    tq = q.shape[0]
    tk = C_K_TILE
    n_kt = kn_ref.shape[0] // tk

    def scores(kt):
        k0 = pl.multiple_of(kt * tk, tk)
        k = jnp.concatenate([kn_ref[pl.ds(k0, tk), :], kpe_ref[pl.ds(k0, tk), :]], axis=1)
        return _dot_nt(k, q)

    def values(kt):
        return vt_ref[:, pl.ds(pl.multiple_of(kt * tk, tk), tk)]

    def finish(acc):
        o_ref[...] = (acc[:LANES] * (1.0 / acc[LANES:LANES + 1])).T.astype(BF16)

    probe = jnp.concatenate([kn_ref[0:PROBE_KEYS, :], kpe_ref[0:PROBE_KEYS, :]], axis=1)
    m0 = jnp.max(_dot_nt(probe, q), axis=0, keepdims=True)
    acc = _flash_fixed_reference(n_kt, lambda t: scores(jnp.int32(t)), lambda t: values(jnp.int32(t)),
                                 lambda t: m0, jnp.zeros((ACC_ROWS, tq), F32), p_refs)
    finish(acc)

    @pl.when(_denominator_overflowed(acc))
    def _():
        finish(_flash_running_max(n_kt, scores, values, tq))


def _attn_c(qc, kc, vct):
    b, s, _ = qc.shape
    tq = Q_TILE
    return pl.pallas_call(
        _attn_c_kernel,
        grid=(b, C_HEADS, s // tq),
        in_specs=[
            pl.BlockSpec((None, tq, 2 * LANES), lambda bi, h, i: (bi, i, h)),
            pl.BlockSpec((None, s, LANES), lambda bi, h, i: (bi, 0, h)),
            pl.BlockSpec((None, s, LANES), lambda bi, h, i: (bi, 0, C_HEADS)),
            pl.BlockSpec((None, V_ROWS, s), lambda bi, h, i: (bi, h, 0)),
        ],
        out_specs=pl.BlockSpec((None, tq, LANES), lambda bi, h, i: (bi, i, h)),
        out_shape=jax.ShapeDtypeStruct((b, s, MIX_WIDTH), BF16),
        scratch_shapes=[pltpu.VMEM((C_K_TILE, tq), BF16)] * P_BUFFERS,
        compiler_params=_params(),
        name="attn_c",
    )(qc, kc, kc, vct)


def _attn_b_kernel(sinks_ref, q_ref, kp_ref, ko_ref, kx_ref, vp_ref, vo_ref, vx_ref, bias_ref, o_ref):
    first = pl.program_id(1) * B_STEP_BLOCKS
    nb = pl.num_programs(1) * B_STEP_BLOCKS
    group = B_Q_HEADS // B_KV_HEADS
    key_row = lax.broadcasted_iota(jnp.int32, (3 * BAND, 1), 0) - BAND
    half = lax.broadcasted_iota(jnp.int32, (BAND, LANES), 1) // B_HEAD_DIM
    row_half = lax.broadcasted_iota(jnp.int32, (LANES, BAND), 0) // B_HEAD_DIM

    for hk in range(B_KV_HEADS):
        heads = range(hk * group, (hk + 1) * group)
        ksl = slice(hk * LANES, (hk + 1) * LANES)
        k_all = jnp.concatenate([kp_ref[:, ksl], ko_ref[:, ksl], kx_ref[:, ksl]], axis=0)
        vt_all = jnp.concatenate([vp_ref[ksl, :], vo_ref[ksl, :], vx_ref[ksl, :]], axis=1)
        sink = jnp.concatenate([jnp.full((1, BAND), sinks_ref[h] * LOG2E, F32) for h in heads], axis=1)
        for u in range(B_STEP_BLOCKS):
            rows = slice(u * BAND, (u + 1) * BAND)
            qg = []
            for h in heads:
                qh = q_ref[rows, (h // 2) * LANES:(h // 2 + 1) * LANES]
                qg.append(jnp.where(half == h % 2, qh, jnp.zeros_like(qh)))
            kpos = (first + u) * BAND + key_row
            edge = jnp.where((kpos >= 0) & (kpos < nb * BAND), 0.0, NEG_INF)
            sc = _dot_nt(k_all[u * BAND:(u + 3) * BAND], jnp.concatenate(qg, axis=0))
            sc = jnp.maximum(sc + bias_ref[hk] + edge, NEG_INF)
            m = jnp.maximum(jnp.max(sc, axis=0, keepdims=True), sink)
            e = jnp.exp2(sc - m)
            den = jnp.sum(e, axis=0, keepdims=True) + jnp.exp2(sink - m)
            ot = _dot(vt_all[:, u * BAND:(u + 3) * BAND], e.astype(BF16)) * (1.0 / den)
            for c in range(group // 2):
                pair = jnp.where(row_half == 0, ot[:, 2 * c * BAND:(2 * c + 1) * BAND],
                                 ot[:, (2 * c + 1) * BAND:(2 * c + 2) * BAND])
                col = hk * (group // 2) + c
                o_ref[rows, col * LANES:(col + 1) * LANES] = pair.T.astype(BF16)


def _band_bias_table():
    kj = jnp.arange(3 * BAND)[:, None] - BAND
    dist = jnp.abs(kj - jnp.arange(BAND)[None, :])
    per_head = [jnp.where(dist <= BAND, -(2.0 ** -(h + 1) * LOG2E) * dist.astype(F32), NEG_INF)
                for h in range(B_Q_HEADS)]
    group = B_Q_HEADS // B_KV_HEADS
    return jnp.stack([jnp.concatenate(per_head[g * group:(g + 1) * group], axis=1)
                      for g in range(B_KV_HEADS)])


def _attn_b(sinks, qb, kb, vbt, bias):
    b, s, _ = qb.shape
    nb = s // BAND
    nq = B_STEP_BLOCKS
    rows = nq * BAND
    prev = lambda j: jnp.maximum(j * nq - 1, 0)
    nxt = lambda j: jnp.minimum(j * nq + nq, nb - 1)
    k_edge = lambda f: pl.BlockSpec((None, BAND, 2 * LANES), lambda bi, j: (bi, f(j), 0))
    v_edge = lambda f: pl.BlockSpec((None, 2 * LANES, BAND), lambda bi, j: (bi, 0, f(j)))
    k_own = pl.BlockSpec((None, rows, 2 * LANES), lambda bi, j: (bi, j, 0))
    v_own = pl.BlockSpec((None, 2 * LANES, rows), lambda bi, j: (bi, 0, j))
    return pl.pallas_call(
        _attn_b_kernel,
        grid=(b, nb // nq),
        in_specs=[pl.BlockSpec(memory_space=pltpu.SMEM),
                  pl.BlockSpec((None, rows, MIX_WIDTH), lambda bi, j: (bi, j, 0)),
                  k_edge(prev), k_own, k_edge(nxt), v_edge(prev), v_own, v_edge(nxt),
                  _full(bias.shape)],
        out_specs=pl.BlockSpec((None, rows, MIX_WIDTH), lambda bi, j: (bi, j, 0)),
        out_shape=jax.ShapeDtypeStruct((b, s, MIX_WIDTH), BF16),
        compiler_params=_params(),
        name="attn_b",
    )(sinks, qb, kb, kb, kb, vbt, vbt, vbt, bias)


def _merge_kernel(x_ref, g_ref, oa_ref, ob_ref, oc_ref, wg_ref, bg_ref, wb_ref, wo_ref, o_ref):
    x = x_ref[...]
    d = x.shape[1]
    h = _rms(x, g_ref[...]).astype(BF16)
    merged = jnp.zeros(x.shape, F32)
    for n, br_ref in enumerate((oa_ref, ob_ref, oc_ref)):
        z = _dot(h, wg_ref[:, n * d:(n + 1) * d]) + bg_ref[:, n * d:(n + 1) * d]
        gate = 1.0 / (1.0 + jnp.exp(-z))
        merged = merged + gate * _dot(br_ref[...], wb_ref[n])
    o_ref[...] = x + _dot(merged.astype(BF16), wo_ref[...])


def _merge(x, g, oa, ob, oc, wg, bg, wb, wo, l):
    b, s, d = x.shape
    tm = TOKEN_TILE
    row = lambda c: pl.BlockSpec((None, tm, c), lambda bi, i: (bi, i, 0))
    return pl.pallas_call(
        _merge_kernel,
        grid=(b, s // tm),
        in_specs=[row(d), _full(g.shape), row(MIX_WIDTH), row(MIX_WIDTH), row(MIX_WIDTH),
                  _layer(wg, l), _full(bg.shape), _layer(wb, l), _layer(wo, l)],
        out_specs=row(d),
        out_shape=jax.ShapeDtypeStruct(x.shape, F32),
        compiler_params=_params(),
        name="merge",
    )(x, g, oa, ob, oc, wg, bg, wb, wo)


def _prep_w_in(w_in):
    dup = lambda w: jnp.concatenate([w[..., 0:64], w[..., 0:64], w[..., 64:128], w[..., 64:128]], axis=-1)
    kr = w_in[..., 2944:3008]
    return jnp.concatenate([
        w_in[..., 0:2048], dup(w_in[..., 2048:2176]), dup(w_in[..., 2176:2304]), w_in[..., 2304:2944],
        kr, kr[..., 32:64], kr[..., 0:32]], axis=-1).astype(BF16)


def _prep_w_uq(w_uq):
    pe = w_uq[..., C_NOPE:]
    half = C_ROPE // 2
    w = jnp.concatenate([w_uq, pe[..., half:], pe[..., :half]], axis=-1)
    return w.reshape(w_uq.shape[0], w_uq.shape[1], C_HEADS * 2 * LANES).astype(BF16)


def _alibi_key_table(c_alibi):
    term = c_alibi[None, :] * jnp.arange(K_TILE, dtype=F32)[:, None]
    pieces = []
    for _ in range(ALIBI_TERMS):
        piece = term.astype(BF16)
        pieces.append(piece)
        term = term - piece.astype(F32)
    tab = jnp.stack(pieces, axis=-1)
    tab = jnp.pad(tab, ((0, 0), (0, 0), (0, LANES - ALIBI_TERMS)))
    return tab.reshape(K_TILE, -1)


def kernel(x, ffn1_norm, ffn1_w13, ffn1_w2, mix_norm, w_in, w_gate, b_gate, a_lambda, a_subln,
           b_sinks, c_q_norm, c_w_uq, c_kv_norm, c_w_ukv, w_branch, w_out, ffn2_norm, ffn2_w13,
           ffn2_w2, final_norm):
    b, s, d = x.shape
    depth = w_in.shape[0]
    assert s % K_TILE == 0 and s % C_K_TILE == 0 and K_TILE % TOKEN_TILE == 0 and d % LANES == 0

    pos = jnp.arange(s, dtype=F32)
    inv_freq = ROPE_THETA ** (-jnp.arange(0, C_ROPE, 2, dtype=F32) / C_ROPE)
    ang = pos[:, None] * inv_freq[None, :]
    cos, sin = jnp.cos(ang), jnp.sin(ang)
    pad = jnp.zeros((s, LANES - C_ROPE), F32)
    cos2 = jnp.concatenate([cos, cos, pad], axis=1)
    sin2 = jnp.concatenate([-sin, sin, pad], axis=1)
    c_alibi = 2.0 ** (-8.0 * jnp.arange(1, A_HEADS + 1, dtype=F32) / A_HEADS) * LOG2E
    alibi = _alibi_key_table(c_alibi)
    band_bias = _band_bias_table()
    row = lambda v: v.reshape(1, -1)
    gf = row(final_norm)

    bf = lambda w: w.astype(BF16)
    w13_1, w2_1, w13_2, w2_2 = bf(ffn1_w13), bf(ffn1_w2), bf(ffn2_w13), bf(ffn2_w2)
    w_in_x, w_uq_x = _prep_w_in(w_in), _prep_w_uq(c_w_uq)
    w_kn = bf(c_w_ukv[..., :C_NOPE].reshape(depth, C_KV_RANK, -1))
    w_v = bf(c_w_ukv[..., C_NOPE:].reshape(depth, C_KV_RANK, -1))
    w_g, w_b, w_o = bf(w_gate), bf(w_branch), bf(w_out)

    for l in range(depth):
        x = _ffn(x, row(ffn1_norm[l]), w13_1, w2_1, gf, l, False)

        qa, ka, vat, qb, kb, vb, qc, kc, vct = _mix_in(
            x, row(mix_norm[l]), w_in_x, row(c_q_norm[l]), w_uq_x, row(c_kv_norm[l]), w_kn, w_v,
            cos2, sin2, alibi, l)

        lam_init = jnp.full((1, 1), 0.8 - 0.6 * math.exp(-0.3 * l), F32)
        oa = _attn_a(c_alibi, qa, ka, vat, a_lambda[l], lam_init, a_subln[l].reshape(-1, 1))
        ob = _attn_b(b_sinks[l], qb, kb, vb, band_bias)
        oc = _attn_c(qc, kc, vct)

        x = _merge(x, row(mix_norm[l]), oa, ob, oc, w_g, row(b_gate[l]), w_b, w_o, l)
        x = _ffn(x, row(ffn2_norm[l]), w13_2, w2_2, gf, l, l == depth - 1)
    return x
```

```python
import functools
import math

import jax
import jax.numpy as jnp
from jax import lax
from jax.experimental import pallas as pl
from jax.experimental.pallas import tpu as pltpu

BF16 = jnp.bfloat16
F32 = jnp.float32

EPS = 1e-6
NEG_INF = -1e30
LOG2E = math.log2(math.e)
ROPE_THETA = 10000.0

A_HEADS = 4
A_QK_DIM = 64
B_Q_HEADS = 8
B_KV_HEADS = 2
B_HEAD_DIM = 64
BAND = 128
B_STEP_BLOCKS = 4
C_HEADS = 4
C_Q_RANK = 384
C_KV_RANK = 256
C_NOPE = 128
C_ROPE = 64
MIX_WIDTH = 512
LANES = 128

VMEM_LIMIT = 52 * 1024 * 1024

TOKEN_TILE = 512
MXU_DIM = 256
FFN_CHUNK = 6 * MXU_DIM
K_TILE = 512
C_K_TILE = 1024
Q_TILE = 2048
A_Q_TILE = K_TILE
V_ROWS = LANES + 16
ACC_ROWS = LANES + 8
ALIBI_TERMS = 3


def _params():
    return pltpu.CompilerParams(vmem_limit_bytes=VMEM_LIMIT)


def _rms(x, g):
    return x * lax.rsqrt(jnp.mean(x * x, axis=-1, keepdims=True) + EPS) * g


def _dot(a, b):
    return jnp.dot(a, b, preferred_element_type=F32)


def _dot_nt(a, b):
    return lax.dot_general(a, b, (((1,), (1,)), ((), ())), preferred_element_type=F32)


def _full(shape):
    return pl.BlockSpec(shape, lambda *_: (0,) * len(shape))


def _layer(stack, l):
    return pl.BlockSpec((None,) + stack.shape[1:], lambda *_: (l,) + (0,) * (stack.ndim - 1),
                        pipeline_mode=pl.Buffered(1))


def _ffn_kernel(x_ref, g_ref, w13_ref, w2_ref, gf_ref, o_ref, *, final):
    x = x_ref[...]
    r = lax.rsqrt(jnp.mean(x * x, axis=-1, keepdims=True) + EPS)
    h = (x * g_ref[...]).astype(BF16)
    d_ff = w2_ref.shape[0]
    acc = jnp.zeros(x.shape, F32)
    for lo in range(0, d_ff, FFN_CHUNK):
        hi = min(lo + FFN_CHUNK, d_ff)
        a = _dot(h, w13_ref[:, lo:hi]) * r
        g = _dot(h, w13_ref[:, d_ff + lo:d_ff + hi]) * r
        act = (a * (1.0 / (1.0 + jnp.exp(-a))) * g).astype(BF16)
        acc = acc + _dot(act, w2_ref[lo:hi, :])
    y = x + 0.5 * acc
    if final:
        y = _rms(y, gf_ref[...])
    o_ref[...] = y


def _ffn(x, g, w13, w2, gf, l, final):
    b, s, d = x.shape
    tm = TOKEN_TILE
    xspec = pl.BlockSpec((None, tm, d), lambda bi, i: (bi, i, 0))
    return pl.pallas_call(
        functools.partial(_ffn_kernel, final=final),
        grid=(b, s // tm),
        in_specs=[xspec, _full(g.shape), _layer(w13, l), _layer(w2, l), _full(gf.shape)],
        out_specs=xspec,
        out_shape=jax.ShapeDtypeStruct(x.shape, F32),
        compiler_params=_params(),
        name="ffn",
    )(x, g, w13, w2, gf)


def _store_values_t(vt_ref, v, heads):
    vt = v.T.astype(BF16)
    ones = jnp.ones((V_ROWS - LANES, vt.shape[1]), BF16)
    for hd in range(heads):
        vt_ref[hd * V_ROWS:hd * V_ROWS + LANES, :] = vt[hd * LANES:(hd + 1) * LANES]
        vt_ref[hd * V_ROWS + LANES:(hd + 1) * V_ROWS, :] = ones


def _mix_in_kernel(x_ref, g_ref, w_ref, cqg_ref, wq_ref, ckvg_ref, wkn_ref, wv_ref,
                   cos_ref, sin_ref, alibi_ref,
                   qa_ref, ka_ref, vat_ref, qb_ref, kb_ref, vbt_ref, qc_ref, kc_ref, vct_ref):
    x = x_ref[...]
    r = lax.rsqrt(jnp.mean(x * x, axis=-1, keepdims=True) + EPS)
    h = (x * g_ref[...]).astype(BF16)
    cos2 = cos_ref[...]
    sin2 = sin_ref[...]

    zc = _dot(h, w_ref[:, 2560:3328]) * r
    cqn = _rms(zc[:, 0:C_Q_RANK], cqg_ref[...]).astype(BF16)
    ckvn = _rms(zc[:, C_Q_RANK:C_Q_RANK + C_KV_RANK], ckvg_ref[...]).astype(BF16)
    kr = zc[:, C_Q_RANK + C_KV_RANK:]
    kc_ref[:, 512:640] = (kr * cos2 + pltpu.roll(kr, C_ROPE, 1) * sin2).astype(BF16)

    za = _dot(h, w_ref[:, 0:1536]) * r
    qa_ref[...] = (za[:, 0:512] * (A_QK_DIM ** -0.5 * LOG2E)).astype(BF16)
    for hd in range(A_HEADS):
        ka_ref[:, 2 * hd * LANES:(2 * hd + 1) * LANES] = (
            za[:, 512 + hd * LANES:512 + (hd + 1) * LANES].astype(BF16))
        ka_ref[:, (2 * hd + 1) * LANES:(2 * hd + 2) * LANES] = alibi_ref[:, hd * LANES:(hd + 1) * LANES]
    _store_values_t(vat_ref, za[:, 1024:1536], A_HEADS)

    c_scale = (C_NOPE + C_ROPE) ** -0.5 * LOG2E
    q = _dot(cqn, wq_ref[...])
    for hd in range(C_HEADS):
        lo = hd * 2 * LANES
        qc_ref[:, lo:lo + LANES] = (q[:, lo:lo + LANES] * c_scale).astype(BF16)
        qp = q[:, lo + LANES:lo + 2 * LANES]
        qpr = qp * cos2 + pltpu.roll(qp, C_ROPE, 1) * sin2
        qc_ref[:, lo + LANES:lo + 2 * LANES] = (qpr * c_scale).astype(BF16)
    kc_ref[:, 0:512] = _dot(ckvn, wkn_ref[...]).astype(BF16)
    _store_values_t(vct_ref, _dot(ckvn, wv_ref[...]), C_HEADS)

    zb = _dot(h, w_ref[:, 1536:2560]) * r
    qb_ref[...] = (zb[:, 0:512] * (B_HEAD_DIM ** -0.5 * LOG2E)).astype(BF16)
    kb_ref[...] = zb[:, 512:768].astype(BF16)
    vbt_ref[...] = zb[:, 768:1024].T.astype(BF16)


def _mix_in(x, g, w, cqg, wq, ckvg, wkn, wv, cos2, sin2, alibi, l):
    b, s, d = x.shape
    tm = TOKEN_TILE
    row = lambda c: pl.BlockSpec((None, tm, c), lambda bi, i: (bi, i, 0))
    colT = pl.BlockSpec((None, 4 * V_ROWS, tm), lambda bi, i: (bi, 0, i))
    tab = pl.BlockSpec((tm, LANES), lambda bi, i: (i, 0))
    sd = lambda c: jax.ShapeDtypeStruct((b, s, c), BF16)
    sdT = jax.ShapeDtypeStruct((b, 4 * V_ROWS, s), BF16)
    return pl.pallas_call(
        _mix_in_kernel,
        grid=(b, s // tm),
        in_specs=[row(d), _full(g.shape), _layer(w, l), _full(cqg.shape), _layer(wq, l),
                  _full(ckvg.shape), _layer(wkn, l), _layer(wv, l), tab, tab,
                  pl.BlockSpec((tm, alibi.shape[1]), lambda bi, i: (i % (K_TILE // tm), 0))],
        out_specs=[row(512), row(1024), colT, row(512), row(256),
                   pl.BlockSpec((None, 2 * LANES, tm), lambda bi, i: (bi, 0, i)),
                   row(1024), row(640), colT],
        out_shape=[sd(512), sd(1024), sdT, sd(512), sd(256),
                   jax.ShapeDtypeStruct((b, 2 * LANES, s), BF16), sd(1024), sd(640), sdT],
        compiler_params=_params(),
        name="mix_in",
    )(x, g, w, cqg, wq, ckvg, wkn, wv, cos2, sin2, alibi)


OVERFLOW_GUARD = 2.0 ** 60
PROBE_KEYS = 128
P_BUFFERS = 2


def _value_product(vt, p):
    return _dot(vt, p)[:ACC_ROWS]


def _tile_stats(st, vt):
    m_t = jnp.max(st, axis=0, keepdims=True)
    return m_t, _value_product(vt, jnp.exp2(st - m_t).astype(BF16))


def _flash_fixed_reference(n, qk_fn, v_fn, ref_fn, acc, p_refs):
    def weights(t):
        p_refs[t % len(p_refs)][...] = jnp.exp2(qk_fn(t) - ref_fn(t)).astype(BF16)

    if n:
        weights(0)
    for t in range(n):
        if t + 1 < n:
            weights(t + 1)
        acc = acc + _value_product(v_fn(t), p_refs[t % len(p_refs)][...])
    return acc


def _flash_running_max(n, score_fn, v_fn, width):
    def body(kt, carry):
        m, acc = carry
        m_t, pv_t = _tile_stats(score_fn(kt), v_fn(kt))
        m_new = jnp.maximum(m, m_t)
        return m_new, jnp.exp2(m - m_new) * acc + jnp.exp2(m_t - m_new) * pv_t

    init = (jnp.full((1, width), -jnp.inf, F32), jnp.zeros((ACC_ROWS, width), F32))
    return lax.fori_loop(0, n, body, init)[1]


def _denominator_overflowed(acc):
    l = acc[LANES:LANES + 1]
    return jnp.max(jnp.where(l < OVERFLOW_GUARD, 0.0, 1.0)) > 0.5


def _attn_a_kernel(c_ref, q_ref, k_ref, vt_ref, lam_ref, lam_init_ref, g_ref, o_ref,
                   qq_ref, sd_ref, *p_refs):
    hd = pl.program_id(1)
    qi = pl.program_id(2)
    tq = q_ref.shape[0]
    assert tq == K_TILE
    n_kt = k_ref.shape[0] // K_TILE
    c = c_ref[hd]

    q = q_ref[...]
    lane = lax.broadcasted_iota(jnp.int32, q.shape, 1)
    zero = jnp.zeros_like(q)
    qm = jnp.concatenate([jnp.where(lane < A_QK_DIM, q, zero),
                          jnp.where(lane >= A_QK_DIM, q, zero)], axis=0)
    sel = (lax.broadcasted_iota(jnp.int32, qm.shape, 1) < ALIBI_TERMS).astype(BF16)
    for slot, sign in enumerate((-1.0, 1.0, 0.0)):
        qq_ref[slot, :, 0:LANES] = qm
        qq_ref[slot, :, LANES:2 * LANES] = sel * sign

    def keys(kt):
        return k_ref[pl.ds(pl.multiple_of(kt * K_TILE, K_TILE), K_TILE), :]

    def values(kt):
        return vt_ref[:, pl.ds(pl.multiple_of(kt * K_TILE, K_TILE), K_TILE)]

    q0 = qi * tq
    rel = (lax.broadcasted_iota(jnp.int32, (K_TILE, tq), 0)
           - lax.broadcasted_iota(jnp.int32, (K_TILE, tq), 1))

    def biased_scores(kt):
        dist = jnp.abs(rel + (kt * K_TILE - q0)).astype(F32) * c
        return _dot_nt(keys(kt), qq_ref[2]) - jnp.concatenate([dist, dist], axis=1)

    def finish(acc):
        lp = lam_ref[...]
        lam_init = lam_init_ref[...]
        lam = (jnp.exp(jnp.sum(lp[0:1] * lp[1:2], axis=1, keepdims=True))
               - jnp.exp(jnp.sum(lp[2:3] * lp[3:4], axis=1, keepdims=True)) + lam_init)
        inv_l = 1.0 / acc[LANES:LANES + 1]
        o = acc[:LANES, :tq] * inv_l[:, :tq] - lam * (acc[:LANES, tq:] * inv_l[:, tq:])
        o = o * lax.rsqrt(jnp.mean(o * o, axis=0, keepdims=True) + EPS) * g_ref[...]
        o = o * (1.0 - lam_init)
        o_ref[...] = o.T.astype(BF16)

    kd = qi
    sd_ref[...] = biased_scores(kd)
    m0 = jnp.max(sd_ref[...], axis=0, keepdims=True)
    q_row = lax.broadcasted_iota(jnp.int32, (1, tq), 1).astype(F32) * c
    q_row = jnp.concatenate([q_row, q_row], axis=1)

    def key_tile(t):
        t = jnp.int32(t - 1)
        kt = t + (t >= kd).astype(jnp.int32)
        return kt, (kt < kd).astype(jnp.int32)

    def qk_fn(t):
        if t == 0:
            return sd_ref[...]
        kt, before = key_tile(t)
        return _dot_nt(keys(kt), qq_ref[before])

    def v_fn(t):
        return values(kd if t == 0 else key_tile(t)[0])

    def ref_fn(t):
        if t == 0:
            return m0
        kt, before = key_tile(t)
        sgn = (2 * before - 1).astype(F32)
        return m0 - ((kt * K_TILE - q0).astype(F32) * c - q_row) * sgn

    acc = _flash_fixed_reference(n_kt, qk_fn, v_fn, ref_fn, jnp.zeros((ACC_ROWS, 2 * tq), F32),
                                 p_refs)
    finish(acc)

    @pl.when(_denominator_overflowed(acc))
    def _():
        finish(_flash_running_max(n_kt, biased_scores, values, 2 * tq))


def _attn_a(c_alibi, qa, ka, vat, lam_p, lam_init, g_col):
    b, s, _ = qa.shape
    tq = A_Q_TILE
    return pl.pallas_call(
        _attn_a_kernel,
        grid=(b, A_HEADS, s // tq),
        in_specs=[
            pl.BlockSpec(memory_space=pltpu.SMEM),
            pl.BlockSpec((None, tq, LANES), lambda bi, h, i: (bi, i, h)),
            pl.BlockSpec((None, s, 2 * LANES), lambda bi, h, i: (bi, 0, h)),
            pl.BlockSpec((None, V_ROWS, s), lambda bi, h, i: (bi, h, 0)),
            _full(lam_p.shape), _full(lam_init.shape), _full(g_col.shape),
        ],
        out_specs=pl.BlockSpec((None, tq, LANES), lambda bi, h, i: (bi, i, h)),
        out_shape=jax.ShapeDtypeStruct((b, s, MIX_WIDTH), BF16),
        scratch_shapes=[pltpu.VMEM((3, 2 * tq, 2 * LANES), BF16), pltpu.VMEM((K_TILE, 2 * tq), F32),
                       ] + [pltpu.VMEM((K_TILE, 2 * tq), BF16)] * P_BUFFERS,
        compiler_params=_params(),
        name="attn_a",
    )(c_alibi, qa, ka, vat, lam_p, lam_init, g_col)


def _attn_c_kernel(q_ref, kn_ref, kpe_ref, vt_ref, o_ref, *p_refs):
    q = q_ref[...]
    tq = q.shape[0]
    tk = C_K_TILE
    n_kt = kn_ref.shape[0] // tk

    def scores(kt):
        k0 = pl.multiple_of(kt * tk, tk)
        k = jnp.concatenate([kn_ref[pl.ds(k0, tk), :], kpe_ref[pl.ds(k0, tk), :]], axis=1)
        return _dot_nt(k, q)

    def values(kt):
        return vt_ref[:, pl.ds(pl.multiple_of(kt * tk, tk), tk)]

    def finish(acc):
        o_ref[...] = (acc[:LANES] * (1.0 / acc[LANES:LANES + 1])).T.astype(BF16)

    probe = jnp.concatenate([kn_ref[0:PROBE_KEYS, :], kpe_ref[0:PROBE_KEYS, :]], axis=1)
    m0 = jnp.max(_dot_nt(probe, q), axis=0, keepdims=True)
    acc = _flash_fixed_reference(n_kt, lambda t: scores(jnp.int32(t)), lambda t: values(jnp.int32(t)),
                                 lambda t: m0, jnp.zeros((ACC_ROWS, tq), F32), p_refs)
    finish(acc)

    @pl.when(_denominator_overflowed(acc))
    def _():
        finish(_flash_running_max(n_kt, scores, values, tq))


def _attn_c(qc, kc, vct):
    b, s, _ = qc.shape
    tq = Q_TILE
    return pl.pallas_call(
        _attn_c_kernel,
        grid=(b, C_HEADS, s // tq),
        in_specs=[
            pl.BlockSpec((None, tq, 2 * LANES), lambda bi, h, i: (bi, i, h)),
            pl.BlockSpec((None, s, LANES), lambda bi, h, i: (bi, 0, h)),
            pl.BlockSpec((None, s, LANES), lambda bi, h, i: (bi, 0, C_HEADS)),
            pl.BlockSpec((None, V_ROWS, s), lambda bi, h, i: (bi, h, 0)),
        ],
        out_specs=pl.BlockSpec((None, tq, LANES), lambda bi, h, i: (bi, i, h)),
        out_shape=jax.ShapeDtypeStruct((b, s, MIX_WIDTH), BF16),
        scratch_shapes=[pltpu.VMEM((C_K_TILE, tq), BF16)] * P_BUFFERS,
        compiler_params=_params(),
        name="attn_c",
    )(qc, kc, kc, vct)


def _attn_b_kernel(sinks_ref, q_ref, kp_ref, ko_ref, kx_ref, vp_ref, vo_ref, vx_ref, bias_ref, o_ref):
    first = pl.program_id(1) * B_STEP_BLOCKS
    nb = pl.num_programs(1) * B_STEP_BLOCKS
    group = B_Q_HEADS // B_KV_HEADS
    half = lax.broadcasted_iota(jnp.int32, (BAND, LANES), 1) // B_HEAD_DIM
    row_half = lax.broadcasted_iota(jnp.int32, (LANES, BAND), 0) // B_HEAD_DIM

    for hk in range(B_KV_HEADS):
        heads = range(hk * group, (hk + 1) * group)
        ksl = slice(hk * LANES, (hk + 1) * LANES)
        k_all = jnp.concatenate([kp_ref[:, ksl], ko_ref[:, ksl], kx_ref[:, ksl]], axis=0)
        vt_all = jnp.concatenate([vp_ref[ksl, :], vo_ref[ksl, :], vx_ref[ksl, :]], axis=1)
        sink = jnp.concatenate([jnp.full((1, BAND), sinks_ref[h] * LOG2E, F32) for h in heads], axis=1)
        for u in range(B_STEP_BLOCKS):
            rows = slice(u * BAND, (u + 1) * BAND)
            qg = []
            for h in heads:
                qh = q_ref[rows, (h // 2) * LANES:(h // 2 + 1) * LANES]
                qg.append(jnp.where(half == h % 2, qh, jnp.zeros_like(qh)))
            bias = bias_ref[hk]
            if u == 0:
                pen = jnp.where(first == 0, NEG_INF, 0.0)
                bias = jnp.concatenate([bias[:BAND] + pen, bias[BAND:]], axis=0)
            if u == B_STEP_BLOCKS - 1:
                pen = jnp.where(first + u == nb - 1, NEG_INF, 0.0)
                bias = jnp.concatenate([bias[:2 * BAND], bias[2 * BAND:] + pen], axis=0)
            sc = _dot_nt(k_all[u * BAND:(u + 3) * BAND], jnp.concatenate(qg, axis=0))
            sc = sc + bias
            m = jnp.maximum(jnp.max(sc, axis=0, keepdims=True), sink)
            e = jnp.exp2(sc - m)
            den = jnp.sum(e, axis=0, keepdims=True) + jnp.exp2(sink - m)
            ot = _dot(vt_all[:, u * BAND:(u + 3) * BAND], e.astype(BF16)) * (1.0 / den)
            for c in range(group // 2):
                pair = jnp.where(row_half == 0, ot[:, 2 * c * BAND:(2 * c + 1) * BAND],
                                 ot[:, (2 * c + 1) * BAND:(2 * c + 2) * BAND])
                col = hk * (group // 2) + c
                o_ref[rows, col * LANES:(col + 1) * LANES] = pair.T.astype(BF16)


def _band_bias_table():
    kj = jnp.arange(3 * BAND)[:, None] - BAND
    dist = jnp.abs(kj - jnp.arange(BAND)[None, :])
    per_head = [jnp.where(dist <= BAND, -(2.0 ** -(h + 1) * LOG2E) * dist.astype(F32), NEG_INF)
                for h in range(B_Q_HEADS)]
    group = B_Q_HEADS // B_KV_HEADS
    return jnp.stack([jnp.concatenate(per_head[g * group:(g + 1) * group], axis=1)
                      for g in range(B_KV_HEADS)])


def _attn_b(sinks, qb, kb, vbt, bias):
    b, s, _ = qb.shape
    nb = s // BAND
    nq = B_STEP_BLOCKS
    rows = nq * BAND
    prev = lambda j: jnp.maximum(j * nq - 1, 0)
    nxt = lambda j: jnp.minimum(j * nq + nq, nb - 1)
    k_edge = lambda f: pl.BlockSpec((None, BAND, 2 * LANES), lambda bi, j: (bi, f(j), 0))
    v_edge = lambda f: pl.BlockSpec((None, 2 * LANES, BAND), lambda bi, j: (bi, 0, f(j)))
    k_own = pl.BlockSpec((None, rows, 2 * LANES), lambda bi, j: (bi, j, 0))
    v_own = pl.BlockSpec((None, 2 * LANES, rows), lambda bi, j: (bi, 0, j))
    return pl.pallas_call(
        _attn_b_kernel,
        grid=(b, nb // nq),
        in_specs=[pl.BlockSpec(memory_space=pltpu.SMEM),
                  pl.BlockSpec((None, rows, MIX_WIDTH), lambda bi, j: (bi, j, 0)),
                  k_edge(prev), k_own, k_edge(nxt), v_edge(prev), v_own, v_edge(nxt),
                  _full(bias.shape)],
        out_specs=pl.BlockSpec((None, rows, MIX_WIDTH), lambda bi, j: (bi, j, 0)),
        out_shape=jax.ShapeDtypeStruct((b, s, MIX_WIDTH), BF16),
        compiler_params=_params(),
        name="attn_b",
    )(sinks, qb, kb, kb, kb, vbt, vbt, vbt, bias)


def _merge_kernel(x_ref, g_ref, oa_ref, ob_ref, oc_ref, wg_ref, bg_ref, wb_ref, wo_ref, o_ref):
    x = x_ref[...]
    d = x.shape[1]
    r = lax.rsqrt(jnp.mean(x * x, axis=-1, keepdims=True) + EPS)
    h = (x * g_ref[...]).astype(BF16)
    merged = jnp.zeros(x.shape, F32)
    for n, br_ref in enumerate((oa_ref, ob_ref, oc_ref)):
        z = _dot(h, wg_ref[:, n * d:(n + 1) * d]) * r + bg_ref[:, n * d:(n + 1) * d]
        gate = 1.0 / (1.0 + jnp.exp(-z))
        merged = merged + gate * _dot(br_ref[...], wb_ref[n])
    o_ref[...] = x + _dot(merged.astype(BF16), wo_ref[...])


def _merge(x, g, oa, ob, oc, wg, bg, wb, wo, l):
    b, s, d = x.shape
    tm = TOKEN_TILE
    row = lambda c: pl.BlockSpec((None, tm, c), lambda bi, i: (bi, i, 0))
    return pl.pallas_call(
        _merge_kernel,
        grid=(b, s // tm),
        in_specs=[row(d), _full(g.shape), row(MIX_WIDTH), row(MIX_WIDTH), row(MIX_WIDTH),
                  _layer(wg, l), _full(bg.shape), _layer(wb, l), _layer(wo, l)],
        out_specs=row(d),
        out_shape=jax.ShapeDtypeStruct(x.shape, F32),
        compiler_params=_params(),
        name="merge",
    )(x, g, oa, ob, oc, wg, bg, wb, wo)


def _prep_w_in(w_in):
    dup = lambda w: jnp.concatenate([w[..., 0:64], w[..., 0:64], w[..., 64:128], w[..., 64:128]], axis=-1)
    kr = w_in[..., 2944:3008]
    return jnp.concatenate([
        w_in[..., 0:2048], dup(w_in[..., 2048:2176]), dup(w_in[..., 2176:2304]), w_in[..., 2304:2944],
        kr, kr[..., 32:64], kr[..., 0:32]], axis=-1).astype(BF16)


def _prep_w_uq(w_uq):
    pe = w_uq[..., C_NOPE:]
    half = C_ROPE // 2
    w = jnp.concatenate([w_uq, pe[..., half:], pe[..., :half]], axis=-1)
    return w.reshape(w_uq.shape[0], w_uq.shape[1], C_HEADS * 2 * LANES).astype(BF16)


def _alibi_key_table(c_alibi):
    term = c_alibi[None, :] * jnp.arange(K_TILE, dtype=F32)[:, None]
    pieces = []
    for _ in range(ALIBI_TERMS):
        piece = term.astype(BF16)
        pieces.append(piece)
        term = term - piece.astype(F32)
    tab = jnp.stack(pieces, axis=-1)
    tab = jnp.pad(tab, ((0, 0), (0, 0), (0, LANES - ALIBI_TERMS)))
    return tab.reshape(K_TILE, -1)


def kernel(x, ffn1_norm, ffn1_w13, ffn1_w2, mix_norm, w_in, w_gate, b_gate, a_lambda, a_subln,
           b_sinks, c_q_norm, c_w_uq, c_kv_norm, c_w_ukv, w_branch, w_out, ffn2_norm, ffn2_w13,
           ffn2_w2, final_norm):
    b, s, d = x.shape
    depth = w_in.shape[0]
    assert s % K_TILE == 0 and s % C_K_TILE == 0 and K_TILE % TOKEN_TILE == 0 and d % LANES == 0

    pos = jnp.arange(s, dtype=F32)
    inv_freq = ROPE_THETA ** (-jnp.arange(0, C_ROPE, 2, dtype=F32) / C_ROPE)
    ang = pos[:, None] * inv_freq[None, :]
    cos, sin = jnp.cos(ang), jnp.sin(ang)
    pad = jnp.zeros((s, LANES - C_ROPE), F32)
    cos2 = jnp.concatenate([cos, cos, pad], axis=1)
    sin2 = jnp.concatenate([-sin, sin, pad], axis=1)
    c_alibi = 2.0 ** (-8.0 * jnp.arange(1, A_HEADS + 1, dtype=F32) / A_HEADS) * LOG2E
    alibi = _alibi_key_table(c_alibi)
    band_bias = _band_bias_table()
    row = lambda v: v.reshape(1, -1)
    gf = row(final_norm)

    bf = lambda w: w.astype(BF16)
    w13_1, w2_1, w13_2, w2_2 = bf(ffn1_w13), bf(ffn1_w2), bf(ffn2_w13), bf(ffn2_w2)
    w_in_x, w_uq_x = _prep_w_in(w_in), _prep_w_uq(c_w_uq)
    w_kn = bf(c_w_ukv[..., :C_NOPE].reshape(depth, C_KV_RANK, -1))
    w_v = bf(c_w_ukv[..., C_NOPE:].reshape(depth, C_KV_RANK, -1))
    w_g, w_b, w_o = bf(w_gate), bf(w_branch), bf(w_out)

    for l in range(depth):
        x = _ffn(x, row(ffn1_norm[l]), w13_1, w2_1, gf, l, False)

        qa, ka, vat, qb, kb, vb, qc, kc, vct = _mix_in(
            x, row(mix_norm[l]), w_in_x, row(c_q_norm[l]), w_uq_x, row(c_kv_norm[l]), w_kn, w_v,
            cos2, sin2, alibi, l)

        lam_init = jnp.full((1, 1), 0.8 - 0.6 * math.exp(-0.3 * l), F32)
        oa = _attn_a(c_alibi, qa, ka, vat, a_lambda[l], lam_init, a_subln[l].reshape(-1, 1))
        ob = _attn_b(b_sinks[l], qb, kb, vb, band_bias)
        oc = _attn_c(qc, kc, vct)

        x = _merge(x, row(mix_norm[l]), oa, ob, oc, w_g, row(b_gate[l]), w_b, w_o, l)
        x = _ffn(x, row(ffn2_norm[l]), w13_2, w2_2, gf, l, l == depth - 1)
    return x
```

```python
import functools
import math

import jax
import jax.numpy as jnp
from jax import lax
from jax.experimental import pallas as pl
from jax.experimental.pallas import tpu as pltpu

BF16 = jnp.bfloat16
F32 = jnp.float32

EPS = 1e-6
NEG_INF = -1e30
LOG2E = math.log2(math.e)
ROPE_THETA = 10000.0

A_HEADS = 4
A_QK_DIM = 64
B_Q_HEADS = 8
B_KV_HEADS = 2
B_HEAD_DIM = 64
BAND = 128
B_STEP_BLOCKS = 4
C_HEADS = 4
C_Q_RANK = 384
C_KV_RANK = 256
C_NOPE = 128
C_ROPE = 64
MIX_WIDTH = 512
LANES = 128

VMEM_LIMIT = 52 * 1024 * 1024

TOKEN_TILE = 512
MXU_DIM = 256
FFN_CHUNK = 6 * MXU_DIM
K_TILE = 512
C_K_TILE = 1024
Q_TILE = 2048
A_Q_TILE = K_TILE
V_ROWS = LANES + 16
ACC_ROWS = LANES + 8
ALIBI_TERMS = 3


def _params():
    return pltpu.CompilerParams(vmem_limit_bytes=VMEM_LIMIT)


def _rms(x, g):
    return x * lax.rsqrt(jnp.mean(x * x, axis=-1, keepdims=True) + EPS) * g


def _dot(a, b):
    return jnp.dot(a, b, preferred_element_type=F32)


def _dot_nt(a, b):
    return lax.dot_general(a, b, (((1,), (1,)), ((), ())), preferred_element_type=F32)


def _full(shape):
    return pl.BlockSpec(shape, lambda *_: (0,) * len(shape))


def _layer(stack, l):
    return pl.BlockSpec((None,) + stack.shape[1:], lambda *_: (l,) + (0,) * (stack.ndim - 1),
                        pipeline_mode=pl.Buffered(1))


def _ffn_kernel(x_ref, g_ref, w13_ref, w2_ref, gf_ref, o_ref, *, final):
    x = x_ref[...]
    r = lax.rsqrt(jnp.mean(x * x, axis=-1, keepdims=True) + EPS)
    h = (x * g_ref[...]).astype(BF16)
    d_ff = w2_ref.shape[0]
    acc = jnp.zeros(x.shape, F32)
    for lo in range(0, d_ff, FFN_CHUNK):
        hi = min(lo + FFN_CHUNK, d_ff)
        a = _dot(h, w13_ref[:, lo:hi]) * r
        g = _dot(h, w13_ref[:, d_ff + lo:d_ff + hi]) * r
        act = (a * (1.0 / (1.0 + jnp.exp(-a))) * g).astype(BF16)
        acc = acc + _dot(act, w2_ref[lo:hi, :])
    y = x + 0.5 * acc
    if final:
        y = _rms(y, gf_ref[...])
    o_ref[...] = y


def _ffn(x, g, w13, w2, gf, l, final):
    b, s, d = x.shape
    tm = TOKEN_TILE
    xspec = pl.BlockSpec((None, tm, d), lambda bi, i: (bi, i, 0))
    return pl.pallas_call(
        functools.partial(_ffn_kernel, final=final),
        grid=(b, s // tm),
        in_specs=[xspec, _full(g.shape), _layer(w13, l), _layer(w2, l), _full(gf.shape)],
        out_specs=xspec,
        out_shape=jax.ShapeDtypeStruct(x.shape, F32),
        compiler_params=_params(),
        name="ffn",
    )(x, g, w13, w2, gf)


def _store_values_t(vt_ref, v, heads):
    vt = v.T.astype(BF16)
    ones = jnp.ones((V_ROWS - LANES, vt.shape[1]), BF16)
    for hd in range(heads):
        vt_ref[hd * V_ROWS:hd * V_ROWS + LANES, :] = vt[hd * LANES:(hd + 1) * LANES]
        vt_ref[hd * V_ROWS + LANES:(hd + 1) * V_ROWS, :] = ones


def _mix_in_kernel(x_ref, g_ref, w_ref, cqg_ref, wq_ref, ckvg_ref, wkn_ref, wv_ref,
                   cos_ref, sin_ref, alibi_ref,
                   qa_ref, ka_ref, vat_ref, qb_ref, kb_ref, vbt_ref, qc_ref, kc_ref, vct_ref, kn2_ref):
    x = x_ref[...]
    r = lax.rsqrt(jnp.mean(x * x, axis=-1, keepdims=True) + EPS)
    h = (x * g_ref[...]).astype(BF16)
    cos2 = cos_ref[...]
    sin2 = sin_ref[...]

    zc = _dot(h, w_ref[:, 2560:3328]) * r
    cqn = _rms(zc[:, 0:C_Q_RANK], cqg_ref[...]).astype(BF16)
    ckvn = _rms(zc[:, C_Q_RANK:C_Q_RANK + C_KV_RANK], ckvg_ref[...]).astype(BF16)
    kr = zc[:, C_Q_RANK + C_KV_RANK:]
    kc_ref[:, 512:640] = (kr * cos2 + pltpu.roll(kr, C_ROPE, 1) * sin2).astype(BF16)

    za = _dot(h, w_ref[:, 0:1536]) * r
    qa_ref[...] = (za[:, 0:512] * (A_QK_DIM ** -0.5 * LOG2E)).astype(BF16)
    for hd in range(A_HEADS):
        ka_ref[:, 2 * hd * LANES:(2 * hd + 1) * LANES] = (
            za[:, 512 + hd * LANES:512 + (hd + 1) * LANES].astype(BF16))
        ka_ref[:, (2 * hd + 1) * LANES:(2 * hd + 2) * LANES] = alibi_ref[:, hd * LANES:(hd + 1) * LANES]
    _store_values_t(vat_ref, za[:, 1024:1536], A_HEADS)
    kk = (za[:, 512:1024] * za[:, 512:1024]).astype(BF16)
    head_of_col = lax.broadcasted_iota(jnp.int32, (A_HEADS * LANES, LANES), 0) // LANES
    ind = (head_of_col == lax.broadcasted_iota(jnp.int32, (A_HEADS * LANES, LANES), 1)).astype(BF16)
    kn2 = jnp.broadcast_to(jnp.max(_dot(kk, ind), axis=0, keepdims=True), kn2_ref.shape)

    @pl.when(pl.program_id(1) == 0)
    def _():
        kn2_ref[...] = kn2

    @pl.when(pl.program_id(1) > 0)
    def _():
        kn2_ref[...] = jnp.maximum(kn2_ref[...], kn2)

    c_scale = (C_NOPE + C_ROPE) ** -0.5 * LOG2E
    q = _dot(cqn, wq_ref[...])
    for hd in range(C_HEADS):
        lo = hd * 2 * LANES
        qc_ref[:, lo:lo + LANES] = (q[:, lo:lo + LANES] * c_scale).astype(BF16)
        qp = q[:, lo + LANES:lo + 2 * LANES]
        qpr = qp * cos2 + pltpu.roll(qp, C_ROPE, 1) * sin2
        qc_ref[:, lo + LANES:lo + 2 * LANES] = (qpr * c_scale).astype(BF16)
    kc_ref[:, 0:512] = _dot(ckvn, wkn_ref[...]).astype(BF16)
    _store_values_t(vct_ref, _dot(ckvn, wv_ref[...]), C_HEADS)

    zb = _dot(h, w_ref[:, 1536:2560]) * r
    qb_ref[...] = (zb[:, 0:512] * (B_HEAD_DIM ** -0.5 * LOG2E)).astype(BF16)
    kb_ref[...] = zb[:, 512:768].astype(BF16)
    vbt_ref[...] = zb[:, 768:1024].T.astype(BF16)


def _mix_in(x, g, w, cqg, wq, ckvg, wkn, wv, cos2, sin2, alibi, l):
    b, s, d = x.shape
    tm = TOKEN_TILE
    row = lambda c: pl.BlockSpec((None, tm, c), lambda bi, i: (bi, i, 0))
    colT = pl.BlockSpec((None, 4 * V_ROWS, tm), lambda bi, i: (bi, 0, i))
    tab = pl.BlockSpec((tm, LANES), lambda bi, i: (i, 0))
    sd = lambda c: jax.ShapeDtypeStruct((b, s, c), BF16)
    sdT = jax.ShapeDtypeStruct((b, 4 * V_ROWS, s), BF16)
    return pl.pallas_call(
        _mix_in_kernel,
        grid=(b, s // tm),
        in_specs=[row(d), _full(g.shape), _layer(w, l), _full(cqg.shape), _layer(wq, l),
                  _full(ckvg.shape), _layer(wkn, l), _layer(wv, l), tab, tab,
                  pl.BlockSpec((tm, alibi.shape[1]), lambda bi, i: (i % (K_TILE // tm), 0))],
        out_specs=[row(512), row(1024), colT, row(512), row(256),
                   pl.BlockSpec((None, 2 * LANES, tm), lambda bi, i: (bi, 0, i)),
                   row(1024), row(640), colT,
                   pl.BlockSpec((None, 8, LANES), lambda bi, i: (bi, 0, 0))],
        out_shape=[sd(512), sd(1024), sdT, sd(512), sd(256),
                   jax.ShapeDtypeStruct((b, 2 * LANES, s), BF16), sd(1024), sd(640), sdT,
                   jax.ShapeDtypeStruct((b, 8, LANES), F32)],
        compiler_params=_params(),
        name="mix_in",
    )(x, g, w, cqg, wq, ckvg, wkn, wv, cos2, sin2, alibi)


OVERFLOW_GUARD = 2.0 ** 60
PROBE_KEYS = 128
SKIP_EXPONENT = 140.0
NORM_MARGIN = 1.02
MASKED_REFERENCE = 1e30
P_BUFFERS = 2


def _value_product(vt, p):
    return _dot(vt, p)[:ACC_ROWS]


def _tile_stats(st, vt):
    m_t = jnp.max(st, axis=0, keepdims=True)
    return m_t, _value_product(vt, jnp.exp2(st - m_t).astype(BF16))


def _flash_fixed_reference(n, qk_fn, v_fn, ref_fn, acc, p_refs):
    def weights(t):
        p_refs[t % len(p_refs)][...] = jnp.exp2(qk_fn(t) - ref_fn(t)).astype(BF16)

    if n:
        weights(0)
    for t in range(n):
        if t + 1 < n:
            weights(t + 1)
        acc = acc + _value_product(v_fn(t), p_refs[t % len(p_refs)][...])
    return acc


def _flash_running_max(n, score_fn, v_fn, width):
    def body(kt, carry):
        m, acc = carry
        m_t, pv_t = _tile_stats(score_fn(kt), v_fn(kt))
        m_new = jnp.maximum(m, m_t)
        return m_new, jnp.exp2(m - m_new) * acc + jnp.exp2(m_t - m_new) * pv_t

    init = (jnp.full((1, width), -jnp.inf, F32), jnp.zeros((ACC_ROWS, width), F32))
    return lax.fori_loop(0, n, body, init)[1]


def _denominator_overflowed(acc):
    l = acc[LANES:LANES + 1]
    return jnp.max(jnp.where(l < OVERFLOW_GUARD, 0.0, 1.0)) > 0.5


def _attn_a_kernel(c_ref, q_ref, k_ref, vt_ref, kn2_ref, lam_ref, lam_init_ref, g_ref, o_ref,
                   qq_ref, sd_ref, acc_ref, *p_refs):
    hd = pl.program_id(1)
    qi = pl.program_id(2)
    tq = q_ref.shape[0]
    assert tq == K_TILE
    n_kt = k_ref.shape[0] // K_TILE
    c = c_ref[hd]

    q = q_ref[...]
    lane = lax.broadcasted_iota(jnp.int32, q.shape, 1)
    zero = jnp.zeros_like(q)
    qm = jnp.concatenate([jnp.where(lane < A_QK_DIM, q, zero),
                          jnp.where(lane >= A_QK_DIM, q, zero)], axis=0)
    sel = (lax.broadcasted_iota(jnp.int32, qm.shape, 1) < ALIBI_TERMS).astype(BF16)
    for slot, sign in enumerate((-1.0, 1.0, 0.0)):
        qq_ref[slot, :, 0:LANES] = qm
        qq_ref[slot, :, LANES:2 * LANES] = sel * sign

    def keys(kt):
        return k_ref[pl.ds(pl.multiple_of(kt * K_TILE, K_TILE), K_TILE), :]

    def values(kt):
        return vt_ref[:, pl.ds(pl.multiple_of(kt * K_TILE, K_TILE), K_TILE)]

    q0 = qi * tq
    rel = (lax.broadcasted_iota(jnp.int32, (K_TILE, tq), 0)
           - lax.broadcasted_iota(jnp.int32, (K_TILE, tq), 1))

    def biased_scores(kt):
        dist = jnp.abs(rel + (kt * K_TILE - q0)).astype(F32) * c
        return _dot_nt(keys(kt), qq_ref[2]) - jnp.concatenate([dist, dist], axis=1)

    def finish(acc):
        lp = lam_ref[...]
        lam_init = lam_init_ref[...]
        lam = (jnp.exp(jnp.sum(lp[0:1] * lp[1:2], axis=1, keepdims=True))
               - jnp.exp(jnp.sum(lp[2:3] * lp[3:4], axis=1, keepdims=True)) + lam_init)
        inv_l = 1.0 / acc[LANES:LANES + 1]
        o = acc[:LANES, :tq] * inv_l[:, :tq] - lam * (acc[:LANES, tq:] * inv_l[:, tq:])
        o = o * lax.rsqrt(jnp.mean(o * o, axis=0, keepdims=True) + EPS) * g_ref[...]
        o = o * (1.0 - lam_init)
        o_ref[...] = o.T.astype(BF16)

    kd = qi
    sd_ref[...] = biased_scores(kd)
    m0 = jnp.max(sd_ref[...], axis=0, keepdims=True)
    q_row = lax.broadcasted_iota(jnp.int32, (1, tq), 1).astype(F32) * c
    q_row = jnp.concatenate([q_row, q_row], axis=1)

    def key_tile(t):
        t = jnp.int32(t - 1)
        kt = t + (t >= kd).astype(jnp.int32)
        return kt, (kt < kd).astype(jnp.int32)

    def qk_fn(t):
        if t == 0:
            return sd_ref[...]
        kt, before = key_tile(t)
        return _dot_nt(keys(kt), qq_ref[before])

    def v_fn(t):
        return values(kd if t == 0 else key_tile(t)[0])

    def ref_fn(t):
        if t == 0:
            return m0
        kt, before = key_tile(t)
        sgn = (2 * before - 1).astype(F32)
        return m0 - ((kt * K_TILE - q0).astype(F32) * c - q_row) * sgn

    lane8 = lax.broadcasted_iota(jnp.int32, kn2_ref.shape, 1)
    k2 = jnp.max(jnp.where(lane8 == hd, kn2_ref[...], 0.0))
    qf = q.astype(F32)
    q2 = jnp.max(jnp.sum(qf * qf, axis=1, keepdims=True))
    bound = jnp.sqrt(jnp.full((1, LANES), q2 * k2, F32)) * NORM_MARGIN
    d = lax.broadcasted_iota(jnp.int32, (1, LANES), 1).astype(F32)
    reach = (d >= 1.0) & (d <= n_kt - 1.0) & (2.0 * bound - c * ((d - 1.0) * K_TILE + 1.0) > -SKIP_EXPONENT)
    radius = jnp.sum(jnp.where(reach, 1.0, 0.0))

    @pl.when(radius > n_kt - 1.5)
    def _():
        acc_ref[...] = _flash_fixed_reference(n_kt, qk_fn, v_fn, ref_fn,
                                              jnp.zeros((ACC_ROWS, 2 * tq), F32), p_refs)

    @pl.when(radius <= n_kt - 1.5)
    def _():
        def weights(kt_raw, before, p_ref):
            kt = jnp.clip(kt_raw, 0, n_kt - 1)
            outside = jnp.where(kt == kt_raw, 0.0, MASKED_REFERENCE)
            ref = m0 - ((kt * K_TILE - q0).astype(F32) * c - q_row) * (2.0 * before - 1.0) + outside
            p_ref[...] = jnp.exp2(_dot_nt(keys(kt), qq_ref[before]) - ref).astype(BF16)
            return kt

        def ring(carry):
            dist, dist_f, acc = carry
            k_before = weights(kd - dist, 1, p_refs[0])
            k_after = weights(kd + dist, 0, p_refs[1])
            acc = (acc + _value_product(values(k_before), p_refs[0][...])
                   + _value_product(values(k_after), p_refs[1][...]))
            return dist + 1, dist_f + 1.0, acc

        acc0 = _value_product(values(kd), jnp.exp2(sd_ref[...] - m0).astype(BF16))
        acc_ref[...] = lax.while_loop(lambda carry: carry[1] < radius + 0.5, ring,
                                      (jnp.int32(1), jnp.float32(1.0), acc0))[2]

    acc = acc_ref[...]
    finish(acc)

    @pl.when(_denominator_overflowed(acc))
    def _():
        finish(_flash_running_max(n_kt, biased_scores, values, 2 * tq))


def _attn_a(c_alibi, qa, ka, vat, kn2, lam_p, lam_init, g_col):
    b, s, _ = qa.shape
    tq = A_Q_TILE
    return pl.pallas_call(
        _attn_a_kernel,
        grid=(b, A_HEADS, s // tq),
        in_specs=[
            pl.BlockSpec(memory_space=pltpu.SMEM),
            pl.BlockSpec((None, tq, LANES), lambda bi, h, i: (bi, i, h)),
            pl.BlockSpec((None, s, 2 * LANES), lambda bi, h, i: (bi, 0, h)),
            pl.BlockSpec((None, V_ROWS, s), lambda bi, h, i: (bi, h, 0)),
            pl.BlockSpec((None,) + kn2.shape[1:], lambda bi, h, i: (bi, 0, 0)),
            _full(lam_p.shape), _full(lam_init.shape), _full(g_col.shape),
        ],
        out_specs=pl.BlockSpec((None, tq, LANES), lambda bi, h, i: (bi, i, h)),
        out_shape=jax.ShapeDtypeStruct((b, s, MIX_WIDTH), BF16),
        scratch_shapes=[pltpu.VMEM((3, 2 * tq, 2 * LANES), BF16), pltpu.VMEM((K_TILE, 2 * tq), F32),
                        pltpu.VMEM((ACC_ROWS, 2 * tq), F32),
                       ] + [pltpu.VMEM((K_TILE, 2 * tq), BF16)] * P_BUFFERS,
        compiler_params=_params(),
        name="attn_a",
    )(c_alibi, qa, ka, vat, kn2, lam_p, lam_init, g_col)


def _attn_c_kernel(q_ref, kn_ref, kpe_ref, vt_ref, o_ref, *p_refs):
    q = q_ref[...]
    tq = q.shape[0]
    tk = C_K_TILE
    n_kt = kn_ref.shape[0] // tk

    def scores(kt):
        k0 = pl.multiple_of(kt * tk, tk)
        k = jnp.concatenate([kn_ref[pl.ds(k0, tk), :], kpe_ref[pl.ds(k0, tk), :]], axis=1)
        return _dot_nt(k, q)

    def values(kt):
        return vt_ref[:, pl.ds(pl.multiple_of(kt * tk, tk), tk)]

    def finish(acc):
        o_ref[...] = (acc[:LANES] * (1.0 / acc[LANES:LANES + 1])).T.astype(BF16)

    probe = jnp.concatenate([kn_ref[0:PROBE_KEYS, :], kpe_ref[0:PROBE_KEYS, :]], axis=1)
    m0 = jnp.max(_dot_nt(probe, q), axis=0, keepdims=True)
    acc = _flash_fixed_reference(n_kt, lambda t: scores(jnp.int32(t)), lambda t: values(jnp.int32(t)),
                                 lambda t: m0, jnp.zeros((ACC_ROWS, tq), F32), p_refs)
    finish(acc)

    @pl.when(_denominator_overflowed(acc))
    def _():
        finish(_flash_running_max(n_kt, scores, values, tq))


def _attn_c(qc, kc, vct):
    b, s, _ = qc.shape
    tq = Q_TILE
    return pl.pallas_call(
        _attn_c_kernel,
        grid=(b, C_HEADS, s // tq),
        in_specs=[
            pl.BlockSpec((None, tq, 2 * LANES), lambda bi, h, i: (bi, i, h)),
            pl.BlockSpec((None, s, LANES), lambda bi, h, i: (bi, 0, h)),
            pl.BlockSpec((None, s, LANES), lambda bi, h, i: (bi, 0, C_HEADS)),
            pl.BlockSpec((None, V_ROWS, s), lambda bi, h, i: (bi, h, 0)),
        ],
        out_specs=pl.BlockSpec((None, tq, LANES), lambda bi, h, i: (bi, i, h)),
        out_shape=jax.ShapeDtypeStruct((b, s, MIX_WIDTH), BF16),
        scratch_shapes=[pltpu.VMEM((C_K_TILE, tq), BF16)] * P_BUFFERS,
        compiler_params=_params(),
        name="attn_c",
    )(qc, kc, kc, vct)


def _attn_b_kernel(sinks_ref, q_ref, kp_ref, ko_ref, kx_ref, vp_ref, vo_ref, vx_ref, bias_ref, o_ref):
    first = pl.program_id(1) * B_STEP_BLOCKS
    nb = pl.num_programs(1) * B_STEP_BLOCKS
    group = B_Q_HEADS // B_KV_HEADS
    half = lax.broadcasted_iota(jnp.int32, (BAND, LANES), 1) // B_HEAD_DIM
    row_half = lax.broadcasted_iota(jnp.int32, (LANES, BAND), 0) // B_HEAD_DIM

    for hk in range(B_KV_HEADS):
        heads = range(hk * group, (hk + 1) * group)
        ksl = slice(hk * LANES, (hk + 1) * LANES)
        k_all = jnp.concatenate([kp_ref[:, ksl], ko_ref[:, ksl], kx_ref[:, ksl]], axis=0)
        vt_all = jnp.concatenate([vp_ref[ksl, :], vo_ref[ksl, :], vx_ref[ksl, :]], axis=1)
        sink = jnp.concatenate([jnp.full((1, BAND), sinks_ref[h] * LOG2E, F32) for h in heads], axis=1)
        for u in range(B_STEP_BLOCKS):
            rows = slice(u * BAND, (u + 1) * BAND)
            qg = []
            for h in heads:
                qh = q_ref[rows, (h // 2) * LANES:(h // 2 + 1) * LANES]
                qg.append(jnp.where(half == h % 2, qh, jnp.zeros_like(qh)))
            bias = bias_ref[hk]
            if u == 0:
                pen = jnp.where(first == 0, NEG_INF, 0.0)
                bias = jnp.concatenate([bias[:BAND] + pen, bias[BAND:]], axis=0)
            if u == B_STEP_BLOCKS - 1:
                pen = jnp.where(first + u == nb - 1, NEG_INF, 0.0)
                bias = jnp.concatenate([bias[:2 * BAND], bias[2 * BAND:] + pen], axis=0)
            sc = _dot_nt(k_all[u * BAND:(u + 3) * BAND], jnp.concatenate(qg, axis=0))
            sc = sc + bias
            m = jnp.maximum(jnp.max(sc, axis=0, keepdims=True), sink)
            e = jnp.exp2(sc - m)
            den = jnp.sum(e, axis=0, keepdims=True) + jnp.exp2(sink - m)
            ot = _dot(vt_all[:, u * BAND:(u + 3) * BAND], e.astype(BF16)) * (1.0 / den)
            for c in range(group // 2):
                pair = jnp.where(row_half == 0, ot[:, 2 * c * BAND:(2 * c + 1) * BAND],
                                 ot[:, (2 * c + 1) * BAND:(2 * c + 2) * BAND])
                col = hk * (group // 2) + c
                o_ref[rows, col * LANES:(col + 1) * LANES] = pair.T.astype(BF16)


def _band_bias_table():
    kj = jnp.arange(3 * BAND)[:, None] - BAND
    dist = jnp.abs(kj - jnp.arange(BAND)[None, :])
    per_head = [jnp.where(dist <= BAND, -(2.0 ** -(h + 1) * LOG2E) * dist.astype(F32), NEG_INF)
                for h in range(B_Q_HEADS)]
    group = B_Q_HEADS // B_KV_HEADS
    return jnp.stack([jnp.concatenate(per_head[g * group:(g + 1) * group], axis=1)
                      for g in range(B_KV_HEADS)])


def _attn_b(sinks, qb, kb, vbt, bias):
    b, s, _ = qb.shape
    nb = s // BAND
    nq = B_STEP_BLOCKS
    rows = nq * BAND
    prev = lambda j: jnp.maximum(j * nq - 1, 0)
    nxt = lambda j: jnp.minimum(j * nq + nq, nb - 1)
    k_edge = lambda f: pl.BlockSpec((None, BAND, 2 * LANES), lambda bi, j: (bi, f(j), 0))
    v_edge = lambda f: pl.BlockSpec((None, 2 * LANES, BAND), lambda bi, j: (bi, 0, f(j)))
    k_own = pl.BlockSpec((None, rows, 2 * LANES), lambda bi, j: (bi, j, 0))
    v_own = pl.BlockSpec((None, 2 * LANES, rows), lambda bi, j: (bi, 0, j))
    return pl.pallas_call(
        _attn_b_kernel,
        grid=(b, nb // nq),
        in_specs=[pl.BlockSpec(memory_space=pltpu.SMEM),
                  pl.BlockSpec((None, rows, MIX_WIDTH), lambda bi, j: (bi, j, 0)),
                  k_edge(prev), k_own, k_edge(nxt), v_edge(prev), v_own, v_edge(nxt),
                  _full(bias.shape)],
        out_specs=pl.BlockSpec((None, rows, MIX_WIDTH), lambda bi, j: (bi, j, 0)),
        out_shape=jax.ShapeDtypeStruct((b, s, MIX_WIDTH), BF16),
        compiler_params=_params(),
        name="attn_b",
    )(sinks, qb, kb, kb, kb, vbt, vbt, vbt, bias)


def _merge_kernel(x_ref, g_ref, oa_ref, ob_ref, oc_ref, wg_ref, bg_ref, wb_ref, wo_ref, o_ref):
    x = x_ref[...]
    d = x.shape[1]
    r = lax.rsqrt(jnp.mean(x * x, axis=-1, keepdims=True) + EPS)
    h = (x * g_ref[...]).astype(BF16)
    merged = jnp.zeros(x.shape, F32)
    for n, br_ref in enumerate((oa_ref, ob_ref, oc_ref)):
        z = _dot(h, wg_ref[:, n * d:(n + 1) * d]) * r + bg_ref[:, n * d:(n + 1) * d]
        gate = 1.0 / (1.0 + jnp.exp(-z))
        merged = merged + gate * _dot(br_ref[...], wb_ref[n])
    o_ref[...] = x + _dot(merged.astype(BF16), wo_ref[...])


def _merge(x, g, oa, ob, oc, wg, bg, wb, wo, l):
    b, s, d = x.shape
    tm = TOKEN_TILE
    row = lambda c: pl.BlockSpec((None, tm, c), lambda bi, i: (bi, i, 0))
    return pl.pallas_call(
        _merge_kernel,
        grid=(b, s // tm),
        in_specs=[row(d), _full(g.shape), row(MIX_WIDTH), row(MIX_WIDTH), row(MIX_WIDTH),
                  _layer(wg, l), _full(bg.shape), _layer(wb, l), _layer(wo, l)],
        out_specs=row(d),
        out_shape=jax.ShapeDtypeStruct(x.shape, F32),
        compiler_params=_params(),
        name="merge",
    )(x, g, oa, ob, oc, wg, bg, wb, wo)


def _prep_w_in(w_in):
    dup = lambda w: jnp.concatenate([w[..., 0:64], w[..., 0:64], w[..., 64:128], w[..., 64:128]], axis=-1)
    kr = w_in[..., 2944:3008]
    return jnp.concatenate([
        w_in[..., 0:2048], dup(w_in[..., 2048:2176]), dup(w_in[..., 2176:2304]), w_in[..., 2304:2944],
        kr, kr[..., 32:64], kr[..., 0:32]], axis=-1).astype(BF16)


def _prep_w_uq(w_uq):
    pe = w_uq[..., C_NOPE:]
    half = C_ROPE // 2
    w = jnp.concatenate([w_uq, pe[..., half:], pe[..., :half]], axis=-1)
    return w.reshape(w_uq.shape[0], w_uq.shape[1], C_HEADS * 2 * LANES).astype(BF16)


def _alibi_key_table(c_alibi):
    term = c_alibi[None, :] * jnp.arange(K_TILE, dtype=F32)[:, None]
    pieces = []
    for _ in range(ALIBI_TERMS):
        piece = term.astype(BF16)
        pieces.append(piece)
        term = term - piece.astype(F32)
    tab = jnp.stack(pieces, axis=-1)
    tab = jnp.pad(tab, ((0, 0), (0, 0), (0, LANES - ALIBI_TERMS)))
    return tab.reshape(K_TILE, -1)


def kernel(x, ffn1_norm, ffn1_w13, ffn1_w2, mix_norm, w_in, w_gate, b_gate, a_lambda, a_subln,
           b_sinks, c_q_norm, c_w_uq, c_kv_norm, c_w_ukv, w_branch, w_out, ffn2_norm, ffn2_w13,
           ffn2_w2, final_norm):
    b, s, d = x.shape
    depth = w_in.shape[0]
    assert s % K_TILE == 0 and s % C_K_TILE == 0 and K_TILE % TOKEN_TILE == 0 and d % LANES == 0

    pos = jnp.arange(s, dtype=F32)
    inv_freq = ROPE_THETA ** (-jnp.arange(0, C_ROPE, 2, dtype=F32) / C_ROPE)
    ang = pos[:, None] * inv_freq[None, :]
    cos, sin = jnp.cos(ang), jnp.sin(ang)
    pad = jnp.zeros((s, LANES - C_ROPE), F32)
    cos2 = jnp.concatenate([cos, cos, pad], axis=1)
    sin2 = jnp.concatenate([-sin, sin, pad], axis=1)
    c_alibi = 2.0 ** (-8.0 * jnp.arange(1, A_HEADS + 1, dtype=F32) / A_HEADS) * LOG2E
    alibi = _alibi_key_table(c_alibi)
    band_bias = _band_bias_table()
    row = lambda v: v.reshape(1, -1)
    gf = row(final_norm)

    bf = lambda w: w.astype(BF16)
    w13_1, w2_1, w13_2, w2_2 = bf(ffn1_w13), bf(ffn1_w2), bf(ffn2_w13), bf(ffn2_w2)
    w_in_x, w_uq_x = _prep_w_in(w_in), _prep_w_uq(c_w_uq)
    w_kn = bf(c_w_ukv[..., :C_NOPE].reshape(depth, C_KV_RANK, -1))
    w_v = bf(c_w_ukv[..., C_NOPE:].reshape(depth, C_KV_RANK, -1))
    w_g, w_b, w_o = bf(w_gate), bf(w_branch), bf(w_out)

    for l in range(depth):
        x = _ffn(x, row(ffn1_norm[l]), w13_1, w2_1, gf, l, False)

        qa, ka, vat, qb, kb, vb, qc, kc, vct, kn2 = _mix_in(
            x, row(mix_norm[l]), w_in_x, row(c_q_norm[l]), w_uq_x, row(c_kv_norm[l]), w_kn, w_v,
            cos2, sin2, alibi, l)

        lam_init = jnp.full((1, 1), 0.8 - 0.6 * math.exp(-0.3 * l), F32)
        oa = _attn_a(c_alibi, qa, ka, vat, kn2, a_lambda[l], lam_init, a_subln[l].reshape(-1, 1))
        ob = _attn_b(b_sinks[l], qb, kb, vb, band_bias)
        oc = _attn_c(qc, kc, vct)

        x = _merge(x, row(mix_norm[l]), oa, ob, oc, w_g, row(b_gate[l]), w_b, w_o, l)
        x = _ffn(x, row(ffn2_norm[l]), w13_2, w2_2, gf, l, l == depth - 1)
    return x
```

```python
import functools
import math

import jax
import jax.numpy as jnp
from jax import lax
from jax.experimental import pallas as pl
from jax.experimental.pallas import tpu as pltpu

BF16 = jnp.bfloat16
F32 = jnp.float32

EPS = 1e-6
NEG_INF = -1e30
LOG2E = math.log2(math.e)
ROPE_THETA = 10000.0

A_HEADS = 4
A_QK_DIM = 64
B_Q_HEADS = 8
B_KV_HEADS = 2
B_HEAD_DIM = 64
BAND = 128
B_STEP_BLOCKS = 4
C_HEADS = 4
C_Q_RANK = 384
C_KV_RANK = 256
C_NOPE = 128
C_ROPE = 64
MIX_WIDTH = 512
LANES = 128

VMEM_LIMIT = 52 * 1024 * 1024

TOKEN_TILE = 512
MXU_DIM = 256
FFN_CHUNK = 6 * MXU_DIM
K_TILE = 512
C_K_TILE = 1024
Q_TILE = 2048
A_Q_TILE = K_TILE
V_ROWS = LANES + 16
ACC_ROWS = LANES + 8
ALIBI_TERMS = 3


def _params():
    return pltpu.CompilerParams(vmem_limit_bytes=VMEM_LIMIT)


def _rms(x, g):
    return x * lax.rsqrt(jnp.mean(x * x, axis=-1, keepdims=True) + EPS) * g


def _dot(a, b):
    return jnp.dot(a, b, preferred_element_type=F32)


def _dot_nt(a, b):
    return lax.dot_general(a, b, (((1,), (1,)), ((), ())), preferred_element_type=F32)


def _full(shape):
    return pl.BlockSpec(shape, lambda *_: (0,) * len(shape))


def _layer(stack, l):
    return pl.BlockSpec((None,) + stack.shape[1:], lambda *_: (l,) + (0,) * (stack.ndim - 1),
                        pipeline_mode=pl.Buffered(1))


def _ffn_kernel(x_ref, g_ref, w13_ref, w2_ref, gf_ref, o_ref, *, final):
    x = x_ref[...]
    r = lax.rsqrt(jnp.mean(x * x, axis=-1, keepdims=True) + EPS)
    h = (x * g_ref[...]).astype(BF16)
    d_ff = w2_ref.shape[0]
    acc = jnp.zeros(x.shape, F32)
    for lo in range(0, d_ff, FFN_CHUNK):
        hi = min(lo + FFN_CHUNK, d_ff)
        a = _dot(h, w13_ref[:, lo:hi]) * r
        g = _dot(h, w13_ref[:, d_ff + lo:d_ff + hi]) * r
        act = (a * (1.0 / (1.0 + jnp.exp(-a))) * g).astype(BF16)
        acc = acc + _dot(act, w2_ref[lo:hi, :])
    y = x + 0.5 * acc
    if final:
        y = _rms(y, gf_ref[...])
    o_ref[...] = y


def _ffn(x, g, w13, w2, gf, l, final):
    b, s, d = x.shape
    tm = TOKEN_TILE
    xspec = pl.BlockSpec((None, tm, d), lambda bi, i: (bi, i, 0))
    return pl.pallas_call(
        functools.partial(_ffn_kernel, final=final),
        grid=(b, s // tm),
        in_specs=[xspec, _full(g.shape), _layer(w13, l), _layer(w2, l), _full(gf.shape)],
        out_specs=xspec,
        out_shape=jax.ShapeDtypeStruct(x.shape, F32),
        compiler_params=_params(),
        name="ffn",
    )(x, g, w13, w2, gf)


def _store_values_t(vt_ref, v, heads):
    vt = v.T.astype(BF16)
    ones = jnp.ones((V_ROWS - LANES, vt.shape[1]), BF16)
    for hd in range(heads):
        vt_ref[hd * V_ROWS:hd * V_ROWS + LANES, :] = vt[hd * LANES:(hd + 1) * LANES]
        vt_ref[hd * V_ROWS + LANES:(hd + 1) * V_ROWS, :] = ones


def _mix_in_kernel(x_ref, g_ref, w_ref, cqg_ref, wq_ref, ckvg_ref, wkn_ref, wv_ref,
                   cos_ref, sin_ref, alibi_ref,
                   qa_ref, ka_ref, vat_ref, qb_ref, kb_ref, vbt_ref, qc_ref, kc_ref, vct_ref, kn2_ref):
    x = x_ref[...]
    r = lax.rsqrt(jnp.mean(x * x, axis=-1, keepdims=True) + EPS)
    h = (x * g_ref[...]).astype(BF16)
    cos2 = cos_ref[...]
    sin2 = sin_ref[...]

    zc = _dot(h, w_ref[:, 2560:3328]) * r
    cqn = _rms(zc[:, 0:C_Q_RANK], cqg_ref[...]).astype(BF16)
    ckvn = _rms(zc[:, C_Q_RANK:C_Q_RANK + C_KV_RANK], ckvg_ref[...]).astype(BF16)
    kr = zc[:, C_Q_RANK + C_KV_RANK:]
    kc_ref[:, 512:640] = (kr * cos2 + pltpu.roll(kr, C_ROPE, 1) * sin2).astype(BF16)

    za = _dot(h, w_ref[:, 0:1536]) * r
    qa_ref[...] = (za[:, 0:512] * (A_QK_DIM ** -0.5 * LOG2E)).astype(BF16)
    for hd in range(A_HEADS):
        ka_ref[:, 2 * hd * LANES:(2 * hd + 1) * LANES] = (
            za[:, 512 + hd * LANES:512 + (hd + 1) * LANES].astype(BF16))
        ka_ref[:, (2 * hd + 1) * LANES:(2 * hd + 2) * LANES] = alibi_ref[:, hd * LANES:(hd + 1) * LANES]
    _store_values_t(vat_ref, za[:, 1024:1536], A_HEADS)
    kb16 = za[:, 512:1024].astype(BF16)
    kk = kb16 * kb16
    map_of_col = lax.broadcasted_iota(jnp.int32, (A_HEADS * LANES, LANES), 0) // A_QK_DIM
    ind = (map_of_col == lax.broadcasted_iota(jnp.int32, (A_HEADS * LANES, LANES), 1)).astype(BF16)
    kn2 = jnp.broadcast_to(jnp.max(_dot(kk, ind), axis=0, keepdims=True), kn2_ref.shape)

    @pl.when(pl.program_id(1) == 0)
    def _():
        kn2_ref[...] = kn2

    @pl.when(pl.program_id(1) > 0)
    def _():
        kn2_ref[...] = jnp.maximum(kn2_ref[...], kn2)

    c_scale = (C_NOPE + C_ROPE) ** -0.5 * LOG2E
    q = _dot(cqn, wq_ref[...])
    for hd in range(C_HEADS):
        lo = hd * 2 * LANES
        qc_ref[:, lo:lo + LANES] = (q[:, lo:lo + LANES] * c_scale).astype(BF16)
        qp = q[:, lo + LANES:lo + 2 * LANES]
        qpr = qp * cos2 + pltpu.roll(qp, C_ROPE, 1) * sin2
        qc_ref[:, lo + LANES:lo + 2 * LANES] = (qpr * c_scale).astype(BF16)
    kc_ref[:, 0:512] = _dot(ckvn, wkn_ref[...]).astype(BF16)
    _store_values_t(vct_ref, _dot(ckvn, wv_ref[...]), C_HEADS)

    zb = _dot(h, w_ref[:, 1536:2560]) * r
    qb_ref[...] = (zb[:, 0:512] * (B_HEAD_DIM ** -0.5 * LOG2E)).astype(BF16)
    kb_ref[...] = zb[:, 512:768].astype(BF16)
    vbt_ref[...] = zb[:, 768:1024].T.astype(BF16)


def _mix_in(x, g, w, cqg, wq, ckvg, wkn, wv, cos2, sin2, alibi, l):
    b, s, d = x.shape
    tm = TOKEN_TILE
    row = lambda c: pl.BlockSpec((None, tm, c), lambda bi, i: (bi, i, 0))
    colT = pl.BlockSpec((None, 4 * V_ROWS, tm), lambda bi, i: (bi, 0, i))
    tab = pl.BlockSpec((tm, LANES), lambda bi, i: (i, 0))
    sd = lambda c: jax.ShapeDtypeStruct((b, s, c), BF16)
    sdT = jax.ShapeDtypeStruct((b, 4 * V_ROWS, s), BF16)
    return pl.pallas_call(
        _mix_in_kernel,
        grid=(b, s // tm),
        in_specs=[row(d), _full(g.shape), _layer(w, l), _full(cqg.shape), _layer(wq, l),
                  _full(ckvg.shape), _layer(wkn, l), _layer(wv, l), tab, tab,
                  pl.BlockSpec((tm, alibi.shape[1]), lambda bi, i: (i % (K_TILE // tm), 0))],
        out_specs=[row(512), row(1024), colT, row(512), row(256),
                   pl.BlockSpec((None, 2 * LANES, tm), lambda bi, i: (bi, 0, i)),
                   row(1024), row(640), colT,
                   pl.BlockSpec((None, 8, LANES), lambda bi, i: (bi, 0, 0))],
        out_shape=[sd(512), sd(1024), sdT, sd(512), sd(256),
                   jax.ShapeDtypeStruct((b, 2 * LANES, s), BF16), sd(1024), sd(640), sdT,
                   jax.ShapeDtypeStruct((b, 8, LANES), F32)],
        compiler_params=_params(),
        name="mix_in",
    )(x, g, w, cqg, wq, ckvg, wkn, wv, cos2, sin2, alibi)


OVERFLOW_GUARD = 2.0 ** 60
PROBE_KEYS = 128
SKIP_EXPONENT = 140.0
NORM_MARGIN = 1.02
MASKED_REFERENCE = 1e30
P_BUFFERS = 2


def _value_product(vt, p):
    return _dot(vt, p)[:ACC_ROWS]


def _tile_stats(st, vt):
    m_t = jnp.max(st, axis=0, keepdims=True)
    return m_t, _value_product(vt, jnp.exp2(st - m_t).astype(BF16))


def _flash_fixed_reference(n, qk_fn, v_fn, ref_fn, acc, p_refs):
    def weights(t):
        p_refs[t % len(p_refs)][...] = jnp.exp2(qk_fn(t) - ref_fn(t)).astype(BF16)

    if n:
        weights(0)
    for t in range(n):
        if t + 1 < n:
            weights(t + 1)
        acc = acc + _value_product(v_fn(t), p_refs[t % len(p_refs)][...])
    return acc


def _flash_running_max(n, score_fn, v_fn, width):
    def body(kt, carry):
        m, acc = carry
        m_t, pv_t = _tile_stats(score_fn(kt), v_fn(kt))
        m_new = jnp.maximum(m, m_t)
        return m_new, jnp.exp2(m - m_new) * acc + jnp.exp2(m_t - m_new) * pv_t

    init = (jnp.full((1, width), -jnp.inf, F32), jnp.zeros((ACC_ROWS, width), F32))
    return lax.fori_loop(0, n, body, init)[1]


def _denominator_overflowed(acc):
    l = acc[LANES:LANES + 1]
    return jnp.max(jnp.where(l < OVERFLOW_GUARD, 0.0, 1.0)) > 0.5


def _attn_a_kernel(c_ref, q_ref, k_ref, vt_ref, kn2_ref, lam_ref, lam_init_ref, g_ref, o_ref,
                   qq_ref, sd_ref, acc_ref, *p_refs):
    hd = pl.program_id(1)
    qi = pl.program_id(2)
    tq = q_ref.shape[0]
    assert tq == K_TILE
    n_kt = k_ref.shape[0] // K_TILE
    c = c_ref[hd]

    q = q_ref[...]
    lane = lax.broadcasted_iota(jnp.int32, q.shape, 1)
    zero = jnp.zeros_like(q)
    qm = jnp.concatenate([jnp.where(lane < A_QK_DIM, q, zero),
                          jnp.where(lane >= A_QK_DIM, q, zero)], axis=0)
    sel = (lax.broadcasted_iota(jnp.int32, qm.shape, 1) < ALIBI_TERMS).astype(BF16)
    for slot, sign in enumerate((-1.0, 1.0, 0.0)):
        qq_ref[slot, :, 0:LANES] = qm
        qq_ref[slot, :, LANES:2 * LANES] = sel * sign

    def keys(kt):
        return k_ref[pl.ds(pl.multiple_of(kt * K_TILE, K_TILE), K_TILE), :]

    def values(kt):
        return vt_ref[:, pl.ds(pl.multiple_of(kt * K_TILE, K_TILE), K_TILE)]

    q0 = qi * tq
    rel = (lax.broadcasted_iota(jnp.int32, (K_TILE, tq), 0)
           - lax.broadcasted_iota(jnp.int32, (K_TILE, tq), 1))

    def biased_scores(kt):
        dist = jnp.abs(rel + (kt * K_TILE - q0)).astype(F32) * c
        return _dot_nt(keys(kt), qq_ref[2]) - jnp.concatenate([dist, dist], axis=1)

    def finish(acc):
        lp = lam_ref[...]
        lam_init = lam_init_ref[...]
        lam = (jnp.exp(jnp.sum(lp[0:1] * lp[1:2], axis=1, keepdims=True))
               - jnp.exp(jnp.sum(lp[2:3] * lp[3:4], axis=1, keepdims=True)) + lam_init)
        inv_l = 1.0 / acc[LANES:LANES + 1]
        o = acc[:LANES, :tq] * inv_l[:, :tq] - lam * (acc[:LANES, tq:] * inv_l[:, tq:])
        o = o * lax.rsqrt(jnp.mean(o * o, axis=0, keepdims=True) + EPS) * g_ref[...]
        o = o * (1.0 - lam_init)
        o_ref[...] = o.T.astype(BF16)

    kd = qi
    sd_ref[...] = biased_scores(kd)
    m0 = jnp.max(sd_ref[...], axis=0, keepdims=True)
    q_row = lax.broadcasted_iota(jnp.int32, (1, tq), 1).astype(F32) * c
    q_row = jnp.concatenate([q_row, q_row], axis=1)

    def key_tile(t):
        t = jnp.int32(t - 1)
        kt = t + (t >= kd).astype(jnp.int32)
        return kt, (kt < kd).astype(jnp.int32)

    def qk_fn(t):
        if t == 0:
            return sd_ref[...]
        kt, before = key_tile(t)
        return _dot_nt(keys(kt), qq_ref[before])

    def v_fn(t):
        return values(kd if t == 0 else key_tile(t)[0])

    def ref_fn(t):
        if t == 0:
            return m0
        kt, before = key_tile(t)
        sgn = (2 * before - 1).astype(F32)
        return m0 - ((kt * K_TILE - q0).astype(F32) * c - q_row) * sgn

    lane8 = lax.broadcasted_iota(jnp.int32, kn2_ref.shape, 1)
    qsq = q.astype(F32) * q.astype(F32)
    excess = []
    for m, in_map in enumerate((lane < A_QK_DIM, lane >= A_QK_DIM)):
        k2 = jnp.max(jnp.where(lane8 == 2 * hd + m, kn2_ref[...], 0.0))
        q2 = jnp.max(jnp.sum(jnp.where(in_map, qsq, 0.0), axis=1, keepdims=True))
        score_max = jnp.sqrt(jnp.full((1, LANES), q2 * k2, F32)) * NORM_MARGIN
        excess.append(score_max - jnp.min(m0[:, m * tq:(m + 1) * tq]))
    d = lax.broadcasted_iota(jnp.int32, (1, LANES), 1).astype(F32)
    reach = ((d >= 1.0) & (d <= n_kt - 1.0)
             & (jnp.maximum(*excess) - c * ((d - 1.0) * K_TILE + 1.0) > -SKIP_EXPONENT))
    radius = jnp.sum(jnp.where(reach, 1.0, 0.0))
    max_ring = (n_kt - 1) // 2 - 1

    @pl.when(radius > max_ring + 0.5)
    def _():
        acc_ref[...] = _flash_fixed_reference(n_kt, qk_fn, v_fn, ref_fn,
                                              jnp.zeros((ACC_ROWS, 2 * tq), F32), p_refs)

    @pl.when(radius <= max_ring + 0.5)
    def _():
        def weights(kt_raw, before, p_ref):
            kt = jnp.clip(kt_raw, 0, n_kt - 1)
            outside = jnp.where(kt == kt_raw, 0.0, MASKED_REFERENCE)
            ref = m0 - ((kt * K_TILE - q0).astype(F32) * c - q_row) * (2.0 * before - 1.0) + outside
            p_ref[...] = jnp.exp2(_dot_nt(keys(kt), qq_ref[before]) - ref).astype(BF16)
            return kt

        def ring(carry):
            dist, dist_f, acc = carry
            k_before = weights(kd - dist, 1, p_refs[0])
            k_after = weights(kd + dist, 0, p_refs[1])
            acc = (acc + _value_product(values(k_before), p_refs[0][...])
                   + _value_product(values(k_after), p_refs[1][...]))
            return dist + 1, dist_f + 1.0, acc

        acc0 = _value_product(values(kd), jnp.exp2(sd_ref[...] - m0).astype(BF16))
        acc_ref[...] = lax.while_loop(lambda carry: carry[1] < radius + 0.5, ring,
                                      (jnp.int32(1), jnp.float32(1.0), acc0))[2]

    acc = acc_ref[...]
    finish(acc)

    @pl.when(_denominator_overflowed(acc))
    def _():
        finish(_flash_running_max(n_kt, biased_scores, values, 2 * tq))


def _attn_a(c_alibi, qa, ka, vat, kn2, lam_p, lam_init, g_col):
    b, s, _ = qa.shape
    tq = A_Q_TILE
    return pl.pallas_call(
        _attn_a_kernel,
        grid=(b, A_HEADS, s // tq),
        in_specs=[
            pl.BlockSpec(memory_space=pltpu.SMEM),
            pl.BlockSpec((None, tq, LANES), lambda bi, h, i: (bi, i, h)),
            pl.BlockSpec((None, s, 2 * LANES), lambda bi, h, i: (bi, 0, h)),
            pl.BlockSpec((None, V_ROWS, s), lambda bi, h, i: (bi, h, 0)),
            pl.BlockSpec((None,) + kn2.shape[1:], lambda bi, h, i: (bi, 0, 0)),
            _full(lam_p.shape), _full(lam_init.shape), _full(g_col.shape),
        ],
        out_specs=pl.BlockSpec((None, tq, LANES), lambda bi, h, i: (bi, i, h)),
        out_shape=jax.ShapeDtypeStruct((b, s, MIX_WIDTH), BF16),
        scratch_shapes=[pltpu.VMEM((3, 2 * tq, 2 * LANES), BF16), pltpu.VMEM((K_TILE, 2 * tq), F32),
                        pltpu.VMEM((ACC_ROWS, 2 * tq), F32),
                       ] + [pltpu.VMEM((K_TILE, 2 * tq), BF16)] * P_BUFFERS,
        compiler_params=_params(),
        name="attn_a",
    )(c_alibi, qa, ka, vat, kn2, lam_p, lam_init, g_col)


def _attn_c_kernel(q_ref, kn_ref, kpe_ref, vt_ref, o_ref, *p_refs):
    q = q_ref[...]
    tq = q.shape[0]
    tk = C_K_TILE
    n_kt = kn_ref.shape[0] // tk

    def scores(kt):
        k0 = pl.multiple_of(kt * tk, tk)
        k = jnp.concatenate([kn_ref[pl.ds(k0, tk), :], kpe_ref[pl.ds(k0, tk), :]], axis=1)
        return _dot_nt(k, q)

    def values(kt):
        return vt_ref[:, pl.ds(pl.multiple_of(kt * tk, tk), tk)]

    def finish(acc):
        o_ref[...] = (acc[:LANES] * (1.0 / acc[LANES:LANES + 1])).T.astype(BF16)

    probe = jnp.concatenate([kn_ref[0:PROBE_KEYS, :], kpe_ref[0:PROBE_KEYS, :]], axis=1)
    m0 = jnp.max(_dot_nt(probe, q), axis=0, keepdims=True)
    acc = _flash_fixed_reference(n_kt, lambda t: scores(jnp.int32(t)), lambda t: values(jnp.int32(t)),
                                 lambda t: m0, jnp.zeros((ACC_ROWS, tq), F32), p_refs)
    finish(acc)

    @pl.when(_denominator_overflowed(acc))
    def _():
        finish(_flash_running_max(n_kt, scores, values, tq))


def _attn_c(qc, kc, vct):
    b, s, _ = qc.shape
    tq = Q_TILE
    return pl.pallas_call(
        _attn_c_kernel,
        grid=(b, C_HEADS, s // tq),
        in_specs=[
            pl.BlockSpec((None, tq, 2 * LANES), lambda bi, h, i: (bi, i, h)),
            pl.BlockSpec((None, s, LANES), lambda bi, h, i: (bi, 0, h)),
            pl.BlockSpec((None, s, LANES), lambda bi, h, i: (bi, 0, C_HEADS)),
            pl.BlockSpec((None, V_ROWS, s), lambda bi, h, i: (bi, h, 0)),
        ],
        out_specs=pl.BlockSpec((None, tq, LANES), lambda bi, h, i: (bi, i, h)),
        out_shape=jax.ShapeDtypeStruct((b, s, MIX_WIDTH), BF16),
        scratch_shapes=[pltpu.VMEM((C_K_TILE, tq), BF16)] * P_BUFFERS,
        compiler_params=_params(),
        name="attn_c",
    )(qc, kc, kc, vct)


def _attn_b_kernel(sinks_ref, q_ref, kp_ref, ko_ref, kx_ref, vp_ref, vo_ref, vx_ref, bias_ref, o_ref):
    first = pl.program_id(1) * B_STEP_BLOCKS
    nb = pl.num_programs(1) * B_STEP_BLOCKS
    group = B_Q_HEADS // B_KV_HEADS
    half = lax.broadcasted_iota(jnp.int32, (BAND, LANES), 1) // B_HEAD_DIM
    row_half = lax.broadcasted_iota(jnp.int32, (LANES, BAND), 0) // B_HEAD_DIM

    for hk in range(B_KV_HEADS):
        heads = range(hk * group, (hk + 1) * group)
        ksl = slice(hk * LANES, (hk + 1) * LANES)
        k_all = jnp.concatenate([kp_ref[:, ksl], ko_ref[:, ksl], kx_ref[:, ksl]], axis=0)
        vt_all = jnp.concatenate([vp_ref[ksl, :], vo_ref[ksl, :], vx_ref[ksl, :]], axis=1)
        sink = jnp.concatenate([jnp.full((1, BAND), sinks_ref[h] * LOG2E, F32) for h in heads], axis=1)
        for u in range(B_STEP_BLOCKS):
            rows = slice(u * BAND, (u + 1) * BAND)
            qg = []
            for h in heads:
                qh = q_ref[rows, (h // 2) * LANES:(h // 2 + 1) * LANES]
                qg.append(jnp.where(half == h % 2, qh, jnp.zeros_like(qh)))
            bias = bias_ref[hk]
            if u == 0:
                pen = jnp.where(first == 0, NEG_INF, 0.0)
                bias = jnp.concatenate([bias[:BAND] + pen, bias[BAND:]], axis=0)
            if u == B_STEP_BLOCKS - 1:
                pen = jnp.where(first + u == nb - 1, NEG_INF, 0.0)
                bias = jnp.concatenate([bias[:2 * BAND], bias[2 * BAND:] + pen], axis=0)
            sc = _dot_nt(k_all[u * BAND:(u + 3) * BAND], jnp.concatenate(qg, axis=0))
            sc = sc + bias
            m = jnp.maximum(jnp.max(sc, axis=0, keepdims=True), sink)
            e = jnp.exp2(sc - m)
            den = jnp.sum(e, axis=0, keepdims=True) + jnp.exp2(sink - m)
            ot = _dot(vt_all[:, u * BAND:(u + 3) * BAND], e.astype(BF16)) * (1.0 / den)
            for c in range(group // 2):
                pair = jnp.where(row_half == 0, ot[:, 2 * c * BAND:(2 * c + 1) * BAND],
                                 ot[:, (2 * c + 1) * BAND:(2 * c + 2) * BAND])
                col = hk * (group // 2) + c
                o_ref[rows, col * LANES:(col + 1) * LANES] = pair.T.astype(BF16)


def _band_bias_table():
    kj = jnp.arange(3 * BAND)[:, None] - BAND
    dist = jnp.abs(kj - jnp.arange(BAND)[None, :])
    per_head = [jnp.where(dist <= BAND, -(2.0 ** -(h + 1) * LOG2E) * dist.astype(F32), NEG_INF)
                for h in range(B_Q_HEADS)]
    group = B_Q_HEADS // B_KV_HEADS
    return jnp.stack([jnp.concatenate(per_head[g * group:(g + 1) * group], axis=1)
                      for g in range(B_KV_HEADS)])


def _attn_b(sinks, qb, kb, vbt, bias):
    b, s, _ = qb.shape
    nb = s // BAND
    nq = B_STEP_BLOCKS
    rows = nq * BAND
    prev = lambda j: jnp.maximum(j * nq - 1, 0)
    nxt = lambda j: jnp.minimum(j * nq + nq, nb - 1)
    k_edge = lambda f: pl.BlockSpec((None, BAND, 2 * LANES), lambda bi, j: (bi, f(j), 0))
    v_edge = lambda f: pl.BlockSpec((None, 2 * LANES, BAND), lambda bi, j: (bi, 0, f(j)))
    k_own = pl.BlockSpec((None, rows, 2 * LANES), lambda bi, j: (bi, j, 0))
    v_own = pl.BlockSpec((None, 2 * LANES, rows), lambda bi, j: (bi, 0, j))
    return pl.pallas_call(
        _attn_b_kernel,
        grid=(b, nb // nq),
        in_specs=[pl.BlockSpec(memory_space=pltpu.SMEM),
                  pl.BlockSpec((None, rows, MIX_WIDTH), lambda bi, j: (bi, j, 0)),
                  k_edge(prev), k_own, k_edge(nxt), v_edge(prev), v_own, v_edge(nxt),
                  _full(bias.shape)],
        out_specs=pl.BlockSpec((None, rows, MIX_WIDTH), lambda bi, j: (bi, j, 0)),
        out_shape=jax.ShapeDtypeStruct((b, s, MIX_WIDTH), BF16),
        compiler_params=_params(),
        name="attn_b",
    )(sinks, qb, kb, kb, kb, vbt, vbt, vbt, bias)


def _merge_kernel(x_ref, g_ref, oa_ref, ob_ref, oc_ref, wg_ref, bg_ref, wb_ref, wo_ref, o_ref):
    x = x_ref[...]
    d = x.shape[1]
    r = lax.rsqrt(jnp.mean(x * x, axis=-1, keepdims=True) + EPS)
    h = (x * g_ref[...]).astype(BF16)
    merged = jnp.zeros(x.shape, F32)
    for n, br_ref in enumerate((oa_ref, ob_ref, oc_ref)):
        z = _dot(h, wg_ref[:, n * d:(n + 1) * d]) * r + bg_ref[:, n * d:(n + 1) * d]
        gate = 1.0 / (1.0 + jnp.exp(-z))
        merged = merged + gate * _dot(br_ref[...], wb_ref[n])
    o_ref[...] = x + _dot(merged.astype(BF16), wo_ref[...])


def _merge(x, g, oa, ob, oc, wg, bg, wb, wo, l):
    b, s, d = x.shape
    tm = TOKEN_TILE
    row = lambda c: pl.BlockSpec((None, tm, c), lambda bi, i: (bi, i, 0))
    return pl.pallas_call(
        _merge_kernel,
        grid=(b, s // tm),
        in_specs=[row(d), _full(g.shape), row(MIX_WIDTH), row(MIX_WIDTH), row(MIX_WIDTH),
                  _layer(wg, l), _full(bg.shape), _layer(wb, l), _layer(wo, l)],
        out_specs=row(d),
        out_shape=jax.ShapeDtypeStruct(x.shape, F32),
        compiler_params=_params(),
        name="merge",
    )(x, g, oa, ob, oc, wg, bg, wb, wo)


def _prep_w_in(w_in):
    dup = lambda w: jnp.concatenate([w[..., 0:64], w[..., 0:64], w[..., 64:128], w[..., 64:128]], axis=-1)
    kr = w_in[..., 2944:3008]
    return jnp.concatenate([
        w_in[..., 0:2048], dup(w_in[..., 2048:2176]), dup(w_in[..., 2176:2304]), w_in[..., 2304:2944],
        kr, kr[..., 32:64], kr[..., 0:32]], axis=-1).astype(BF16)


def _prep_w_uq(w_uq):
    pe = w_uq[..., C_NOPE:]
    half = C_ROPE // 2
    w = jnp.concatenate([w_uq, pe[..., half:], pe[..., :half]], axis=-1)
    return w.reshape(w_uq.shape[0], w_uq.shape[1], C_HEADS * 2 * LANES).astype(BF16)


def _alibi_key_table(c_alibi):
    term = c_alibi[None, :] * jnp.arange(K_TILE, dtype=F32)[:, None]
    pieces = []
    for _ in range(ALIBI_TERMS):
        piece = term.astype(BF16)
        pieces.append(piece)
        term = term - piece.astype(F32)
    tab = jnp.stack(pieces, axis=-1)
    tab = jnp.pad(tab, ((0, 0), (0, 0), (0, LANES - ALIBI_TERMS)))
    return tab.reshape(K_TILE, -1)


def kernel(x, ffn1_norm, ffn1_w13, ffn1_w2, mix_norm, w_in, w_gate, b_gate, a_lambda, a_subln,
           b_sinks, c_q_norm, c_w_uq, c_kv_norm, c_w_ukv, w_branch, w_out, ffn2_norm, ffn2_w13,
           ffn2_w2, final_norm):
    b, s, d = x.shape
    depth = w_in.shape[0]
    assert s % K_TILE == 0 and s % C_K_TILE == 0 and K_TILE % TOKEN_TILE == 0 and d % LANES == 0

    pos = jnp.arange(s, dtype=F32)
    inv_freq = ROPE_THETA ** (-jnp.arange(0, C_ROPE, 2, dtype=F32) / C_ROPE)
    ang = pos[:, None] * inv_freq[None, :]
    cos, sin = jnp.cos(ang), jnp.sin(ang)
    pad = jnp.zeros((s, LANES - C_ROPE), F32)
    cos2 = jnp.concatenate([cos, cos, pad], axis=1)
    sin2 = jnp.concatenate([-sin, sin, pad], axis=1)
    c_alibi = 2.0 ** (-8.0 * jnp.arange(1, A_HEADS + 1, dtype=F32) / A_HEADS) * LOG2E
    alibi = _alibi_key_table(c_alibi)
    band_bias = _band_bias_table()
    row = lambda v: v.reshape(1, -1)
    gf = row(final_norm)

    bf = lambda w: w.astype(BF16)
    w13_1, w2_1, w13_2, w2_2 = bf(ffn1_w13), bf(ffn1_w2), bf(ffn2_w13), bf(ffn2_w2)
    w_in_x, w_uq_x = _prep_w_in(w_in), _prep_w_uq(c_w_uq)
    w_kn = bf(c_w_ukv[..., :C_NOPE].reshape(depth, C_KV_RANK, -1))
    w_v = bf(c_w_ukv[..., C_NOPE:].reshape(depth, C_KV_RANK, -1))
    w_g, w_b, w_o = bf(w_gate), bf(w_branch), bf(w_out)

    for l in range(depth):
        x = _ffn(x, row(ffn1_norm[l]), w13_1, w2_1, gf, l, False)

        qa, ka, vat, qb, kb, vb, qc, kc, vct, kn2 = _mix_in(
            x, row(mix_norm[l]), w_in_x, row(c_q_norm[l]), w_uq_x, row(c_kv_norm[l]), w_kn, w_v,
            cos2, sin2, alibi, l)

        lam_init = jnp.full((1, 1), 0.8 - 0.6 * math.exp(-0.3 * l), F32)
        oa = _attn_a(c_alibi, qa, ka, vat, kn2, a_lambda[l], lam_init, a_subln[l].reshape(-1, 1))
        ob = _attn_b(b_sinks[l], qb, kb, vb, band_bias)
        oc = _attn_c(qc, kc, vct)

        x = _merge(x, row(mix_norm[l]), oa, ob, oc, w_g, row(b_gate[l]), w_b, w_o, l)
        x = _ffn(x, row(ffn2_norm[l]), w13_2, w2_2, gf, l, l == depth - 1)
    return x
```

```python
import functools
import math

import jax
import jax.numpy as jnp
from jax import lax
from jax.experimental import pallas as pl
from jax.experimental.pallas import tpu as pltpu

BF16 = jnp.bfloat16
F32 = jnp.float32

EPS = 1e-6
NEG_INF = -1e30
LOG2E = math.log2(math.e)
ROPE_THETA = 10000.0

A_HEADS = 4
A_QK_DIM = 64
B_Q_HEADS = 8
B_KV_HEADS = 2
B_HEAD_DIM = 64
BAND = 128
B_STEP_BLOCKS = 4
C_HEADS = 4
C_Q_RANK = 384
C_KV_RANK = 256
C_NOPE = 128
C_ROPE = 64
MIX_WIDTH = 512
LANES = 128

VMEM_LIMIT = 52 * 1024 * 1024

TOKEN_TILE = 512
MXU_DIM = 256
FFN_CHUNK = 6 * MXU_DIM
K_TILE = 512
C_K_TILE = 1024
Q_TILE = 2048
A_Q_TILE = K_TILE
V_ROWS = LANES + 16
ACC_ROWS = LANES + 8
ALIBI_TERMS = 3


def _params():
    return pltpu.CompilerParams(vmem_limit_bytes=VMEM_LIMIT)


def _rms(x, g):
    return x * lax.rsqrt(jnp.mean(x * x, axis=-1, keepdims=True) + EPS) * g


def _dot(a, b):
    return jnp.dot(a, b, preferred_element_type=F32)


def _dot_nt(a, b):
    return lax.dot_general(a, b, (((1,), (1,)), ((), ())), preferred_element_type=F32)


def _full(shape):
    return pl.BlockSpec(shape, lambda *_: (0,) * len(shape))


def _layer(stack, l):
    return pl.BlockSpec((None,) + stack.shape[1:], lambda *_: (l,) + (0,) * (stack.ndim - 1),
                        pipeline_mode=pl.Buffered(1))


def _ffn_kernel(x_ref, g_ref, w13_ref, w2_ref, gf_ref, o_ref, *, final):
    x = x_ref[...]
    r = lax.rsqrt(jnp.mean(x * x, axis=-1, keepdims=True) + EPS)
    h = (x * g_ref[...]).astype(BF16)
    d_ff = w2_ref.shape[0]
    acc = jnp.zeros(x.shape, F32)
    for lo in range(0, d_ff, FFN_CHUNK):
        hi = min(lo + FFN_CHUNK, d_ff)
        a = _dot(h, w13_ref[:, lo:hi]) * r
        g = _dot(h, w13_ref[:, d_ff + lo:d_ff + hi]) * r
        act = (a * (1.0 / (1.0 + jnp.exp(-a))) * g).astype(BF16)
        acc = acc + _dot(act, w2_ref[lo:hi, :])
    y = x + 0.5 * acc
    if final:
        y = _rms(y, gf_ref[...])
    o_ref[...] = y


def _ffn(x, g, w13, w2, gf, l, final):
    b, s, d = x.shape
    tm = TOKEN_TILE
    xspec = pl.BlockSpec((None, tm, d), lambda bi, i: (bi, i, 0))
    return pl.pallas_call(
        functools.partial(_ffn_kernel, final=final),
        grid=(b, s // tm),
        in_specs=[xspec, _full(g.shape), _layer(w13, l), _layer(w2, l), _full(gf.shape)],
        out_specs=xspec,
        out_shape=jax.ShapeDtypeStruct(x.shape, F32),
        compiler_params=_params(),
        name="ffn",
    )(x, g, w13, w2, gf)


def _store_values_t(vt_ref, v, heads):
    vt = v.T.astype(BF16)
    ones = jnp.ones((V_ROWS - LANES, vt.shape[1]), BF16)
    for hd in range(heads):
        vt_ref[hd * V_ROWS:hd * V_ROWS + LANES, :] = vt[hd * LANES:(hd + 1) * LANES]
        vt_ref[hd * V_ROWS + LANES:(hd + 1) * V_ROWS, :] = ones


def _mix_in_kernel(x_ref, g_ref, w_ref, cqg_ref, wq_ref, ckvg_ref, wkn_ref, wv_ref,
                   cos_ref, sin_ref, alibi_ref,
                   qa_ref, ka_ref, vat_ref, qb_ref, kb_ref, vbt_ref, qc_ref, kc_ref, vct_ref, kn2_ref):
    x = x_ref[...]
    r = lax.rsqrt(jnp.mean(x * x, axis=-1, keepdims=True) + EPS)
    h = (x * g_ref[...]).astype(BF16)
    cos2 = cos_ref[...]
    sin2 = sin_ref[...]

    zc = _dot(h, w_ref[:, 2560:3328]) * r
    cqn = _rms(zc[:, 0:C_Q_RANK], cqg_ref[...]).astype(BF16)
    ckvn = _rms(zc[:, C_Q_RANK:C_Q_RANK + C_KV_RANK], ckvg_ref[...]).astype(BF16)
    kr = zc[:, C_Q_RANK + C_KV_RANK:]
    kc_ref[:, 512:640] = (kr * cos2 + pltpu.roll(kr, C_ROPE, 1) * sin2).astype(BF16)

    za = _dot(h, w_ref[:, 0:1536]) * r
    qa_ref[...] = (za[:, 0:512] * (A_QK_DIM ** -0.5 * LOG2E)).astype(BF16)
    for hd in range(A_HEADS):
        ka_ref[:, 2 * hd * LANES:(2 * hd + 1) * LANES] = (
            za[:, 512 + hd * LANES:512 + (hd + 1) * LANES].astype(BF16))
        ka_ref[:, (2 * hd + 1) * LANES:(2 * hd + 2) * LANES] = alibi_ref[:, hd * LANES:(hd + 1) * LANES]
    _store_values_t(vat_ref, za[:, 1024:1536], A_HEADS)
    kb16 = za[:, 512:1024].astype(BF16)
    kk = kb16 * kb16
    map_of_col = lax.broadcasted_iota(jnp.int32, (A_HEADS * LANES, LANES), 0) // A_QK_DIM
    ind = (map_of_col == lax.broadcasted_iota(jnp.int32, (A_HEADS * LANES, LANES), 1)).astype(BF16)
    kn2 = jnp.broadcast_to(jnp.max(_dot(kk, ind), axis=0, keepdims=True), kn2_ref.shape)

    @pl.when(pl.program_id(1) == 0)
    def _():
        kn2_ref[...] = kn2

    @pl.when(pl.program_id(1) > 0)
    def _():
        kn2_ref[...] = jnp.maximum(kn2_ref[...], kn2)

    c_scale = (C_NOPE + C_ROPE) ** -0.5 * LOG2E
    q = _dot(cqn, wq_ref[...])
    for hd in range(C_HEADS):
        lo = hd * 2 * LANES
        qc_ref[:, lo:lo + LANES] = (q[:, lo:lo + LANES] * c_scale).astype(BF16)
        qp = q[:, lo + LANES:lo + 2 * LANES]
        qpr = qp * cos2 + pltpu.roll(qp, C_ROPE, 1) * sin2
        qc_ref[:, lo + LANES:lo + 2 * LANES] = (qpr * c_scale).astype(BF16)
    kc_ref[:, 0:512] = _dot(ckvn, wkn_ref[...]).astype(BF16)
    _store_values_t(vct_ref, _dot(ckvn, wv_ref[...]), C_HEADS)

    zb = _dot(h, w_ref[:, 1536:2560]) * r
    qb_ref[...] = (zb[:, 0:512] * (B_HEAD_DIM ** -0.5 * LOG2E)).astype(BF16)
    kb_ref[...] = zb[:, 512:768].astype(BF16)
    vbt_ref[...] = zb[:, 768:1024].T.astype(BF16)


def _mix_in(x, g, w, cqg, wq, ckvg, wkn, wv, cos2, sin2, alibi, l):
    b, s, d = x.shape
    tm = TOKEN_TILE
    row = lambda c: pl.BlockSpec((None, tm, c), lambda bi, i: (bi, i, 0))
    colT = pl.BlockSpec((None, 4 * V_ROWS, tm), lambda bi, i: (bi, 0, i))
    tab = pl.BlockSpec((tm, LANES), lambda bi, i: (i, 0))
    sd = lambda c: jax.ShapeDtypeStruct((b, s, c), BF16)
    sdT = jax.ShapeDtypeStruct((b, 4 * V_ROWS, s), BF16)
    return pl.pallas_call(
        _mix_in_kernel,
        grid=(b, s // tm),
        in_specs=[row(d), _full(g.shape), _layer(w, l), _full(cqg.shape), _layer(wq, l),
                  _full(ckvg.shape), _layer(wkn, l), _layer(wv, l), tab, tab,
                  pl.BlockSpec((tm, alibi.shape[1]), lambda bi, i: (i % (K_TILE // tm), 0))],
        out_specs=[row(512), row(1024), colT, row(512), row(256),
                   pl.BlockSpec((None, 2 * LANES, tm), lambda bi, i: (bi, 0, i)),
                   row(1024), row(640), colT,
                   pl.BlockSpec((None, 8, LANES), lambda bi, i: (bi, 0, 0))],
        out_shape=[sd(512), sd(1024), sdT, sd(512), sd(256),
                   jax.ShapeDtypeStruct((b, 2 * LANES, s), BF16), sd(1024), sd(640), sdT,
                   jax.ShapeDtypeStruct((b, 8, LANES), F32)],
        compiler_params=_params(),
        name="mix_in",
    )(x, g, w, cqg, wq, ckvg, wkn, wv, cos2, sin2, alibi)


OVERFLOW_GUARD = 2.0 ** 60
PROBE_KEYS = 128
SKIP_EXPONENT = 64.0
NORM_MARGIN = 1.02
MASKED_REFERENCE = 1e30
P_BUFFERS = 2


def _value_product(vt, p):
    return _dot(vt, p)[:ACC_ROWS]


def _tile_stats(st, vt):
    m_t = jnp.max(st, axis=0, keepdims=True)
    return m_t, _value_product(vt, jnp.exp2(st - m_t).astype(BF16))


def _flash_fixed_reference(n, qk_fn, v_fn, ref_fn, acc, p_refs):
    def weights(t):
        p_refs[t % len(p_refs)][...] = jnp.exp2(qk_fn(t) - ref_fn(t)).astype(BF16)

    if n:
        weights(0)
    for t in range(n):
        if t + 1 < n:
            weights(t + 1)
        acc = acc + _value_product(v_fn(t), p_refs[t % len(p_refs)][...])
    return acc


def _flash_running_max(n, score_fn, v_fn, width):
    def body(kt, carry):
        m, acc = carry
        m_t, pv_t = _tile_stats(score_fn(kt), v_fn(kt))
        m_new = jnp.maximum(m, m_t)
        return m_new, jnp.exp2(m - m_new) * acc + jnp.exp2(m_t - m_new) * pv_t

    init = (jnp.full((1, width), -jnp.inf, F32), jnp.zeros((ACC_ROWS, width), F32))
    return lax.fori_loop(0, n, body, init)[1]


def _denominator_overflowed(acc):
    l = acc[LANES:LANES + 1]
    return jnp.max(jnp.where(l < OVERFLOW_GUARD, 0.0, 1.0)) > 0.5


def _attn_a_kernel(c_ref, q_ref, k_ref, vt_ref, kn2_ref, lam_ref, lam_init_ref, g_ref, o_ref,
                   qq_ref, sd_ref, acc_ref, *p_refs):
    hd = pl.program_id(1)
    qi = pl.program_id(2)
    tq = q_ref.shape[0]
    assert tq == K_TILE
    n_kt = k_ref.shape[0] // K_TILE
    c = c_ref[hd]

    q = q_ref[...]
    lane = lax.broadcasted_iota(jnp.int32, q.shape, 1)
    zero = jnp.zeros_like(q)
    qm = jnp.concatenate([jnp.where(lane < A_QK_DIM, q, zero),
                          jnp.where(lane >= A_QK_DIM, q, zero)], axis=0)
    sel = (lax.broadcasted_iota(jnp.int32, qm.shape, 1) < ALIBI_TERMS).astype(BF16)
    for slot, sign in enumerate((-1.0, 1.0, 0.0)):
        qq_ref[slot, :, 0:LANES] = qm
        qq_ref[slot, :, LANES:2 * LANES] = sel * sign

    def keys(kt):
        return k_ref[pl.ds(pl.multiple_of(kt * K_TILE, K_TILE), K_TILE), :]

    def values(kt):
        return vt_ref[:, pl.ds(pl.multiple_of(kt * K_TILE, K_TILE), K_TILE)]

    q0 = qi * tq
    rel = (lax.broadcasted_iota(jnp.int32, (K_TILE, tq), 0)
           - lax.broadcasted_iota(jnp.int32, (K_TILE, tq), 1))

    def biased_scores(kt):
        dist = jnp.abs(rel + (kt * K_TILE - q0)).astype(F32) * c
        return _dot_nt(keys(kt), qq_ref[2]) - jnp.concatenate([dist, dist], axis=1)

    def finish(acc):
        lp = lam_ref[...]
        lam_init = lam_init_ref[...]
        lam = (jnp.exp(jnp.sum(lp[0:1] * lp[1:2], axis=1, keepdims=True))
               - jnp.exp(jnp.sum(lp[2:3] * lp[3:4], axis=1, keepdims=True)) + lam_init)
        inv_l = 1.0 / acc[LANES:LANES + 1]
        o = acc[:LANES, :tq] * inv_l[:, :tq] - lam * (acc[:LANES, tq:] * inv_l[:, tq:])
        o = o * lax.rsqrt(jnp.mean(o * o, axis=0, keepdims=True) + EPS) * g_ref[...]
        o = o * (1.0 - lam_init)
        o_ref[...] = o.T.astype(BF16)

    kd = qi
    sd_ref[...] = biased_scores(kd)
    m0 = jnp.max(sd_ref[...], axis=0, keepdims=True)
    q_row = lax.broadcasted_iota(jnp.int32, (1, tq), 1).astype(F32) * c
    q_row = jnp.concatenate([q_row, q_row], axis=1)

    def key_tile(t):
        t = jnp.int32(t - 1)
        kt = t + (t >= kd).astype(jnp.int32)
        return kt, (kt < kd).astype(jnp.int32)

    def qk_fn(t):
        if t == 0:
            return sd_ref[...]
        kt, before = key_tile(t)
        return _dot_nt(keys(kt), qq_ref[before])

    def v_fn(t):
        return values(kd if t == 0 else key_tile(t)[0])

    def ref_fn(t):
        if t == 0:
            return m0
        kt, before = key_tile(t)
        sgn = (2 * before - 1).astype(F32)
        return m0 - ((kt * K_TILE - q0).astype(F32) * c - q_row) * sgn

    lane8 = lax.broadcasted_iota(jnp.int32, kn2_ref.shape, 1)
    qsq = q.astype(F32) * q.astype(F32)
    excess = []
    for m, in_map in enumerate((lane < A_QK_DIM, lane >= A_QK_DIM)):
        k2 = jnp.max(jnp.where(lane8 == 2 * hd + m, kn2_ref[...], 0.0))
        q2 = jnp.max(jnp.sum(jnp.where(in_map, qsq, 0.0), axis=1, keepdims=True))
        score_max = jnp.sqrt(jnp.full((1, LANES), q2 * k2, F32)) * NORM_MARGIN
        excess.append(score_max - jnp.min(m0[:, m * tq:(m + 1) * tq]))
    d = lax.broadcasted_iota(jnp.int32, (1, LANES), 1).astype(F32)
    reach = ((d >= 1.0) & (d <= n_kt - 1.0)
             & (jnp.maximum(*excess) - c * ((d - 1.0) * K_TILE + 1.0) > -SKIP_EXPONENT))
    radius = jnp.sum(jnp.where(reach, 1.0, 0.0))
    max_ring = (n_kt - 1) // 2 - 1

    @pl.when(radius > max_ring + 0.5)
    def _():
        acc_ref[...] = _flash_fixed_reference(n_kt, qk_fn, v_fn, ref_fn,
                                              jnp.zeros((ACC_ROWS, 2 * tq), F32), p_refs)

    @pl.when(radius <= max_ring + 0.5)
    def _():
        def weights(kt_raw, before, p_ref):
            kt = jnp.clip(kt_raw, 0, n_kt - 1)
            outside = jnp.where(kt == kt_raw, 0.0, MASKED_REFERENCE)
            ref = m0 - ((kt * K_TILE - q0).astype(F32) * c - q_row) * (2.0 * before - 1.0) + outside
            p_ref[...] = jnp.exp2(_dot_nt(keys(kt), qq_ref[before]) - ref).astype(BF16)
            return kt

        def ring(carry):
            dist, dist_f, acc = carry
            k_before = weights(kd - dist, 1, p_refs[0])
            k_after = weights(kd + dist, 0, p_refs[1])
            acc = (acc + _value_product(values(k_before), p_refs[0][...])
                   + _value_product(values(k_after), p_refs[1][...]))
            return dist + 1, dist_f + 1.0, acc

        acc0 = _value_product(values(kd), jnp.exp2(sd_ref[...] - m0).astype(BF16))
        acc_ref[...] = lax.while_loop(lambda carry: carry[1] < radius + 0.5, ring,
                                      (jnp.int32(1), jnp.float32(1.0), acc0))[2]

    acc = acc_ref[...]
    finish(acc)

    @pl.when(_denominator_overflowed(acc))
    def _():
        finish(_flash_running_max(n_kt, biased_scores, values, 2 * tq))


def _attn_a(c_alibi, qa, ka, vat, kn2, lam_p, lam_init, g_col):
    b, s, _ = qa.shape
    tq = A_Q_TILE
    return pl.pallas_call(
        _attn_a_kernel,
        grid=(b, A_HEADS, s // tq),
        in_specs=[
            pl.BlockSpec(memory_space=pltpu.SMEM),
            pl.BlockSpec((None, tq, LANES), lambda bi, h, i: (bi, i, h)),
            pl.BlockSpec((None, s, 2 * LANES), lambda bi, h, i: (bi, 0, h)),
            pl.BlockSpec((None, V_ROWS, s), lambda bi, h, i: (bi, h, 0)),
            pl.BlockSpec((None,) + kn2.shape[1:], lambda bi, h, i: (bi, 0, 0)),
            _full(lam_p.shape), _full(lam_init.shape), _full(g_col.shape),
        ],
        out_specs=pl.BlockSpec((None, tq, LANES), lambda bi, h, i: (bi, i, h)),
        out_shape=jax.ShapeDtypeStruct((b, s, MIX_WIDTH), BF16),
        scratch_shapes=[pltpu.VMEM((3, 2 * tq, 2 * LANES), BF16), pltpu.VMEM((K_TILE, 2 * tq), F32),
                        pltpu.VMEM((ACC_ROWS, 2 * tq), F32),
                       ] + [pltpu.VMEM((K_TILE, 2 * tq), BF16)] * P_BUFFERS,
        compiler_params=_params(),
        name="attn_a",
    )(c_alibi, qa, ka, vat, kn2, lam_p, lam_init, g_col)


def _attn_c_kernel(q_ref, kn_ref, kpe_ref, vt_ref, o_ref, *p_refs):
    q = q_ref[...]
    tq = q.shape[0]
    tk = C_K_TILE
    n_kt = kn_ref.shape[0] // tk

    def scores(kt):
        k0 = pl.multiple_of(kt * tk, tk)
        k = jnp.concatenate([kn_ref[pl.ds(k0, tk), :], kpe_ref[pl.ds(k0, tk), :]], axis=1)
        return _dot_nt(k, q)

    def values(kt):
        return vt_ref[:, pl.ds(pl.multiple_of(kt * tk, tk), tk)]

    def finish(acc):
        o_ref[...] = (acc[:LANES] * (1.0 / acc[LANES:LANES + 1])).T.astype(BF16)

    probe = jnp.concatenate([kn_ref[0:PROBE_KEYS, :], kpe_ref[0:PROBE_KEYS, :]], axis=1)
    m0 = jnp.max(_dot_nt(probe, q), axis=0, keepdims=True)
    acc = _flash_fixed_reference(n_kt, lambda t: scores(jnp.int32(t)), lambda t: values(jnp.int32(t)),
                                 lambda t: m0, jnp.zeros((ACC_ROWS, tq), F32), p_refs)
    finish(acc)

    @pl.when(_denominator_overflowed(acc))
    def _():
        finish(_flash_running_max(n_kt, scores, values, tq))


def _attn_c(qc, kc, vct):
    b, s, _ = qc.shape
    tq = Q_TILE
    return pl.pallas_call(
        _attn_c_kernel,
        grid=(b, C_HEADS, s // tq),
        in_specs=[
            pl.BlockSpec((None, tq, 2 * LANES), lambda bi, h, i: (bi, i, h)),
            pl.BlockSpec((None, s, LANES), lambda bi, h, i: (bi, 0, h)),
            pl.BlockSpec((None, s, LANES), lambda bi, h, i: (bi, 0, C_HEADS)),
            pl.BlockSpec((None, V_ROWS, s), lambda bi, h, i: (bi, h, 0)),
        ],
        out_specs=pl.BlockSpec((None, tq, LANES), lambda bi, h, i: (bi, i, h)),
        out_shape=jax.ShapeDtypeStruct((b, s, MIX_WIDTH), BF16),
        scratch_shapes=[pltpu.VMEM((C_K_TILE, tq), BF16)] * P_BUFFERS,
        compiler_params=_params(),
        name="attn_c",
    )(qc, kc, kc, vct)


def _attn_b_kernel(sinks_ref, q_ref, kp_ref, ko_ref, kx_ref, vp_ref, vo_ref, vx_ref, bias_ref, o_ref):
    first = pl.program_id(1) * B_STEP_BLOCKS
    nb = pl.num_programs(1) * B_STEP_BLOCKS
    group = B_Q_HEADS // B_KV_HEADS
    half = lax.broadcasted_iota(jnp.int32, (BAND, LANES), 1) // B_HEAD_DIM
    row_half = lax.broadcasted_iota(jnp.int32, (LANES, BAND), 0) // B_HEAD_DIM

    for hk in range(B_KV_HEADS):
        heads = range(hk * group, (hk + 1) * group)
        ksl = slice(hk * LANES, (hk + 1) * LANES)
        k_all = jnp.concatenate([kp_ref[:, ksl], ko_ref[:, ksl], kx_ref[:, ksl]], axis=0)
        vt_all = jnp.concatenate([vp_ref[ksl, :], vo_ref[ksl, :], vx_ref[ksl, :]], axis=1)
        sink = jnp.concatenate([jnp.full((1, BAND), sinks_ref[h] * LOG2E, F32) for h in heads], axis=1)
        for u in range(B_STEP_BLOCKS):
            rows = slice(u * BAND, (u + 1) * BAND)
            qg = []
            for h in heads:
                qh = q_ref[rows, (h // 2) * LANES:(h // 2 + 1) * LANES]
                qg.append(jnp.where(half == h % 2, qh, jnp.zeros_like(qh)))
            bias = bias_ref[hk]
            if u == 0:
                pen = jnp.where(first == 0, NEG_INF, 0.0)
                bias = jnp.concatenate([bias[:BAND] + pen, bias[BAND:]], axis=0)
            if u == B_STEP_BLOCKS - 1:
                pen = jnp.where(first + u == nb - 1, NEG_INF, 0.0)
                bias = jnp.concatenate([bias[:2 * BAND], bias[2 * BAND:] + pen], axis=0)
            sc = _dot_nt(k_all[u * BAND:(u + 3) * BAND], jnp.concatenate(qg, axis=0))
            sc = sc + bias
            m = jnp.maximum(jnp.max(sc, axis=0, keepdims=True), sink)
            e = jnp.exp2(sc - m)
            den = jnp.sum(e, axis=0, keepdims=True) + jnp.exp2(sink - m)
            ot = _dot(vt_all[:, u * BAND:(u + 3) * BAND], e.astype(BF16)) * (1.0 / den)
            for c in range(group // 2):
                pair = jnp.where(row_half == 0, ot[:, 2 * c * BAND:(2 * c + 1) * BAND],
                                 ot[:, (2 * c + 1) * BAND:(2 * c + 2) * BAND])
                col = hk * (group // 2) + c
                o_ref[rows, col * LANES:(col + 1) * LANES] = pair.T.astype(BF16)


def _band_bias_table():
    kj = jnp.arange(3 * BAND)[:, None] - BAND
    dist = jnp.abs(kj - jnp.arange(BAND)[None, :])
    per_head = [jnp.where(dist <= BAND, -(2.0 ** -(h + 1) * LOG2E) * dist.astype(F32), NEG_INF)
                for h in range(B_Q_HEADS)]
    group = B_Q_HEADS // B_KV_HEADS
    return jnp.stack([jnp.concatenate(per_head[g * group:(g + 1) * group], axis=1)
                      for g in range(B_KV_HEADS)])


def _attn_b(sinks, qb, kb, vbt, bias):
    b, s, _ = qb.shape
    nb = s // BAND
    nq = B_STEP_BLOCKS
    rows = nq * BAND
    prev = lambda j: jnp.maximum(j * nq - 1, 0)
    nxt = lambda j: jnp.minimum(j * nq + nq, nb - 1)
    k_edge = lambda f: pl.BlockSpec((None, BAND, 2 * LANES), lambda bi, j: (bi, f(j), 0))
    v_edge = lambda f: pl.BlockSpec((None, 2 * LANES, BAND), lambda bi, j: (bi, 0, f(j)))
    k_own = pl.BlockSpec((None, rows, 2 * LANES), lambda bi, j: (bi, j, 0))
    v_own = pl.BlockSpec((None, 2 * LANES, rows), lambda bi, j: (bi, 0, j))
    return pl.pallas_call(
        _attn_b_kernel,
        grid=(b, nb // nq),
        in_specs=[pl.BlockSpec(memory_space=pltpu.SMEM),
                  pl.BlockSpec((None, rows, MIX_WIDTH), lambda bi, j: (bi, j, 0)),
                  k_edge(prev), k_own, k_edge(nxt), v_edge(prev), v_own, v_edge(nxt),
                  _full(bias.shape)],
        out_specs=pl.BlockSpec((None, rows, MIX_WIDTH), lambda bi, j: (bi, j, 0)),
        out_shape=jax.ShapeDtypeStruct((b, s, MIX_WIDTH), BF16),
        compiler_params=_params(),
        name="attn_b",
    )(sinks, qb, kb, kb, kb, vbt, vbt, vbt, bias)


def _merge_kernel(x_ref, g_ref, oa_ref, ob_ref, oc_ref, wg_ref, bg_ref, wb_ref, wo_ref, o_ref):
    x = x_ref[...]
    d = x.shape[1]
    r = lax.rsqrt(jnp.mean(x * x, axis=-1, keepdims=True) + EPS)
    h = (x * g_ref[...]).astype(BF16)
    merged = jnp.zeros(x.shape, F32)
    for n, br_ref in enumerate((oa_ref, ob_ref, oc_ref)):
        z = _dot(h, wg_ref[:, n * d:(n + 1) * d]) * r + bg_ref[:, n * d:(n + 1) * d]
        gate = 1.0 / (1.0 + jnp.exp(-z))
        merged = merged + gate * _dot(br_ref[...], wb_ref[n])
    o_ref[...] = x + _dot(merged.astype(BF16), wo_ref[...])


def _merge(x, g, oa, ob, oc, wg, bg, wb, wo, l):
    b, s, d = x.shape
    tm = TOKEN_TILE
    row = lambda c: pl.BlockSpec((None, tm, c), lambda bi, i: (bi, i, 0))
    return pl.pallas_call(
        _merge_kernel,
        grid=(b, s // tm),
        in_specs=[row(d), _full(g.shape), row(MIX_WIDTH), row(MIX_WIDTH), row(MIX_WIDTH),
                  _layer(wg, l), _full(bg.shape), _layer(wb, l), _layer(wo, l)],
        out_specs=row(d),
        out_shape=jax.ShapeDtypeStruct(x.shape, F32),
        compiler_params=_params(),
        name="merge",
    )(x, g, oa, ob, oc, wg, bg, wb, wo)


def _prep_w_in(w_in):
    dup = lambda w: jnp.concatenate([w[..., 0:64], w[..., 0:64], w[..., 64:128], w[..., 64:128]], axis=-1)
    kr = w_in[..., 2944:3008]
    return jnp.concatenate([
        w_in[..., 0:2048], dup(w_in[..., 2048:2176]), dup(w_in[..., 2176:2304]), w_in[..., 2304:2944],
        kr, kr[..., 32:64], kr[..., 0:32]], axis=-1).astype(BF16)


def _prep_w_uq(w_uq):
    pe = w_uq[..., C_NOPE:]
    half = C_ROPE // 2
    w = jnp.concatenate([w_uq, pe[..., half:], pe[..., :half]], axis=-1)
    return w.reshape(w_uq.shape[0], w_uq.shape[1], C_HEADS * 2 * LANES).astype(BF16)


def _alibi_key_table(c_alibi):
    term = c_alibi[None, :] * jnp.arange(K_TILE, dtype=F32)[:, None]
    pieces = []
    for _ in range(ALIBI_TERMS):
        piece = term.astype(BF16)
        pieces.append(piece)
        term = term - piece.astype(F32)
    tab = jnp.stack(pieces, axis=-1)
    tab = jnp.pad(tab, ((0, 0), (0, 0), (0, LANES - ALIBI_TERMS)))
    return tab.reshape(K_TILE, -1)


def kernel(x, ffn1_norm, ffn1_w13, ffn1_w2, mix_norm, w_in, w_gate, b_gate, a_lambda, a_subln,
           b_sinks, c_q_norm, c_w_uq, c_kv_norm, c_w_ukv, w_branch, w_out, ffn2_norm, ffn2_w13,
           ffn2_w2, final_norm):
    b, s, d = x.shape
    depth = w_in.shape[0]
    assert s % K_TILE == 0 and s % C_K_TILE == 0 and K_TILE % TOKEN_TILE == 0 and d % LANES == 0

    pos = jnp.arange(s, dtype=F32)
    inv_freq = ROPE_THETA ** (-jnp.arange(0, C_ROPE, 2, dtype=F32) / C_ROPE)
    ang = pos[:, None] * inv_freq[None, :]
    cos, sin = jnp.cos(ang), jnp.sin(ang)
    pad = jnp.zeros((s, LANES - C_ROPE), F32)
    cos2 = jnp.concatenate([cos, cos, pad], axis=1)
    sin2 = jnp.concatenate([-sin, sin, pad], axis=1)
    c_alibi = 2.0 ** (-8.0 * jnp.arange(1, A_HEADS + 1, dtype=F32) / A_HEADS) * LOG2E
    alibi = _alibi_key_table(c_alibi)
    band_bias = _band_bias_table()
    row = lambda v: v.reshape(1, -1)
    gf = row(final_norm)

    bf = lambda w: w.astype(BF16)
    w13_1, w2_1, w13_2, w2_2 = bf(ffn1_w13), bf(ffn1_w2), bf(ffn2_w13), bf(ffn2_w2)
    w_in_x, w_uq_x = _prep_w_in(w_in), _prep_w_uq(c_w_uq)
    w_kn = bf(c_w_ukv[..., :C_NOPE].reshape(depth, C_KV_RANK, -1))
    w_v = bf(c_w_ukv[..., C_NOPE:].reshape(depth, C_KV_RANK, -1))
    w_g, w_b, w_o = bf(w_gate), bf(w_branch), bf(w_out)

    for l in range(depth):
        x = _ffn(x, row(ffn1_norm[l]), w13_1, w2_1, gf, l, False)

        qa, ka, vat, qb, kb, vb, qc, kc, vct, kn2 = _mix_in(
            x, row(mix_norm[l]), w_in_x, row(c_q_norm[l]), w_uq_x, row(c_kv_norm[l]), w_kn, w_v,
            cos2, sin2, alibi, l)

        lam_init = jnp.full((1, 1), 0.8 - 0.6 * math.exp(-0.3 * l), F32)
        oa = _attn_a(c_alibi, qa, ka, vat, kn2, a_lambda[l], lam_init, a_subln[l].reshape(-1, 1))
        ob = _attn_b(b_sinks[l], qb, kb, vb, band_bias)
        oc = _attn_c(qc, kc, vct)

        x = _merge(x, row(mix_norm[l]), oa, ob, oc, w_g, row(b_gate[l]), w_b, w_o, l)
        x = _ffn(x, row(ffn2_norm[l]), w13_2, w2_2, gf, l, l == depth - 1)
    return x
```

```python
import functools
import math

import jax
import jax.numpy as jnp
from jax import lax
from jax.experimental import pallas as pl
from jax.experimental.pallas import tpu as pltpu

BF16 = jnp.bfloat16
F32 = jnp.float32

EPS = 1e-6
NEG_INF = -1e30
LOG2E = math.log2(math.e)
ROPE_THETA = 10000.0

A_HEADS = 4
A_QK_DIM = 64
B_Q_HEADS = 8
B_KV_HEADS = 2
B_HEAD_DIM = 64
BAND = 128
B_STEP_BLOCKS = 4
C_HEADS = 4
C_Q_RANK = 384
C_KV_RANK = 256
C_NOPE = 128
C_ROPE = 64
MIX_WIDTH = 512
LANES = 128

VMEM_LIMIT = 52 * 1024 * 1024

TOKEN_TILE = 512
MXU_DIM = 256
FFN_CHUNK = 6 * MXU_DIM
K_TILE = 512
C_K_TILE = 1024
Q_TILE = 2048
A_Q_TILE = K_TILE
V_ROWS = LANES + 16
ACC_ROWS = LANES + 8
ALIBI_TERMS = 3


def _params():
    return pltpu.CompilerParams(vmem_limit_bytes=VMEM_LIMIT)


def _rms(x, g):
    return x * lax.rsqrt(jnp.mean(x * x, axis=-1, keepdims=True) + EPS) * g


def _dot(a, b):
    return jnp.dot(a, b, preferred_element_type=F32)


def _dot_nt(a, b):
    return lax.dot_general(a, b, (((1,), (1,)), ((), ())), preferred_element_type=F32)


def _full(shape):
    return pl.BlockSpec(shape, lambda *_: (0,) * len(shape))


def _layer(stack, l):
    return pl.BlockSpec((None,) + stack.shape[1:], lambda *_: (l,) + (0,) * (stack.ndim - 1),
                        pipeline_mode=pl.Buffered(1))


def _ffn_kernel(x_ref, g_ref, w13_ref, w2_ref, gf_ref, o_ref, *, final):
    x = x_ref[...]
    r = lax.rsqrt(jnp.mean(x * x, axis=-1, keepdims=True) + EPS)
    h = (x * g_ref[...]).astype(BF16)
    d_ff = w2_ref.shape[0]
    acc = jnp.zeros(x.shape, F32)
    for lo in range(0, d_ff, FFN_CHUNK):
        hi = min(lo + FFN_CHUNK, d_ff)
        a = _dot(h, w13_ref[:, lo:hi]) * r
        g = _dot(h, w13_ref[:, d_ff + lo:d_ff + hi]) * r
        act = (a * (1.0 / (1.0 + jnp.exp(-a))) * g).astype(BF16)
        acc = acc + _dot(act, w2_ref[lo:hi, :])
    y = x + 0.5 * acc
    if final:
        y = _rms(y, gf_ref[...])
    o_ref[...] = y


def _ffn(x, g, w13, w2, gf, l, final):
    b, s, d = x.shape
    tm = TOKEN_TILE
    xspec = pl.BlockSpec((None, tm, d), lambda bi, i: (bi, i, 0))
    return pl.pallas_call(
        functools.partial(_ffn_kernel, final=final),
        grid=(b, s // tm),
        in_specs=[xspec, _full(g.shape), _layer(w13, l), _layer(w2, l), _full(gf.shape)],
        out_specs=xspec,
        out_shape=jax.ShapeDtypeStruct(x.shape, F32),
        compiler_params=_params(),
        name="ffn",
    )(x, g, w13, w2, gf)


def _store_values_t(vt_ref, v, heads):
    vt = v.T.astype(BF16)
    ones = jnp.ones((V_ROWS - LANES, vt.shape[1]), BF16)
    for hd in range(heads):
        vt_ref[hd * V_ROWS:hd * V_ROWS + LANES, :] = vt[hd * LANES:(hd + 1) * LANES]
        vt_ref[hd * V_ROWS + LANES:(hd + 1) * V_ROWS, :] = ones


def _mix_in_kernel(x_ref, g_ref, w_ref, cqg_ref, wq_ref, ckvg_ref, wkn_ref, wv_ref,
                   cos_ref, sin_ref, alibi_ref,
                   qa_ref, ka_ref, vat_ref, qb_ref, kb_ref, vbt_ref, qc_ref, kc_ref, vct_ref, kn2_ref):
    x = x_ref[...]
    r = lax.rsqrt(jnp.mean(x * x, axis=-1, keepdims=True) + EPS)
    h = (x * g_ref[...]).astype(BF16)
    cos2 = cos_ref[...]
    sin2 = sin_ref[...]

    zc = _dot(h, w_ref[:, 2560:3328]) * r
    cqn = _rms(zc[:, 0:C_Q_RANK], cqg_ref[...]).astype(BF16)
    ckvn = _rms(zc[:, C_Q_RANK:C_Q_RANK + C_KV_RANK], ckvg_ref[...]).astype(BF16)
    kr = zc[:, C_Q_RANK + C_KV_RANK:]
    kc_ref[:, 512:640] = (kr * cos2 + pltpu.roll(kr, C_ROPE, 1) * sin2).astype(BF16)

    za = _dot(h, w_ref[:, 0:1536]) * r
    qa_ref[...] = (za[:, 0:512] * (A_QK_DIM ** -0.5 * LOG2E)).astype(BF16)
    for hd in range(A_HEADS):
        ka_ref[:, 2 * hd * LANES:(2 * hd + 1) * LANES] = (
            za[:, 512 + hd * LANES:512 + (hd + 1) * LANES].astype(BF16))
        ka_ref[:, (2 * hd + 1) * LANES:(2 * hd + 2) * LANES] = alibi_ref[:, hd * LANES:(hd + 1) * LANES]
    _store_values_t(vat_ref, za[:, 1024:1536], A_HEADS)
    kb16 = za[:, 512:1024].astype(BF16)
    kk = kb16 * kb16
    map_of_col = lax.broadcasted_iota(jnp.int32, (A_HEADS * LANES, LANES), 0) // A_QK_DIM
    ind = (map_of_col == lax.broadcasted_iota(jnp.int32, (A_HEADS * LANES, LANES), 1)).astype(BF16)
    kn2 = jnp.broadcast_to(jnp.max(_dot(kk, ind), axis=0, keepdims=True), kn2_ref.shape)

    c_scale = (C_NOPE + C_ROPE) ** -0.5 * LOG2E
    q = _dot(cqn, wq_ref[...])
    for hd in range(C_HEADS):
        lo = hd * 2 * LANES
        qc_ref[:, lo:lo + LANES] = (q[:, lo:lo + LANES] * c_scale).astype(BF16)
        qp = q[:, lo + LANES:lo + 2 * LANES]
        qpr = qp * cos2 + pltpu.roll(qp, C_ROPE, 1) * sin2
        qc_ref[:, lo + LANES:lo + 2 * LANES] = (qpr * c_scale).astype(BF16)
    kc_ref[:, 0:512] = _dot(ckvn, wkn_ref[...]).astype(BF16)
    _store_values_t(vct_ref, _dot(ckvn, wv_ref[...]), C_HEADS)

    zb = _dot(h, w_ref[:, 1536:2560]) * r
    qb_ref[...] = (zb[:, 0:512] * (B_HEAD_DIM ** -0.5 * LOG2E)).astype(BF16)
    kb_ref[...] = zb[:, 512:768].astype(BF16)
    vbt_ref[...] = zb[:, 768:1024].T.astype(BF16)

    @pl.when(pl.program_id(1) == 0)
    def _():
        kn2_ref[...] = kn2

    @pl.when(pl.program_id(1) > 0)
    def _():
        kn2_ref[...] = jnp.maximum(kn2_ref[...], kn2)


def _mix_in(x, g, w, cqg, wq, ckvg, wkn, wv, cos2, sin2, alibi, l):
    b, s, d = x.shape
    tm = TOKEN_TILE
    row = lambda c: pl.BlockSpec((None, tm, c), lambda bi, i: (bi, i, 0))
    colT = pl.BlockSpec((None, 4 * V_ROWS, tm), lambda bi, i: (bi, 0, i))
    tab = pl.BlockSpec((tm, LANES), lambda bi, i: (i, 0))
    sd = lambda c: jax.ShapeDtypeStruct((b, s, c), BF16)
    sdT = jax.ShapeDtypeStruct((b, 4 * V_ROWS, s), BF16)
    return pl.pallas_call(
        _mix_in_kernel,
        grid=(b, s // tm),
        in_specs=[row(d), _full(g.shape), _layer(w, l), _full(cqg.shape), _layer(wq, l),
                  _full(ckvg.shape), _layer(wkn, l), _layer(wv, l), tab, tab,
                  pl.BlockSpec((tm, alibi.shape[1]), lambda bi, i: (i % (K_TILE // tm), 0))],
        out_specs=[row(512), row(1024), colT, row(512), row(256),
                   pl.BlockSpec((None, 2 * LANES, tm), lambda bi, i: (bi, 0, i)),
                   row(1024), row(640), colT,
                   pl.BlockSpec((None, 8, LANES), lambda bi, i: (bi, 0, 0))],
        out_shape=[sd(512), sd(1024), sdT, sd(512), sd(256),
                   jax.ShapeDtypeStruct((b, 2 * LANES, s), BF16), sd(1024), sd(640), sdT,
                   jax.ShapeDtypeStruct((b, 8, LANES), F32)],
        compiler_params=_params(),
        name="mix_in",
    )(x, g, w, cqg, wq, ckvg, wkn, wv, cos2, sin2, alibi)


OVERFLOW_GUARD = 2.0 ** 60
PROBE_KEYS = 128
SKIP_EXPONENT = 64.0
NORM_MARGIN = 1.02
MASKED_REFERENCE = 1e30
P_BUFFERS = 2


def _value_product(vt, p):
    return _dot(vt, p)[:ACC_ROWS]


def _tile_stats(st, vt):
    m_t = jnp.max(st, axis=0, keepdims=True)
    return m_t, _value_product(vt, jnp.exp2(st - m_t).astype(BF16))


def _flash_fixed_reference(n, qk_fn, v_fn, ref_fn, acc, p_refs):
    def weights(t):
        p_refs[t % len(p_refs)][...] = jnp.exp2(qk_fn(t) - ref_fn(t)).astype(BF16)

    if n:
        weights(0)
    for t in range(n):
        if t + 1 < n:
            weights(t + 1)
        acc = acc + _value_product(v_fn(t), p_refs[t % len(p_refs)][...])
    return acc


def _flash_running_max(n, score_fn, v_fn, width):
    def body(kt, carry):
        m, acc = carry
        m_t, pv_t = _tile_stats(score_fn(kt), v_fn(kt))
        m_new = jnp.maximum(m, m_t)
        return m_new, jnp.exp2(m - m_new) * acc + jnp.exp2(m_t - m_new) * pv_t

    init = (jnp.full((1, width), -jnp.inf, F32), jnp.zeros((ACC_ROWS, width), F32))
    return lax.fori_loop(0, n, body, init)[1]


def _denominator_overflowed(acc):
    l = acc[LANES:LANES + 1]
    return jnp.max(jnp.where(l < OVERFLOW_GUARD, 0.0, 1.0)) > 0.5


def _attn_a_kernel(c_ref, q_ref, k_ref, vt_ref, kn2_ref, lam_ref, lam_init_ref, g_ref, o_ref,
                   qq_ref, sd_ref, acc_ref, *p_refs):
    hd = pl.program_id(1)
    qi = pl.program_id(2)
    tq = q_ref.shape[0]
    assert tq == K_TILE
    n_kt = k_ref.shape[0] // K_TILE
    c = c_ref[hd]

    q = q_ref[...]
    lane = lax.broadcasted_iota(jnp.int32, q.shape, 1)
    zero = jnp.zeros_like(q)
    qm = jnp.concatenate([jnp.where(lane < A_QK_DIM, q, zero),
                          jnp.where(lane >= A_QK_DIM, q, zero)], axis=0)
    sel = (lax.broadcasted_iota(jnp.int32, qm.shape, 1) < ALIBI_TERMS).astype(BF16)
    for slot, sign in enumerate((-1.0, 1.0, 0.0)):
        qq_ref[slot, :, 0:LANES] = qm
        qq_ref[slot, :, LANES:2 * LANES] = sel * sign

    def keys(kt):
        return k_ref[pl.ds(pl.multiple_of(kt * K_TILE, K_TILE), K_TILE), :]

    def values(kt):
        return vt_ref[:, pl.ds(pl.multiple_of(kt * K_TILE, K_TILE), K_TILE)]

    q0 = qi * tq
    rel = (lax.broadcasted_iota(jnp.int32, (K_TILE, tq), 0)
           - lax.broadcasted_iota(jnp.int32, (K_TILE, tq), 1))

    def biased_scores(kt):
        dist = jnp.abs(rel + (kt * K_TILE - q0)).astype(F32) * c
        return _dot_nt(keys(kt), qq_ref[2]) - jnp.concatenate([dist, dist], axis=1)

    def finish(acc):
        lp = lam_ref[...]
        lam_init = lam_init_ref[...]
        lam = (jnp.exp(jnp.sum(lp[0:1] * lp[1:2], axis=1, keepdims=True))
               - jnp.exp(jnp.sum(lp[2:3] * lp[3:4], axis=1, keepdims=True)) + lam_init)
        inv_l = 1.0 / acc[LANES:LANES + 1]
        o = acc[:LANES, :tq] * inv_l[:, :tq] - lam * (acc[:LANES, tq:] * inv_l[:, tq:])
        o = o * lax.rsqrt(jnp.mean(o * o, axis=0, keepdims=True) + EPS) * g_ref[...]
        o = o * (1.0 - lam_init)
        o_ref[...] = o.T.astype(BF16)

    kd = qi
    sd_ref[...] = biased_scores(kd)
    m0 = jnp.max(sd_ref[...], axis=0, keepdims=True)
    q_row = lax.broadcasted_iota(jnp.int32, (1, tq), 1).astype(F32) * c
    q_row = jnp.concatenate([q_row, q_row], axis=1)

    def key_tile(t):
        t = jnp.int32(t - 1)
        kt = t + (t >= kd).astype(jnp.int32)
        return kt, (kt < kd).astype(jnp.int32)

    def qk_fn(t):
        if t == 0:
            return sd_ref[...]
        kt, before = key_tile(t)
        return _dot_nt(keys(kt), qq_ref[before])

    def v_fn(t):
        return values(kd if t == 0 else key_tile(t)[0])

    def ref_fn(t):
        if t == 0:
            return m0
        kt, before = key_tile(t)
        sgn = (2 * before - 1).astype(F32)
        return m0 - ((kt * K_TILE - q0).astype(F32) * c - q_row) * sgn

    lane8 = lax.broadcasted_iota(jnp.int32, kn2_ref.shape, 1)
    qsq = q.astype(F32) * q.astype(F32)
    excess = []
    for m, in_map in enumerate((lane < A_QK_DIM, lane >= A_QK_DIM)):
        k2 = jnp.max(jnp.where(lane8 == 2 * hd + m, kn2_ref[...], 0.0))
        q2 = jnp.max(jnp.sum(jnp.where(in_map, qsq, 0.0), axis=1, keepdims=True))
        score_max = jnp.sqrt(jnp.full((1, LANES), q2 * k2, F32)) * NORM_MARGIN
        excess.append(score_max - jnp.min(m0[:, m * tq:(m + 1) * tq]))
    d = lax.broadcasted_iota(jnp.int32, (1, LANES), 1).astype(F32)
    reach = ((d >= 1.0) & (d <= n_kt - 1.0)
             & (jnp.maximum(*excess) - c * ((d - 1.0) * K_TILE + 1.0) > -SKIP_EXPONENT))
    radius = jnp.sum(jnp.where(reach, 1.0, 0.0))
    max_ring = (n_kt - 1) // 2 - 1

    @pl.when(radius > max_ring + 0.5)
    def _():
        acc_ref[...] = _flash_fixed_reference(n_kt, qk_fn, v_fn, ref_fn,
                                              jnp.zeros((ACC_ROWS, 2 * tq), F32), p_refs)

    @pl.when(radius <= max_ring + 0.5)
    def _():
        def weights(kt_raw, before, p_ref):
            kt = jnp.clip(kt_raw, 0, n_kt - 1)
            outside = jnp.where(kt == kt_raw, 0.0, MASKED_REFERENCE)
            ref = m0 - ((kt * K_TILE - q0).astype(F32) * c - q_row) * (2.0 * before - 1.0) + outside
            p_ref[...] = jnp.exp2(_dot_nt(keys(kt), qq_ref[before]) - ref).astype(BF16)
            return kt

        def ring(carry):
            dist, dist_f, acc = carry
            k_before = weights(kd - dist, 1, p_refs[0])
            k_after = weights(kd + dist, 0, p_refs[1])
            acc = (acc + _value_product(values(k_before), p_refs[0][...])
                   + _value_product(values(k_after), p_refs[1][...]))
            return dist + 1, dist_f + 1.0, acc

        acc0 = _value_product(values(kd), jnp.exp2(sd_ref[...] - m0).astype(BF16))
        acc_ref[...] = lax.while_loop(lambda carry: carry[1] < radius + 0.5, ring,
                                      (jnp.int32(1), jnp.float32(1.0), acc0))[2]

    acc = acc_ref[...]
    finish(acc)

    @pl.when(_denominator_overflowed(acc))
    def _():
        finish(_flash_running_max(n_kt, biased_scores, values, 2 * tq))


def _attn_a(c_alibi, qa, ka, vat, kn2, lam_p, lam_init, g_col):
    b, s, _ = qa.shape
    tq = A_Q_TILE
    return pl.pallas_call(
        _attn_a_kernel,
        grid=(b, A_HEADS, s // tq),
        in_specs=[
            pl.BlockSpec(memory_space=pltpu.SMEM),
            pl.BlockSpec((None, tq, LANES), lambda bi, h, i: (bi, i, h)),
            pl.BlockSpec((None, s, 2 * LANES), lambda bi, h, i: (bi, 0, h)),
            pl.BlockSpec((None, V_ROWS, s), lambda bi, h, i: (bi, h, 0)),
            pl.BlockSpec((None,) + kn2.shape[1:], lambda bi, h, i: (bi, 0, 0)),
            _full(lam_p.shape), _full(lam_init.shape), _full(g_col.shape),
        ],
        out_specs=pl.BlockSpec((None, tq, LANES), lambda bi, h, i: (bi, i, h)),
        out_shape=jax.ShapeDtypeStruct((b, s, MIX_WIDTH), BF16),
        scratch_shapes=[pltpu.VMEM((3, 2 * tq, 2 * LANES), BF16), pltpu.VMEM((K_TILE, 2 * tq), F32),
                        pltpu.VMEM((ACC_ROWS, 2 * tq), F32),
                       ] + [pltpu.VMEM((K_TILE, 2 * tq), BF16)] * P_BUFFERS,
        compiler_params=_params(),
        name="attn_a",
    )(c_alibi, qa, ka, vat, kn2, lam_p, lam_init, g_col)


def _attn_c_kernel(q_ref, kn_ref, kpe_ref, vt_ref, o_ref, *p_refs):
    q = q_ref[...]
    tq = q.shape[0]
    tk = C_K_TILE
    n_kt = kn_ref.shape[0] // tk

    def scores(kt):
        k0 = pl.multiple_of(kt * tk, tk)
        k = jnp.concatenate([kn_ref[pl.ds(k0, tk), :], kpe_ref[pl.ds(k0, tk), :]], axis=1)
        return _dot_nt(k, q)

    def values(kt):
        return vt_ref[:, pl.ds(pl.multiple_of(kt * tk, tk), tk)]

    def finish(acc):
        o_ref[...] = (acc[:LANES] * (1.0 / acc[LANES:LANES + 1])).T.astype(BF16)

    probe = jnp.concatenate([kn_ref[0:PROBE_KEYS, :], kpe_ref[0:PROBE_KEYS, :]], axis=1)
    m0 = jnp.max(_dot_nt(probe, q), axis=0, keepdims=True)
    acc = _flash_fixed_reference(n_kt, lambda t: scores(jnp.int32(t)), lambda t: values(jnp.int32(t)),
                                 lambda t: m0, jnp.zeros((ACC_ROWS, tq), F32), p_refs)
    finish(acc)

    @pl.when(_denominator_overflowed(acc))
    def _():
        finish(_flash_running_max(n_kt, scores, values, tq))


def _attn_c(qc, kc, vct):
    b, s, _ = qc.shape
    tq = Q_TILE
    return pl.pallas_call(
        _attn_c_kernel,
        grid=(b, C_HEADS, s // tq),
        in_specs=[
            pl.BlockSpec((None, tq, 2 * LANES), lambda bi, h, i: (bi, i, h)),
            pl.BlockSpec((None, s, LANES), lambda bi, h, i: (bi, 0, h)),
            pl.BlockSpec((None, s, LANES), lambda bi, h, i: (bi, 0, C_HEADS)),
            pl.BlockSpec((None, V_ROWS, s), lambda bi, h, i: (bi, h, 0)),
        ],
        out_specs=pl.BlockSpec((None, tq, LANES), lambda bi, h, i: (bi, i, h)),
        out_shape=jax.ShapeDtypeStruct((b, s, MIX_WIDTH), BF16),
        scratch_shapes=[pltpu.VMEM((C_K_TILE, tq), BF16)] * P_BUFFERS,
        compiler_params=_params(),
        name="attn_c",
    )(qc, kc, kc, vct)


def _attn_b_kernel(sinks_ref, q_ref, kp_ref, ko_ref, kx_ref, vp_ref, vo_ref, vx_ref, bias_ref, o_ref):
    first = pl.program_id(1) * B_STEP_BLOCKS
    nb = pl.num_programs(1) * B_STEP_BLOCKS
    group = B_Q_HEADS // B_KV_HEADS
    half = lax.broadcasted_iota(jnp.int32, (BAND, LANES), 1) // B_HEAD_DIM
    row_half = lax.broadcasted_iota(jnp.int32, (LANES, BAND), 0) // B_HEAD_DIM

    for hk in range(B_KV_HEADS):
        heads = range(hk * group, (hk + 1) * group)
        ksl = slice(hk * LANES, (hk + 1) * LANES)
        k_all = jnp.concatenate([kp_ref[:, ksl], ko_ref[:, ksl], kx_ref[:, ksl]], axis=0)
        vt_all = jnp.concatenate([vp_ref[ksl, :], vo_ref[ksl, :], vx_ref[ksl, :]], axis=1)
        sink = jnp.concatenate([jnp.full((1, BAND), sinks_ref[h] * LOG2E, F32) for h in heads], axis=1)
        for u in range(B_STEP_BLOCKS):
            rows = slice(u * BAND, (u + 1) * BAND)
            qg = []
            for h in heads:
                qh = q_ref[rows, (h // 2) * LANES:(h // 2 + 1) * LANES]
                qg.append(jnp.where(half == h % 2, qh, jnp.zeros_like(qh)))
            bias = bias_ref[hk]
            if u == 0:
                pen = jnp.where(first == 0, NEG_INF, 0.0)
                bias = jnp.concatenate([bias[:BAND] + pen, bias[BAND:]], axis=0)
            if u == B_STEP_BLOCKS - 1:
                pen = jnp.where(first + u == nb - 1, NEG_INF, 0.0)
                bias = jnp.concatenate([bias[:2 * BAND], bias[2 * BAND:] + pen], axis=0)
            sc = _dot_nt(k_all[u * BAND:(u + 3) * BAND], jnp.concatenate(qg, axis=0))
            sc = sc + bias
            m = jnp.maximum(jnp.max(sc, axis=0, keepdims=True), sink)
            e = jnp.exp2(sc - m)
            den = jnp.sum(e, axis=0, keepdims=True) + jnp.exp2(sink - m)
            ot = _dot(vt_all[:, u * BAND:(u + 3) * BAND], e.astype(BF16)) * (1.0 / den)
            for c in range(group // 2):
                pair = jnp.where(row_half == 0, ot[:, 2 * c * BAND:(2 * c + 1) * BAND],
                                 ot[:, (2 * c + 1) * BAND:(2 * c + 2) * BAND])
                col = hk * (group // 2) + c
                o_ref[rows, col * LANES:(col + 1) * LANES] = pair.T.astype(BF16)


def _band_bias_table():
    kj = jnp.arange(3 * BAND)[:, None] - BAND
    dist = jnp.abs(kj - jnp.arange(BAND)[None, :])
    per_head = [jnp.where(dist <= BAND, -(2.0 ** -(h + 1) * LOG2E) * dist.astype(F32), NEG_INF)
                for h in range(B_Q_HEADS)]
    group = B_Q_HEADS // B_KV_HEADS
    return jnp.stack([jnp.concatenate(per_head[g * group:(g + 1) * group], axis=1)
                      for g in range(B_KV_HEADS)])


def _attn_b(sinks, qb, kb, vbt, bias):
    b, s, _ = qb.shape
    nb = s // BAND
    nq = B_STEP_BLOCKS
    rows = nq * BAND
    prev = lambda j: jnp.maximum(j * nq - 1, 0)
    nxt = lambda j: jnp.minimum(j * nq + nq, nb - 1)
    k_edge = lambda f: pl.BlockSpec((None, BAND, 2 * LANES), lambda bi, j: (bi, f(j), 0))
    v_edge = lambda f: pl.BlockSpec((None, 2 * LANES, BAND), lambda bi, j: (bi, 0, f(j)))
    k_own = pl.BlockSpec((None, rows, 2 * LANES), lambda bi, j: (bi, j, 0))
    v_own = pl.BlockSpec((None, 2 * LANES, rows), lambda bi, j: (bi, 0, j))
    return pl.pallas_call(
        _attn_b_kernel,
        grid=(b, nb // nq),
        in_specs=[pl.BlockSpec(memory_space=pltpu.SMEM),
                  pl.BlockSpec((None, rows, MIX_WIDTH), lambda bi, j: (bi, j, 0)),
                  k_edge(prev), k_own, k_edge(nxt), v_edge(prev), v_own, v_edge(nxt),
                  _full(bias.shape)],
        out_specs=pl.BlockSpec((None, rows, MIX_WIDTH), lambda bi, j: (bi, j, 0)),
        out_shape=jax.ShapeDtypeStruct((b, s, MIX_WIDTH), BF16),
        compiler_params=_params(),
        name="attn_b",
    )(sinks, qb, kb, kb, kb, vbt, vbt, vbt, bias)


def _merge_kernel(x_ref, g_ref, oa_ref, ob_ref, oc_ref, wg_ref, bg_ref, wb_ref, wo_ref, o_ref):
    x = x_ref[...]
    d = x.shape[1]
    r = lax.rsqrt(jnp.mean(x * x, axis=-1, keepdims=True) + EPS)
    h = (x * g_ref[...]).astype(BF16)
    merged = jnp.zeros(x.shape, F32)
    for n, br_ref in enumerate((oa_ref, ob_ref, oc_ref)):
        z = _dot(h, wg_ref[:, n * d:(n + 1) * d]) * r + bg_ref[:, n * d:(n + 1) * d]
        gate = 1.0 / (1.0 + jnp.exp(-z))
        merged = merged + gate * _dot(br_ref[...], wb_ref[n])
    o_ref[...] = x + _dot(merged.astype(BF16), wo_ref[...])


def _merge(x, g, oa, ob, oc, wg, bg, wb, wo, l):
    b, s, d = x.shape
    tm = TOKEN_TILE
    row = lambda c: pl.BlockSpec((None, tm, c), lambda bi, i: (bi, i, 0))
    return pl.pallas_call(
        _merge_kernel,
        grid=(b, s // tm),
        in_specs=[row(d), _full(g.shape), row(MIX_WIDTH), row(MIX_WIDTH), row(MIX_WIDTH),
                  _layer(wg, l), _full(bg.shape), _layer(wb, l), _layer(wo, l)],
        out_specs=row(d),
        out_shape=jax.ShapeDtypeStruct(x.shape, F32),
        compiler_params=_params(),
        name="merge",
    )(x, g, oa, ob, oc, wg, bg, wb, wo)


def _prep_w_in(w_in):
    dup = lambda w: jnp.concatenate([w[..., 0:64], w[..., 0:64], w[..., 64:128], w[..., 64:128]], axis=-1)
    kr = w_in[..., 2944:3008]
    return jnp.concatenate([
        w_in[..., 0:2048], dup(w_in[..., 2048:2176]), dup(w_in[..., 2176:2304]), w_in[..., 2304:2944],
        kr, kr[..., 32:64], kr[..., 0:32]], axis=-1).astype(BF16)


def _prep_w_uq(w_uq):
    pe = w_uq[..., C_NOPE:]
    half = C_ROPE // 2
    w = jnp.concatenate([w_uq, pe[..., half:], pe[..., :half]], axis=-1)
    return w.reshape(w_uq.shape[0], w_uq.shape[1], C_HEADS * 2 * LANES).astype(BF16)


def _alibi_key_table(c_alibi):
    term = c_alibi[None, :] * jnp.arange(K_TILE, dtype=F32)[:, None]
    pieces = []
    for _ in range(ALIBI_TERMS):
        piece = term.astype(BF16)
        pieces.append(piece)
        term = term - piece.astype(F32)
    tab = jnp.stack(pieces, axis=-1)
    tab = jnp.pad(tab, ((0, 0), (0, 0), (0, LANES - ALIBI_TERMS)))
    return tab.reshape(K_TILE, -1)


def kernel(x, ffn1_norm, ffn1_w13, ffn1_w2, mix_norm, w_in, w_gate, b_gate, a_lambda, a_subln,
           b_sinks, c_q_norm, c_w_uq, c_kv_norm, c_w_ukv, w_branch, w_out, ffn2_norm, ffn2_w13,
           ffn2_w2, final_norm):
    b, s, d = x.shape
    depth = w_in.shape[0]
    assert s % K_TILE == 0 and s % C_K_TILE == 0 and K_TILE % TOKEN_TILE == 0 and d % LANES == 0

    pos = jnp.arange(s, dtype=F32)
    inv_freq = ROPE_THETA ** (-jnp.arange(0, C_ROPE, 2, dtype=F32) / C_ROPE)
    ang = pos[:, None] * inv_freq[None, :]
    cos, sin = jnp.cos(ang), jnp.sin(ang)
    pad = jnp.zeros((s, LANES - C_ROPE), F32)
    cos2 = jnp.concatenate([cos, cos, pad], axis=1)
    sin2 = jnp.concatenate([-sin, sin, pad], axis=1)
    c_alibi = 2.0 ** (-8.0 * jnp.arange(1, A_HEADS + 1, dtype=F32) / A_HEADS) * LOG2E
    alibi = _alibi_key_table(c_alibi)
    band_bias = _band_bias_table()
    row = lambda v: v.reshape(1, -1)
    gf = row(final_norm)

    bf = lambda w: w.astype(BF16)
    w13_1, w2_1, w13_2, w2_2 = bf(ffn1_w13), bf(ffn1_w2), bf(ffn2_w13), bf(ffn2_w2)
    w_in_x, w_uq_x = _prep_w_in(w_in), _prep_w_uq(c_w_uq)
    w_kn = bf(c_w_ukv[..., :C_NOPE].reshape(depth, C_KV_RANK, -1))
    w_v = bf(c_w_ukv[..., C_NOPE:].reshape(depth, C_KV_RANK, -1))
    w_g, w_b, w_o = bf(w_gate), bf(w_branch), bf(w_out)

    for l in range(depth):
        x = _ffn(x, row(ffn1_norm[l]), w13_1, w2_1, gf, l, False)

        qa, ka, vat, qb, kb, vb, qc, kc, vct, kn2 = _mix_in(
            x, row(mix_norm[l]), w_in_x, row(c_q_norm[l]), w_uq_x, row(c_kv_norm[l]), w_kn, w_v,
            cos2, sin2, alibi, l)

        lam_init = jnp.full((1, 1), 0.8 - 0.6 * math.exp(-0.3 * l), F32)
        oa = _attn_a(c_alibi, qa, ka, vat, kn2, a_lambda[l], lam_init, a_subln[l].reshape(-1, 1))
        ob = _attn_b(b_sinks[l], qb, kb, vb, band_bias)
        oc = _attn_c(qc, kc, vct)

        x = _merge(x, row(mix_norm[l]), oa, ob, oc, w_g, row(b_gate[l]), w_b, w_o, l)
        x = _ffn(x, row(ffn2_norm[l]), w13_2, w2_2, gf, l, l == depth - 1)
    return x
```

```python
import functools
import math

import jax
import jax.numpy as jnp
from jax import lax
from jax.experimental import pallas as pl
from jax.experimental.pallas import tpu as pltpu

BF16 = jnp.bfloat16
F32 = jnp.float32

EPS = 1e-6
NEG_INF = -1e30
LOG2E = math.log2(math.e)
ROPE_THETA = 10000.0

A_HEADS = 4
A_QK_DIM = 64
B_Q_HEADS = 8
B_KV_HEADS = 2
B_HEAD_DIM = 64
BAND = 128
B_STEP_BLOCKS = 4
C_HEADS = 4
C_Q_RANK = 384
C_KV_RANK = 256
C_NOPE = 128
C_ROPE = 64
MIX_WIDTH = 512
LANES = 128

VMEM_LIMIT = 52 * 1024 * 1024

TOKEN_TILE = 512
MXU_DIM = 256
FFN_CHUNK = 6 * MXU_DIM
K_TILE = 512
C_K_TILE = 1024
Q_TILE = 2048
A_Q_TILE = K_TILE
V_ROWS = LANES + 16
ACC_ROWS = LANES + 8
ALIBI_TERMS = 3


def _params():
    return pltpu.CompilerParams(vmem_limit_bytes=VMEM_LIMIT)


def _rms(x, g):
    return x * lax.rsqrt(jnp.mean(x * x, axis=-1, keepdims=True) + EPS) * g


def _dot(a, b):
    return jnp.dot(a, b, preferred_element_type=F32)


def _dot_nt(a, b):
    return lax.dot_general(a, b, (((1,), (1,)), ((), ())), preferred_element_type=F32)


def _full(shape):
    return pl.BlockSpec(shape, lambda *_: (0,) * len(shape))


def _layer(stack, l):
    return pl.BlockSpec((None,) + stack.shape[1:], lambda *_: (l,) + (0,) * (stack.ndim - 1),
                        pipeline_mode=pl.Buffered(1))


def _ffn_kernel(x_ref, g_ref, w13_ref, w2_ref, gf_ref, o_ref, *, final):
    x = x_ref[...]
    r = lax.rsqrt(jnp.mean(x * x, axis=-1, keepdims=True) + EPS)
    h = (x * g_ref[...]).astype(BF16)
    d_ff = w2_ref.shape[0]
    acc = jnp.zeros(x.shape, F32)
    for lo in range(0, d_ff, FFN_CHUNK):
        hi = min(lo + FFN_CHUNK, d_ff)
        a = _dot(h, w13_ref[:, lo:hi]) * r
        g = _dot(h, w13_ref[:, d_ff + lo:d_ff + hi]) * r
        act = (a * (1.0 / (1.0 + jnp.exp(-a))) * g).astype(BF16)
        acc = acc + _dot(act, w2_ref[lo:hi, :])
    y = x + 0.5 * acc
    if final:
        y = _rms(y, gf_ref[...])
    o_ref[...] = y


def _ffn(x, g, w13, w2, gf, l, final):
    b, s, d = x.shape
    tm = TOKEN_TILE
    xspec = pl.BlockSpec((None, tm, d), lambda bi, i: (bi, i, 0))
    return pl.pallas_call(
        functools.partial(_ffn_kernel, final=final),
        grid=(b, s // tm),
        in_specs=[xspec, _full(g.shape), _layer(w13, l), _layer(w2, l), _full(gf.shape)],
        out_specs=xspec,
        out_shape=jax.ShapeDtypeStruct(x.shape, F32),
        compiler_params=_params(),
        name="ffn",
    )(x, g, w13, w2, gf)


def _store_values_t(vt_ref, v, heads):
    vt = v.T.astype(BF16)
    ones = jnp.ones((V_ROWS - LANES, vt.shape[1]), BF16)
    for hd in range(heads):
        vt_ref[hd * V_ROWS:hd * V_ROWS + LANES, :] = vt[hd * LANES:(hd + 1) * LANES]
        vt_ref[hd * V_ROWS + LANES:(hd + 1) * V_ROWS, :] = ones


def _mix_in_kernel(x_ref, g_ref, w_ref, cqg_ref, wq_ref, ckvg_ref, wkn_ref, wv_ref,
                   cos_ref, sin_ref, alibi_ref,
                   qa_ref, ka_ref, vat_ref, qb_ref, kb_ref, vbt_ref, qc_ref, kc_ref, vct_ref, kn2_ref):
    x = x_ref[...]
    r = lax.rsqrt(jnp.mean(x * x, axis=-1, keepdims=True) + EPS)
    h = (x * g_ref[...]).astype(BF16)
    cos2 = cos_ref[...]
    sin2 = sin_ref[...]

    zc = _dot(h, w_ref[:, 2560:3328]) * r
    cqn = _rms(zc[:, 0:C_Q_RANK], cqg_ref[...]).astype(BF16)
    ckvn = _rms(zc[:, C_Q_RANK:C_Q_RANK + C_KV_RANK], ckvg_ref[...]).astype(BF16)
    kr = zc[:, C_Q_RANK + C_KV_RANK:]
    kc_ref[:, 512:640] = (kr * cos2 + pltpu.roll(kr, C_ROPE, 1) * sin2).astype(BF16)

    za = _dot(h, w_ref[:, 0:1536]) * r
    qa_ref[...] = (za[:, 0:512] * (A_QK_DIM ** -0.5 * LOG2E)).astype(BF16)
    for hd in range(A_HEADS):
        ka_ref[:, 2 * hd * LANES:(2 * hd + 1) * LANES] = (
            za[:, 512 + hd * LANES:512 + (hd + 1) * LANES].astype(BF16))
        ka_ref[:, (2 * hd + 1) * LANES:(2 * hd + 2) * LANES] = alibi_ref[:, hd * LANES:(hd + 1) * LANES]
    _store_values_t(vat_ref, za[:, 1024:1536], A_HEADS)
    kb16 = za[:, 512:1024].astype(BF16)
    kk = kb16 * kb16
    map_of_col = lax.broadcasted_iota(jnp.int32, (A_HEADS * LANES, LANES), 0) // A_QK_DIM
    ind = (map_of_col == lax.broadcasted_iota(jnp.int32, (A_HEADS * LANES, LANES), 1)).astype(BF16)
    kn2 = jnp.broadcast_to(jnp.max(_dot(kk, ind), axis=0, keepdims=True), kn2_ref.shape)

    c_scale = (C_NOPE + C_ROPE) ** -0.5 * LOG2E
    q = _dot(cqn, wq_ref[...])
    for hd in range(C_HEADS):
        lo = hd * 2 * LANES
        qc_ref[:, lo:lo + LANES] = (q[:, lo:lo + LANES] * c_scale).astype(BF16)
        qp = q[:, lo + LANES:lo + 2 * LANES]
        qpr = qp * cos2 + pltpu.roll(qp, C_ROPE, 1) * sin2
        qc_ref[:, lo + LANES:lo + 2 * LANES] = (qpr * c_scale).astype(BF16)
    kc_ref[:, 0:512] = _dot(ckvn, wkn_ref[...]).astype(BF16)
    _store_values_t(vct_ref, _dot(ckvn, wv_ref[...]), C_HEADS)

    zb = _dot(h, w_ref[:, 1536:2560]) * r
    qb_ref[...] = (zb[:, 0:512] * (B_HEAD_DIM ** -0.5 * LOG2E)).astype(BF16)
    kb_ref[...] = zb[:, 512:768].astype(BF16)
    vbt_ref[...] = zb[:, 768:1024].T.astype(BF16)

    @pl.when(pl.program_id(1) == 0)
    def _():
        kn2_ref[...] = kn2

    @pl.when(pl.program_id(1) > 0)
    def _():
        kn2_ref[...] = jnp.maximum(kn2_ref[...], kn2)


def _mix_in(x, g, w, cqg, wq, ckvg, wkn, wv, cos2, sin2, alibi, l):
    b, s, d = x.shape
    tm = TOKEN_TILE
    row = lambda c: pl.BlockSpec((None, tm, c), lambda bi, i: (bi, i, 0))
    colT = pl.BlockSpec((None, 4 * V_ROWS, tm), lambda bi, i: (bi, 0, i))
    tab = pl.BlockSpec((tm, LANES), lambda bi, i: (i, 0))
    sd = lambda c: jax.ShapeDtypeStruct((b, s, c), BF16)
    sdT = jax.ShapeDtypeStruct((b, 4 * V_ROWS, s), BF16)
    return pl.pallas_call(
        _mix_in_kernel,
        grid=(b, s // tm),
        in_specs=[row(d), _full(g.shape), _layer(w, l), _full(cqg.shape), _layer(wq, l),
                  _full(ckvg.shape), _layer(wkn, l), _layer(wv, l), tab, tab,
                  pl.BlockSpec((tm, alibi.shape[1]), lambda bi, i: (i % (K_TILE // tm), 0))],
        out_specs=[row(512), row(1024), colT, row(512), row(256),
                   pl.BlockSpec((None, 2 * LANES, tm), lambda bi, i: (bi, 0, i)),
                   row(1024), row(640), colT,
                   pl.BlockSpec((None, 8, LANES), lambda bi, i: (bi, 0, 0))],
        out_shape=[sd(512), sd(1024), sdT, sd(512), sd(256),
                   jax.ShapeDtypeStruct((b, 2 * LANES, s), BF16), sd(1024), sd(640), sdT,
                   jax.ShapeDtypeStruct((b, 8, LANES), F32)],
        compiler_params=_params(),
        name="mix_in",
    )(x, g, w, cqg, wq, ckvg, wkn, wv, cos2, sin2, alibi)


OVERFLOW_GUARD = 2.0 ** 60
PROBE_KEYS = 128
SKIP_EXPONENT = 64.0
NORM_MARGIN = 1.02
MASKED_REFERENCE = 1e30
P_BUFFERS = 2


def _value_product(vt, p):
    return _dot(vt, p)[:ACC_ROWS]


def _tile_stats(st, vt):
    m_t = jnp.max(st, axis=0, keepdims=True)
    return m_t, _value_product(vt, jnp.exp2(st - m_t).astype(BF16))


def _flash_fixed_reference(n, qk_fn, v_fn, ref_fn, acc, p_refs):
    def weights(t):
        p_refs[t % len(p_refs)][...] = jnp.exp2(qk_fn(t) - ref_fn(t)).astype(BF16)

    if n:
        weights(0)
    for t in range(n):
        if t + 1 < n:
            weights(t + 1)
        acc = acc + _value_product(v_fn(t), p_refs[t % len(p_refs)][...])
    return acc


def _flash_running_max(n, score_fn, v_fn, width):
    def body(kt, carry):
        m, acc = carry
        m_t, pv_t = _tile_stats(score_fn(kt), v_fn(kt))
        m_new = jnp.maximum(m, m_t)
        return m_new, jnp.exp2(m - m_new) * acc + jnp.exp2(m_t - m_new) * pv_t

    init = (jnp.full((1, width), -jnp.inf, F32), jnp.zeros((ACC_ROWS, width), F32))
    return lax.fori_loop(0, n, body, init)[1]


def _denominator_overflowed(acc):
    l = acc[LANES:LANES + 1]
    return jnp.max(jnp.where(l < OVERFLOW_GUARD, 0.0, 1.0)) > 0.5


def _attn_a_kernel(c_ref, q_ref, k_ref, vt_ref, kn2_ref, lam_ref, lam_init_ref, g_ref, o_ref,
                   qq_ref, sd_ref, acc_ref, *p_refs):
    hd = pl.program_id(1)
    qi = pl.program_id(2)
    tq = q_ref.shape[0]
    assert tq == K_TILE
    n_kt = k_ref.shape[0] // K_TILE
    c = c_ref[hd]

    q = q_ref[...]
    lane = lax.broadcasted_iota(jnp.int32, q.shape, 1)
    zero = jnp.zeros_like(q)
    qm = jnp.concatenate([jnp.where(lane < A_QK_DIM, q, zero),
                          jnp.where(lane >= A_QK_DIM, q, zero)], axis=0)
    sel = (lax.broadcasted_iota(jnp.int32, qm.shape, 1) < ALIBI_TERMS).astype(BF16)
    for slot, sign in enumerate((-1.0, 1.0, 0.0)):
        qq_ref[slot, :, 0:LANES] = qm
        qq_ref[slot, :, LANES:2 * LANES] = sel * sign

    def keys(kt):
        return k_ref[pl.ds(pl.multiple_of(kt * K_TILE, K_TILE), K_TILE), :]

    def values(kt):
        return vt_ref[:, pl.ds(pl.multiple_of(kt * K_TILE, K_TILE), K_TILE)]

    q0 = qi * tq
    rel = (lax.broadcasted_iota(jnp.int32, (K_TILE, tq), 0)
           - lax.broadcasted_iota(jnp.int32, (K_TILE, tq), 1))

    def biased_scores(kt):
        dist = jnp.abs(rel + (kt * K_TILE - q0)).astype(F32) * c
        return _dot_nt(keys(kt), qq_ref[2]) - jnp.concatenate([dist, dist], axis=1)

    def finish(acc):
        lp = lam_ref[...]
        lam_init = lam_init_ref[...]
        lam = (jnp.exp(jnp.sum(lp[0:1] * lp[1:2], axis=1, keepdims=True))
               - jnp.exp(jnp.sum(lp[2:3] * lp[3:4], axis=1, keepdims=True)) + lam_init)
        inv_l = 1.0 / acc[LANES:LANES + 1]
        o = acc[:LANES, :tq] * inv_l[:, :tq] - lam * (acc[:LANES, tq:] * inv_l[:, tq:])
        o = o * lax.rsqrt(jnp.mean(o * o, axis=0, keepdims=True) + EPS) * g_ref[...]
        o = o * (1.0 - lam_init)
        o_ref[...] = o.T.astype(BF16)

    kd = qi
    sd_ref[...] = biased_scores(kd)
    m0 = jnp.max(sd_ref[...], axis=0, keepdims=True)
    q_row = lax.broadcasted_iota(jnp.int32, (1, tq), 1).astype(F32) * c
    q_row = jnp.concatenate([q_row, q_row], axis=1)

    def key_tile(t):
        t = jnp.int32(t - 1)
        kt = t + (t >= kd).astype(jnp.int32)
        return kt, (kt < kd).astype(jnp.int32)

    def qk_fn(t):
        if t == 0:
            return sd_ref[...]
        kt, before = key_tile(t)
        return _dot_nt(keys(kt), qq_ref[before])

    def v_fn(t):
        return values(kd if t == 0 else key_tile(t)[0])

    def ref_fn(t):
        if t == 0:
            return m0
        kt, before = key_tile(t)
        sgn = (2 * before - 1).astype(F32)
        return m0 - ((kt * K_TILE - q0).astype(F32) * c - q_row) * sgn

    lane8 = lax.broadcasted_iota(jnp.int32, kn2_ref.shape, 1)
    qsq = q.astype(F32) * q.astype(F32)
    excess = []
    for m, in_map in enumerate((lane < A_QK_DIM, lane >= A_QK_DIM)):
        k2 = jnp.max(jnp.where(lane8 == 2 * hd + m, kn2_ref[...], 0.0))
        q2 = jnp.max(jnp.sum(jnp.where(in_map, qsq, 0.0), axis=1, keepdims=True))
        score_max = jnp.sqrt(jnp.full((1, LANES), q2 * k2, F32)) * NORM_MARGIN
        excess.append(score_max - jnp.min(m0[:, m * tq:(m + 1) * tq]))
    d = lax.broadcasted_iota(jnp.int32, (1, LANES), 1).astype(F32)
    reach = ((d >= 1.0) & (d <= n_kt - 1.0)
             & (jnp.maximum(*excess) - c * ((d - 1.0) * K_TILE + 1.0) > -SKIP_EXPONENT))
    reach_count = jnp.sum(jnp.where(reach, 1.0, 0.0))
    radius = sum(jnp.where(reach_count > dd - 0.5, 1, 0) for dd in range(1, n_kt))
    n_before = jnp.minimum(kd, radius)
    n_after = jnp.minimum(n_kt - 1 - kd, radius)
    n_near = jnp.minimum(n_before, n_after)
    n_off = n_before + n_after
    ring_limit = n_kt - 4

    @pl.when(n_off > ring_limit)
    def _():
        acc_ref[...] = _flash_fixed_reference(n_kt, qk_fn, v_fn, ref_fn,
                                              jnp.zeros((ACC_ROWS, 2 * tq), F32), p_refs)

    @pl.when(n_off <= ring_limit)
    def _():
        def weights(slot, p_ref):
            both = slot < 2 * n_near
            dist = jnp.where(both, slot // 2 + 1, slot - n_near + 1)
            before = jnp.where(both, 1 - slot % 2, jnp.where(n_before > n_after, 1, 0))
            kt = jnp.clip(jnp.where(before == 1, kd - dist, kd + dist), 0, n_kt - 1)
            past_end = jnp.where(slot < n_off, 0.0, MASKED_REFERENCE)
            sgn = (2 * before - 1).astype(F32)
            ref = m0 - ((kt * K_TILE - q0).astype(F32) * c - q_row) * sgn + past_end
            p_ref[...] = jnp.exp2(_dot_nt(keys(kt), qq_ref[before]) - ref).astype(BF16)
            return kt

        def pair(j, acc):
            kt0 = weights(2 * j, p_refs[0])
            kt1 = weights(2 * j + 1, p_refs[1])
            return (acc + _value_product(values(kt0), p_refs[0][...])
                    + _value_product(values(kt1), p_refs[1][...]))

        acc0 = _value_product(values(kd), jnp.exp2(sd_ref[...] - m0).astype(BF16))
        acc_ref[...] = lax.fori_loop(0, (n_off + 1) // 2, pair, acc0)

    acc = acc_ref[...]
    finish(acc)

    @pl.when(_denominator_overflowed(acc))
    def _():
        finish(_flash_running_max(n_kt, biased_scores, values, 2 * tq))


def _attn_a(c_alibi, qa, ka, vat, kn2, lam_p, lam_init, g_col):
    b, s, _ = qa.shape
    tq = A_Q_TILE
    return pl.pallas_call(
        _attn_a_kernel,
        grid=(b, A_HEADS, s // tq),
        in_specs=[
            pl.BlockSpec(memory_space=pltpu.SMEM),
            pl.BlockSpec((None, tq, LANES), lambda bi, h, i: (bi, i, h)),
            pl.BlockSpec((None, s, 2 * LANES), lambda bi, h, i: (bi, 0, h)),
            pl.BlockSpec((None, V_ROWS, s), lambda bi, h, i: (bi, h, 0)),
            pl.BlockSpec((None,) + kn2.shape[1:], lambda bi, h, i: (bi, 0, 0)),
            _full(lam_p.shape), _full(lam_init.shape), _full(g_col.shape),
        ],
        out_specs=pl.BlockSpec((None, tq, LANES), lambda bi, h, i: (bi, i, h)),
        out_shape=jax.ShapeDtypeStruct((b, s, MIX_WIDTH), BF16),
        scratch_shapes=[pltpu.VMEM((3, 2 * tq, 2 * LANES), BF16), pltpu.VMEM((K_TILE, 2 * tq), F32),
                        pltpu.VMEM((ACC_ROWS, 2 * tq), F32),
                       ] + [pltpu.VMEM((K_TILE, 2 * tq), BF16)] * P_BUFFERS,
        compiler_params=_params(),
        name="attn_a",
    )(c_alibi, qa, ka, vat, kn2, lam_p, lam_init, g_col)


def _attn_c_kernel(q_ref, kn_ref, kpe_ref, vt_ref, o_ref, *p_refs):
    q = q_ref[...]
    tq = q.shape[0]
    tk = C_K_TILE
    n_kt = kn_ref.shape[0] // tk

    def scores(kt):
        k0 = pl.multiple_of(kt * tk, tk)
        k = jnp.concatenate([kn_ref[pl.ds(k0, tk), :], kpe_ref[pl.ds(k0, tk), :]], axis=1)
        return _dot_nt(k, q)

    def values(kt):
        return vt_ref[:, pl.ds(pl.multiple_of(kt * tk, tk), tk)]

    def finish(acc):
        o_ref[...] = (acc[:LANES] * (1.0 / acc[LANES:LANES + 1])).T.astype(BF16)

    probe = jnp.concatenate([kn_ref[0:PROBE_KEYS, :], kpe_ref[0:PROBE_KEYS, :]], axis=1)
    m0 = jnp.max(_dot_nt(probe, q), axis=0, keepdims=True)
    acc = _flash_fixed_reference(n_kt, lambda t: scores(jnp.int32(t)), lambda t: values(jnp.int32(t)),
                                 lambda t: m0, jnp.zeros((ACC_ROWS, tq), F32), p_refs)
    finish(acc)

    @pl.when(_denominator_overflowed(acc))
    def _():
        finish(_flash_running_max(n_kt, scores, values, tq))


def _attn_c(qc, kc, vct):
    b, s, _ = qc.shape
    tq = Q_TILE
    return pl.pallas_call(
        _attn_c_kernel,
        grid=(b, C_HEADS, s // tq),
        in_specs=[
            pl.BlockSpec((None, tq, 2 * LANES), lambda bi, h, i: (bi, i, h)),
            pl.BlockSpec((None, s, LANES), lambda bi, h, i: (bi, 0, h)),
            pl.BlockSpec((None, s, LANES), lambda bi, h, i: (bi, 0, C_HEADS)),
            pl.BlockSpec((None, V_ROWS, s), lambda bi, h, i: (bi, h, 0)),
        ],
        out_specs=pl.BlockSpec((None, tq, LANES), lambda bi, h, i: (bi, i, h)),
        out_shape=jax.ShapeDtypeStruct((b, s, MIX_WIDTH), BF16),
        scratch_shapes=[pltpu.VMEM((C_K_TILE, tq), BF16)] * P_BUFFERS,
        compiler_params=_params(),
        name="attn_c",
    )(qc, kc, kc, vct)


def _attn_b_kernel(sinks_ref, q_ref, kp_ref, ko_ref, kx_ref, vp_ref, vo_ref, vx_ref, bias_ref, o_ref):
    first = pl.program_id(1) * B_STEP_BLOCKS
    nb = pl.num_programs(1) * B_STEP_BLOCKS
    group = B_Q_HEADS // B_KV_HEADS
    half = lax.broadcasted_iota(jnp.int32, (BAND, LANES), 1) // B_HEAD_DIM
    row_half = lax.broadcasted_iota(jnp.int32, (LANES, BAND), 0) // B_HEAD_DIM

    for hk in range(B_KV_HEADS):
        heads = range(hk * group, (hk + 1) * group)
        ksl = slice(hk * LANES, (hk + 1) * LANES)
        k_all = jnp.concatenate([kp_ref[:, ksl], ko_ref[:, ksl], kx_ref[:, ksl]], axis=0)
        vt_all = jnp.concatenate([vp_ref[ksl, :], vo_ref[ksl, :], vx_ref[ksl, :]], axis=1)
        sink = jnp.concatenate([jnp.full((1, BAND), sinks_ref[h] * LOG2E, F32) for h in heads], axis=1)
        for u in range(B_STEP_BLOCKS):
            rows = slice(u * BAND, (u + 1) * BAND)
            qg = []
            for h in heads:
                qh = q_ref[rows, (h // 2) * LANES:(h // 2 + 1) * LANES]
                qg.append(jnp.where(half == h % 2, qh, jnp.zeros_like(qh)))
            bias = bias_ref[hk]
            if u == 0:
                pen = jnp.where(first == 0, NEG_INF, 0.0)
                bias = jnp.concatenate([bias[:BAND] + pen, bias[BAND:]], axis=0)
            if u == B_STEP_BLOCKS - 1:
                pen = jnp.where(first + u == nb - 1, NEG_INF, 0.0)
                bias = jnp.concatenate([bias[:2 * BAND], bias[2 * BAND:] + pen], axis=0)
            sc = _dot_nt(k_all[u * BAND:(u + 3) * BAND], jnp.concatenate(qg, axis=0))
            sc = sc + bias
            m = jnp.maximum(jnp.max(sc, axis=0, keepdims=True), sink)
            e = jnp.exp2(sc - m)
            den = jnp.sum(e, axis=0, keepdims=True) + jnp.exp2(sink - m)
            ot = _dot(vt_all[:, u * BAND:(u + 3) * BAND], e.astype(BF16)) * (1.0 / den)
            for c in range(group // 2):
                pair = jnp.where(row_half == 0, ot[:, 2 * c * BAND:(2 * c + 1) * BAND],
                                 ot[:, (2 * c + 1) * BAND:(2 * c + 2) * BAND])
                col = hk * (group // 2) + c
                o_ref[rows, col * LANES:(col + 1) * LANES] = pair.T.astype(BF16)


def _band_bias_table():
    kj = jnp.arange(3 * BAND)[:, None] - BAND
    dist = jnp.abs(kj - jnp.arange(BAND)[None, :])
    per_head = [jnp.where(dist <= BAND, -(2.0 ** -(h + 1) * LOG2E) * dist.astype(F32), NEG_INF)
                for h in range(B_Q_HEADS)]
    group = B_Q_HEADS // B_KV_HEADS
    return jnp.stack([jnp.concatenate(per_head[g * group:(g + 1) * group], axis=1)
                      for g in range(B_KV_HEADS)])


def _attn_b(sinks, qb, kb, vbt, bias):
    b, s, _ = qb.shape
    nb = s // BAND
    nq = B_STEP_BLOCKS
    rows = nq * BAND
    prev = lambda j: jnp.maximum(j * nq - 1, 0)
    nxt = lambda j: jnp.minimum(j * nq + nq, nb - 1)
    k_edge = lambda f: pl.BlockSpec((None, BAND, 2 * LANES), lambda bi, j: (bi, f(j), 0))
    v_edge = lambda f: pl.BlockSpec((None, 2 * LANES, BAND), lambda bi, j: (bi, 0, f(j)))
    k_own = pl.BlockSpec((None, rows, 2 * LANES), lambda bi, j: (bi, j, 0))
    v_own = pl.BlockSpec((None, 2 * LANES, rows), lambda bi, j: (bi, 0, j))
    return pl.pallas_call(
        _attn_b_kernel,
        grid=(b, nb // nq),
        in_specs=[pl.BlockSpec(memory_space=pltpu.SMEM),
                  pl.BlockSpec((None, rows, MIX_WIDTH), lambda bi, j: (bi, j, 0)),
                  k_edge(prev), k_own, k_edge(nxt), v_edge(prev), v_own, v_edge(nxt),
                  _full(bias.shape)],
        out_specs=pl.BlockSpec((None, rows, MIX_WIDTH), lambda bi, j: (bi, j, 0)),
        out_shape=jax.ShapeDtypeStruct((b, s, MIX_WIDTH), BF16),
        compiler_params=_params(),
        name="attn_b",
    )(sinks, qb, kb, kb, kb, vbt, vbt, vbt, bias)


def _merge_kernel(x_ref, g_ref, oa_ref, ob_ref, oc_ref, wg_ref, bg_ref, wb_ref, wo_ref, o_ref):
    x = x_ref[...]
    d = x.shape[1]
    r = lax.rsqrt(jnp.mean(x * x, axis=-1, keepdims=True) + EPS)
    h = (x * g_ref[...]).astype(BF16)
    merged = jnp.zeros(x.shape, F32)
    for n, br_ref in enumerate((oa_ref, ob_ref, oc_ref)):
        z = _dot(h, wg_ref[:, n * d:(n + 1) * d]) * r + bg_ref[:, n * d:(n + 1) * d]
        gate = 1.0 / (1.0 + jnp.exp(-z))
        merged = merged + gate * _dot(br_ref[...], wb_ref[n])
    o_ref[...] = x + _dot(merged.astype(BF16), wo_ref[...])


def _merge(x, g, oa, ob, oc, wg, bg, wb, wo, l):
    b, s, d = x.shape
    tm = TOKEN_TILE
    row = lambda c: pl.BlockSpec((None, tm, c), lambda bi, i: (bi, i, 0))
    return pl.pallas_call(
        _merge_kernel,
        grid=(b, s // tm),
        in_specs=[row(d), _full(g.shape), row(MIX_WIDTH), row(MIX_WIDTH), row(MIX_WIDTH),
                  _layer(wg, l), _full(bg.shape), _layer(wb, l), _layer(wo, l)],
        out_specs=row(d),
        out_shape=jax.ShapeDtypeStruct(x.shape, F32),
        compiler_params=_params(),
        name="merge",
    )(x, g, oa, ob, oc, wg, bg, wb, wo)


def _prep_w_in(w_in):
    dup = lambda w: jnp.concatenate([w[..., 0:64], w[..., 0:64], w[..., 64:128], w[..., 64:128]], axis=-1)
    kr = w_in[..., 2944:3008]
    return jnp.concatenate([
        w_in[..., 0:2048], dup(w_in[..., 2048:2176]), dup(w_in[..., 2176:2304]), w_in[..., 2304:2944],
        kr, kr[..., 32:64], kr[..., 0:32]], axis=-1).astype(BF16)


def _prep_w_uq(w_uq):
    pe = w_uq[..., C_NOPE:]
    half = C_ROPE // 2
    w = jnp.concatenate([w_uq, pe[..., half:], pe[..., :half]], axis=-1)
    return w.reshape(w_uq.shape[0], w_uq.shape[1], C_HEADS * 2 * LANES).astype(BF16)


def _alibi_key_table(c_alibi):
    term = c_alibi[None, :] * jnp.arange(K_TILE, dtype=F32)[:, None]
    pieces = []
    for _ in range(ALIBI_TERMS):
        piece = term.astype(BF16)
        pieces.append(piece)
        term = term - piece.astype(F32)
    tab = jnp.stack(pieces, axis=-1)
    tab = jnp.pad(tab, ((0, 0), (0, 0), (0, LANES - ALIBI_TERMS)))
    return tab.reshape(K_TILE, -1)


def kernel(x, ffn1_norm, ffn1_w13, ffn1_w2, mix_norm, w_in, w_gate, b_gate, a_lambda, a_subln,
           b_sinks, c_q_norm, c_w_uq, c_kv_norm, c_w_ukv, w_branch, w_out, ffn2_norm, ffn2_w13,
           ffn2_w2, final_norm):
    b, s, d = x.shape
    depth = w_in.shape[0]
    assert s % K_TILE == 0 and s % C_K_TILE == 0 and K_TILE % TOKEN_TILE == 0 and d % LANES == 0

    pos = jnp.arange(s, dtype=F32)
    inv_freq = ROPE_THETA ** (-jnp.arange(0, C_ROPE, 2, dtype=F32) / C_ROPE)
    ang = pos[:, None] * inv_freq[None, :]
    cos, sin = jnp.cos(ang), jnp.sin(ang)
    pad = jnp.zeros((s, LANES - C_ROPE), F32)
    cos2 = jnp.concatenate([cos, cos, pad], axis=1)
    sin2 = jnp.concatenate([-sin, sin, pad], axis=1)
    c_alibi = 2.0 ** (-8.0 * jnp.arange(1, A_HEADS + 1, dtype=F32) / A_HEADS) * LOG2E
    alibi = _alibi_key_table(c_alibi)
    band_bias = _band_bias_table()
    row = lambda v: v.reshape(1, -1)
    gf = row(final_norm)

    bf = lambda w: w.astype(BF16)
    w13_1, w2_1, w13_2, w2_2 = bf(ffn1_w13), bf(ffn1_w2), bf(ffn2_w13), bf(ffn2_w2)
    w_in_x, w_uq_x = _prep_w_in(w_in), _prep_w_uq(c_w_uq)
    w_kn = bf(c_w_ukv[..., :C_NOPE].reshape(depth, C_KV_RANK, -1))
    w_v = bf(c_w_ukv[..., C_NOPE:].reshape(depth, C_KV_RANK, -1))
    w_g, w_b, w_o = bf(w_gate), bf(w_branch), bf(w_out)

    for l in range(depth):
        x = _ffn(x, row(ffn1_norm[l]), w13_1, w2_1, gf, l, False)

        qa, ka, vat, qb, kb, vb, qc, kc, vct, kn2 = _mix_in(
            x, row(mix_norm[l]), w_in_x, row(c_q_norm[l]), w_uq_x, row(c_kv_norm[l]), w_kn, w_v,
            cos2, sin2, alibi, l)

        lam_init = jnp.full((1, 1), 0.8 - 0.6 * math.exp(-0.3 * l), F32)
        oa = _attn_a(c_alibi, qa, ka, vat, kn2, a_lambda[l], lam_init, a_subln[l].reshape(-1, 1))
        ob = _attn_b(b_sinks[l], qb, kb, vb, band_bias)
        oc = _attn_c(qc, kc, vct)

        x = _merge(x, row(mix_norm[l]), oa, ob, oc, w_g, row(b_gate[l]), w_b, w_o, l)
        x = _ffn(x, row(ffn2_norm[l]), w13_2, w2_2, gf, l, l == depth - 1)
    return x
```

```python
import functools
import math

import jax
import jax.numpy as jnp
from jax import lax
from jax.experimental import pallas as pl
from jax.experimental.pallas import tpu as pltpu

BF16 = jnp.bfloat16
F32 = jnp.float32

EPS = 1e-6
NEG_INF = -1e30
LOG2E = math.log2(math.e)
ROPE_THETA = 10000.0

A_HEADS = 4
A_QK_DIM = 64
B_Q_HEADS = 8
B_KV_HEADS = 2
B_HEAD_DIM = 64
BAND = 128
B_STEP_BLOCKS = 4
C_HEADS = 4
C_Q_RANK = 384
C_KV_RANK = 256
C_NOPE = 128
C_ROPE = 64
MIX_WIDTH = 512
LANES = 128

VMEM_LIMIT = 52 * 1024 * 1024

TOKEN_TILE = 512
MXU_DIM = 256
FFN_CHUNK = 6 * MXU_DIM
K_TILE = 512
C_K_TILE = 1024
Q_TILE = 2048
A_Q_TILE = K_TILE
V_ROWS = LANES + 16
ACC_ROWS = LANES + 8
ALIBI_TERMS = 3


def _params(n_operands=0, fusible=()):
    fuse = [i in fusible for i in range(n_operands)] if fusible else None
    return pltpu.CompilerParams(vmem_limit_bytes=VMEM_LIMIT, allow_input_fusion=fuse)


def _rms(x, g):
    return x * lax.rsqrt(jnp.mean(x * x, axis=-1, keepdims=True) + EPS) * g


def _dot(a, b):
    return jnp.dot(a, b, preferred_element_type=F32)


def _dot_nt(a, b):
    return lax.dot_general(a, b, (((1,), (1,)), ((), ())), preferred_element_type=F32)


def _full(shape):
    return pl.BlockSpec(shape, lambda *_: (0,) * len(shape))


def _layer(stack, l):
    return pl.BlockSpec((None,) + stack.shape[1:], lambda *_: (l,) + (0,) * (stack.ndim - 1),
                        pipeline_mode=pl.Buffered(1))


def _ffn_kernel(x_ref, g_ref, w13_ref, w2_ref, gf_ref, o_ref, *, final):
    x = x_ref[...]
    r = lax.rsqrt(jnp.mean(x * x, axis=-1, keepdims=True) + EPS)
    h = (x * g_ref[...]).astype(BF16)
    d_ff = w2_ref.shape[0]
    acc = jnp.zeros(x.shape, F32)
    for lo in range(0, d_ff, FFN_CHUNK):
        hi = min(lo + FFN_CHUNK, d_ff)
        a = _dot(h, w13_ref[:, lo:hi]) * r
        g = _dot(h, w13_ref[:, d_ff + lo:d_ff + hi]) * r
        act = (a * (1.0 / (1.0 + jnp.exp(-a))) * g).astype(BF16)
        acc = acc + _dot(act, w2_ref[lo:hi, :])
    y = x + 0.5 * acc
    if final:
        y = _rms(y, gf_ref[...])
    o_ref[...] = y


def _ffn(x, g, w13, w2, gf, l, final):
    b, s, d = x.shape
    tm = TOKEN_TILE
    xspec = pl.BlockSpec((None, tm, d), lambda bi, i: (bi, i, 0))
    return pl.pallas_call(
        functools.partial(_ffn_kernel, final=final),
        grid=(b, s // tm),
        in_specs=[xspec, _full(g.shape), _layer(w13, l), _layer(w2, l), _full(gf.shape)],
        out_specs=xspec,
        out_shape=jax.ShapeDtypeStruct(x.shape, F32),
        compiler_params=_params(5, (2, 3)),
        name="ffn",
    )(x, g, w13, w2, gf)


def _store_values_t(vt_ref, v, heads):
    vt = v.T.astype(BF16)
    ones = jnp.ones((V_ROWS - LANES, vt.shape[1]), BF16)
    for hd in range(heads):
        vt_ref[hd * V_ROWS:hd * V_ROWS + LANES, :] = vt[hd * LANES:(hd + 1) * LANES]
        vt_ref[hd * V_ROWS + LANES:(hd + 1) * V_ROWS, :] = ones


def _mix_in_kernel(x_ref, g_ref, w_ref, cqg_ref, wq_ref, ckvg_ref, wkn_ref, wv_ref,
                   cos_ref, sin_ref, alibi_ref,
                   qa_ref, ka_ref, vat_ref, qb_ref, kb_ref, vbt_ref, qc_ref, kc_ref, vct_ref, kn2_ref):
    x = x_ref[...]
    r = lax.rsqrt(jnp.mean(x * x, axis=-1, keepdims=True) + EPS)
    h = (x * g_ref[...]).astype(BF16)
    cos2 = cos_ref[...]
    sin2 = sin_ref[...]

    zc = _dot(h, w_ref[:, 2560:3328]) * r
    cqn = _rms(zc[:, 0:C_Q_RANK], cqg_ref[...]).astype(BF16)
    ckvn = _rms(zc[:, C_Q_RANK:C_Q_RANK + C_KV_RANK], ckvg_ref[...]).astype(BF16)
    kr = zc[:, C_Q_RANK + C_KV_RANK:]
    kc_ref[:, 512:640] = (kr * cos2 + pltpu.roll(kr, C_ROPE, 1) * sin2).astype(BF16)

    za = _dot(h, w_ref[:, 0:1536]) * r
    qa_ref[...] = (za[:, 0:512] * (A_QK_DIM ** -0.5 * LOG2E)).astype(BF16)
    for hd in range(A_HEADS):
        ka_ref[:, 2 * hd * LANES:(2 * hd + 1) * LANES] = (
            za[:, 512 + hd * LANES:512 + (hd + 1) * LANES].astype(BF16))
        ka_ref[:, (2 * hd + 1) * LANES:(2 * hd + 2) * LANES] = alibi_ref[:, hd * LANES:(hd + 1) * LANES]
    _store_values_t(vat_ref, za[:, 1024:1536], A_HEADS)
    kb16 = za[:, 512:1024].astype(BF16)
    kk = kb16 * kb16
    map_of_col = lax.broadcasted_iota(jnp.int32, (A_HEADS * LANES, LANES), 0) // A_QK_DIM
    ind = (map_of_col == lax.broadcasted_iota(jnp.int32, (A_HEADS * LANES, LANES), 1)).astype(BF16)
    kn2 = jnp.broadcast_to(jnp.max(_dot(kk, ind), axis=0, keepdims=True), kn2_ref.shape)

    c_scale = (C_NOPE + C_ROPE) ** -0.5 * LOG2E
    q = _dot(cqn, wq_ref[...])
    for hd in range(C_HEADS):
        lo = hd * 2 * LANES
        qc_ref[:, lo:lo + LANES] = (q[:, lo:lo + LANES] * c_scale).astype(BF16)
        qp = q[:, lo + LANES:lo + 2 * LANES]
        qpr = qp * cos2 + pltpu.roll(qp, C_ROPE, 1) * sin2
        qc_ref[:, lo + LANES:lo + 2 * LANES] = (qpr * c_scale).astype(BF16)
    kc_ref[:, 0:512] = _dot(ckvn, wkn_ref[...]).astype(BF16)
    _store_values_t(vct_ref, _dot(ckvn, wv_ref[...]), C_HEADS)

    zb = _dot(h, w_ref[:, 1536:2560]) * r
    qb_ref[...] = (zb[:, 0:512] * (B_HEAD_DIM ** -0.5 * LOG2E)).astype(BF16)
    kb_ref[...] = zb[:, 512:768].astype(BF16)
    vbt_ref[...] = zb[:, 768:1024].T.astype(BF16)

    @pl.when(pl.program_id(1) == 0)
    def _():
        kn2_ref[...] = kn2

    @pl.when(pl.program_id(1) > 0)
    def _():
        kn2_ref[...] = jnp.maximum(kn2_ref[...], kn2)


def _mix_in(x, g, w, cqg, wq, ckvg, wkn, wv, cos2, sin2, alibi, l):
    b, s, d = x.shape
    tm = TOKEN_TILE
    row = lambda c: pl.BlockSpec((None, tm, c), lambda bi, i: (bi, i, 0))
    colT = pl.BlockSpec((None, 4 * V_ROWS, tm), lambda bi, i: (bi, 0, i))
    tab = pl.BlockSpec((tm, LANES), lambda bi, i: (i, 0))
    sd = lambda c: jax.ShapeDtypeStruct((b, s, c), BF16)
    sdT = jax.ShapeDtypeStruct((b, 4 * V_ROWS, s), BF16)
    return pl.pallas_call(
        _mix_in_kernel,
        grid=(b, s // tm),
        in_specs=[row(d), _full(g.shape), _layer(w, l), _full(cqg.shape), _layer(wq, l),
                  _full(ckvg.shape), _layer(wkn, l), _layer(wv, l), tab, tab,
                  pl.BlockSpec((tm, alibi.shape[1]), lambda bi, i: (i % (K_TILE // tm), 0))],
        out_specs=[row(512), row(1024), colT, row(512), row(256),
                   pl.BlockSpec((None, 2 * LANES, tm), lambda bi, i: (bi, 0, i)),
                   row(1024), row(640), colT,
                   pl.BlockSpec((None, 8, LANES), lambda bi, i: (bi, 0, 0))],
        out_shape=[sd(512), sd(1024), sdT, sd(512), sd(256),
                   jax.ShapeDtypeStruct((b, 2 * LANES, s), BF16), sd(1024), sd(640), sdT,
                   jax.ShapeDtypeStruct((b, 8, LANES), F32)],
        compiler_params=_params(11, (2, 4, 6, 7)),
        name="mix_in",
    )(x, g, w, cqg, wq, ckvg, wkn, wv, cos2, sin2, alibi)


OVERFLOW_GUARD = 2.0 ** 60
PROBE_KEYS = 128
SKIP_EXPONENT = 64.0
NORM_MARGIN = 1.02
MASKED_REFERENCE = 1e30
P_BUFFERS = 2


def _value_product(vt, p):
    return _dot(vt, p)[:ACC_ROWS]


def _tile_stats(st, vt):
    m_t = jnp.max(st, axis=0, keepdims=True)
    return m_t, _value_product(vt, jnp.exp2(st - m_t).astype(BF16))


def _flash_fixed_reference(n, qk_fn, v_fn, ref_fn, acc, p_refs):
    def weights(t):
        p_refs[t % len(p_refs)][...] = jnp.exp2(qk_fn(t) - ref_fn(t)).astype(BF16)

    if n:
        weights(0)
    for t in range(n):
        if t + 1 < n:
            weights(t + 1)
        acc = acc + _value_product(v_fn(t), p_refs[t % len(p_refs)][...])
    return acc


def _flash_running_max(n, score_fn, v_fn, width):
    def body(kt, carry):
        m, acc = carry
        m_t, pv_t = _tile_stats(score_fn(kt), v_fn(kt))
        m_new = jnp.maximum(m, m_t)
        return m_new, jnp.exp2(m - m_new) * acc + jnp.exp2(m_t - m_new) * pv_t

    init = (jnp.full((1, width), -jnp.inf, F32), jnp.zeros((ACC_ROWS, width), F32))
    return lax.fori_loop(0, n, body, init)[1]


def _denominator_overflowed(acc):
    l = acc[LANES:LANES + 1]
    return jnp.max(jnp.where(l < OVERFLOW_GUARD, 0.0, 1.0)) > 0.5


def _attn_a_kernel(c_ref, q_ref, k_ref, vt_ref, kn2_ref, lam_ref, lam_init_ref, g_ref, o_ref,
                   qq_ref, sd_ref, acc_ref, *p_refs):
    hd = pl.program_id(1)
    qi = pl.program_id(2)
    tq = q_ref.shape[0]
    assert tq == K_TILE
    n_kt = k_ref.shape[0] // K_TILE
    c = c_ref[hd]

    q = q_ref[...]
    lane = lax.broadcasted_iota(jnp.int32, q.shape, 1)
    zero = jnp.zeros_like(q)
    qm = jnp.concatenate([jnp.where(lane < A_QK_DIM, q, zero),
                          jnp.where(lane >= A_QK_DIM, q, zero)], axis=0)
    sel = (lax.broadcasted_iota(jnp.int32, qm.shape, 1) < ALIBI_TERMS).astype(BF16)
    for slot, sign in enumerate((-1.0, 1.0, 0.0)):
        qq_ref[slot, :, 0:LANES] = qm
        qq_ref[slot, :, LANES:2 * LANES] = sel * sign

    def keys(kt):
        return k_ref[pl.ds(pl.multiple_of(kt * K_TILE, K_TILE), K_TILE), :]

    def values(kt):
        return vt_ref[:, pl.ds(pl.multiple_of(kt * K_TILE, K_TILE), K_TILE)]

    q0 = qi * tq
    rel = (lax.broadcasted_iota(jnp.int32, (K_TILE, tq), 0)
           - lax.broadcasted_iota(jnp.int32, (K_TILE, tq), 1))

    def biased_scores(kt):
        dist = jnp.abs(rel + (kt * K_TILE - q0)).astype(F32) * c
        return _dot_nt(keys(kt), qq_ref[2]) - jnp.concatenate([dist, dist], axis=1)

    def finish(acc):
        lp = lam_ref[...]
        lam_init = lam_init_ref[...]
        lam = (jnp.exp(jnp.sum(lp[0:1] * lp[1:2], axis=1, keepdims=True))
               - jnp.exp(jnp.sum(lp[2:3] * lp[3:4], axis=1, keepdims=True)) + lam_init)
        inv_l = 1.0 / acc[LANES:LANES + 1]
        o = acc[:LANES, :tq] * inv_l[:, :tq] - lam * (acc[:LANES, tq:] * inv_l[:, tq:])
        o = o * lax.rsqrt(jnp.mean(o * o, axis=0, keepdims=True) + EPS) * g_ref[...]
        o = o * (1.0 - lam_init)
        o_ref[...] = o.T.astype(BF16)

    kd = qi
    sd_ref[...] = biased_scores(kd)
    m0 = jnp.max(sd_ref[...], axis=0, keepdims=True)
    q_row = lax.broadcasted_iota(jnp.int32, (1, tq), 1).astype(F32) * c
    q_row = jnp.concatenate([q_row, q_row], axis=1)

    def key_tile(t):
        t = jnp.int32(t - 1)
        kt = t + (t >= kd).astype(jnp.int32)
        return kt, (kt < kd).astype(jnp.int32)

    def qk_fn(t):
        if t == 0:
            return sd_ref[...]
        kt, before = key_tile(t)
        return _dot_nt(keys(kt), qq_ref[before])

    def v_fn(t):
        return values(kd if t == 0 else key_tile(t)[0])

    def ref_fn(t):
        if t == 0:
            return m0
        kt, before = key_tile(t)
        sgn = (2 * before - 1).astype(F32)
        return m0 - ((kt * K_TILE - q0).astype(F32) * c - q_row) * sgn

    lane8 = lax.broadcasted_iota(jnp.int32, kn2_ref.shape, 1)
    qsq = q.astype(F32) * q.astype(F32)
    excess = []
    for m, in_map in enumerate((lane < A_QK_DIM, lane >= A_QK_DIM)):
        k2 = jnp.max(jnp.where(lane8 == 2 * hd + m, kn2_ref[...], 0.0))
        q2 = jnp.max(jnp.sum(jnp.where(in_map, qsq, 0.0), axis=1, keepdims=True))
        score_max = jnp.sqrt(jnp.full((1, LANES), q2 * k2, F32)) * NORM_MARGIN
        excess.append(score_max - jnp.min(m0[:, m * tq:(m + 1) * tq]))
    d = lax.broadcasted_iota(jnp.int32, (1, LANES), 1).astype(F32)
    reach = ((d >= 1.0) & (d <= n_kt - 1.0)
             & (jnp.maximum(*excess) - c * ((d - 1.0) * K_TILE + 1.0) > -SKIP_EXPONENT))
    reach_count = jnp.sum(jnp.where(reach, 1.0, 0.0))
    radius = sum(jnp.where(reach_count > dd - 0.5, 1, 0) for dd in range(1, n_kt))
    n_before = jnp.minimum(kd, radius)
    n_after = jnp.minimum(n_kt - 1 - kd, radius)
    n_near = jnp.minimum(n_before, n_after)
    n_off = n_before + n_after
    ring_limit = n_kt - 4

    @pl.when(n_off > ring_limit)
    def _():
        acc_ref[...] = _flash_fixed_reference(n_kt, qk_fn, v_fn, ref_fn,
                                              jnp.zeros((ACC_ROWS, 2 * tq), F32), p_refs)

    @pl.when(n_off <= ring_limit)
    def _():
        def weights(slot, p_ref):
            both = slot < 2 * n_near
            dist = jnp.where(both, slot // 2 + 1, slot - n_near + 1)
            before = jnp.where(both, 1 - slot % 2, jnp.where(n_before > n_after, 1, 0))
            kt = jnp.clip(jnp.where(before == 1, kd - dist, kd + dist), 0, n_kt - 1)
            past_end = jnp.where(slot < n_off, 0.0, MASKED_REFERENCE)
            sgn = (2 * before - 1).astype(F32)
            ref = m0 - ((kt * K_TILE - q0).astype(F32) * c - q_row) * sgn + past_end
            p_ref[...] = jnp.exp2(_dot_nt(keys(kt), qq_ref[before]) - ref).astype(BF16)
            return kt

        def pair(j, acc):
            kt0 = weights(2 * j, p_refs[0])
            kt1 = weights(2 * j + 1, p_refs[1])
            return (acc + _value_product(values(kt0), p_refs[0][...])
                    + _value_product(values(kt1), p_refs[1][...]))

        acc0 = _value_product(values(kd), jnp.exp2(sd_ref[...] - m0).astype(BF16))
        acc_ref[...] = lax.fori_loop(0, (n_off + 1) // 2, pair, acc0)

    acc = acc_ref[...]
    finish(acc)

    @pl.when(_denominator_overflowed(acc))
    def _():
        finish(_flash_running_max(n_kt, biased_scores, values, 2 * tq))


def _attn_a(c_alibi, qa, ka, vat, kn2, lam_p, lam_init, g_col):
    b, s, _ = qa.shape
    tq = A_Q_TILE
    return pl.pallas_call(
        _attn_a_kernel,
        grid=(b, A_HEADS, s // tq),
        in_specs=[
            pl.BlockSpec(memory_space=pltpu.SMEM),
            pl.BlockSpec((None, tq, LANES), lambda bi, h, i: (bi, i, h)),
            pl.BlockSpec((None, s, 2 * LANES), lambda bi, h, i: (bi, 0, h)),
            pl.BlockSpec((None, V_ROWS, s), lambda bi, h, i: (bi, h, 0)),
            pl.BlockSpec((None,) + kn2.shape[1:], lambda bi, h, i: (bi, 0, 0)),
            _full(lam_p.shape), _full(lam_init.shape), _full(g_col.shape),
        ],
        out_specs=pl.BlockSpec((None, tq, LANES), lambda bi, h, i: (bi, i, h)),
        out_shape=jax.ShapeDtypeStruct((b, s, MIX_WIDTH), BF16),
        scratch_shapes=[pltpu.VMEM((3, 2 * tq, 2 * LANES), BF16), pltpu.VMEM((K_TILE, 2 * tq), F32),
                        pltpu.VMEM((ACC_ROWS, 2 * tq), F32),
                       ] + [pltpu.VMEM((K_TILE, 2 * tq), BF16)] * P_BUFFERS,
        compiler_params=_params(),
        name="attn_a",
    )(c_alibi, qa, ka, vat, kn2, lam_p, lam_init, g_col)


def _attn_c_kernel(q_ref, kn_ref, kpe_ref, vt_ref, o_ref, *p_refs):
    q = q_ref[...]
    tq = q.shape[0]
    tk = C_K_TILE
    n_kt = kn_ref.shape[0] // tk

    def scores(kt):
        k0 = pl.multiple_of(kt * tk, tk)
        k = jnp.concatenate([kn_ref[pl.ds(k0, tk), :], kpe_ref[pl.ds(k0, tk), :]], axis=1)
        return _dot_nt(k, q)

    def values(kt):
        return vt_ref[:, pl.ds(pl.multiple_of(kt * tk, tk), tk)]

    def finish(acc):
        o_ref[...] = (acc[:LANES] * (1.0 / acc[LANES:LANES + 1])).T.astype(BF16)

    probe = jnp.concatenate([kn_ref[0:PROBE_KEYS, :], kpe_ref[0:PROBE_KEYS, :]], axis=1)
    m0 = jnp.max(_dot_nt(probe, q), axis=0, keepdims=True)
    acc = _flash_fixed_reference(n_kt, lambda t: scores(jnp.int32(t)), lambda t: values(jnp.int32(t)),
                                 lambda t: m0, jnp.zeros((ACC_ROWS, tq), F32), p_refs)
    finish(acc)

    @pl.when(_denominator_overflowed(acc))
    def _():
        finish(_flash_running_max(n_kt, scores, values, tq))


def _attn_c(qc, kc, vct):
    b, s, _ = qc.shape
    tq = Q_TILE
    return pl.pallas_call(
        _attn_c_kernel,
        grid=(b, C_HEADS, s // tq),
        in_specs=[
            pl.BlockSpec((None, tq, 2 * LANES), lambda bi, h, i: (bi, i, h)),
            pl.BlockSpec((None, s, LANES), lambda bi, h, i: (bi, 0, h)),
            pl.BlockSpec((None, s, LANES), lambda bi, h, i: (bi, 0, C_HEADS)),
            pl.BlockSpec((None, V_ROWS, s), lambda bi, h, i: (bi, h, 0)),
        ],
        out_specs=pl.BlockSpec((None, tq, LANES), lambda bi, h, i: (bi, i, h)),
        out_shape=jax.ShapeDtypeStruct((b, s, MIX_WIDTH), BF16),
        scratch_shapes=[pltpu.VMEM((C_K_TILE, tq), BF16)] * P_BUFFERS,
        compiler_params=_params(),
        name="attn_c",
    )(qc, kc, kc, vct)


def _attn_b_kernel(sinks_ref, q_ref, kp_ref, ko_ref, kx_ref, vp_ref, vo_ref, vx_ref, bias_ref, o_ref):
    first = pl.program_id(1) * B_STEP_BLOCKS
    nb = pl.num_programs(1) * B_STEP_BLOCKS
    group = B_Q_HEADS // B_KV_HEADS
    half = lax.broadcasted_iota(jnp.int32, (BAND, LANES), 1) // B_HEAD_DIM
    row_half = lax.broadcasted_iota(jnp.int32, (LANES, BAND), 0) // B_HEAD_DIM

    for hk in range(B_KV_HEADS):
        heads = range(hk * group, (hk + 1) * group)
        ksl = slice(hk * LANES, (hk + 1) * LANES)
        k_all = jnp.concatenate([kp_ref[:, ksl], ko_ref[:, ksl], kx_ref[:, ksl]], axis=0)
        vt_all = jnp.concatenate([vp_ref[ksl, :], vo_ref[ksl, :], vx_ref[ksl, :]], axis=1)
        sink = jnp.concatenate([jnp.full((1, BAND), sinks_ref[h] * LOG2E, F32) for h in heads], axis=1)
        for u in range(B_STEP_BLOCKS):
            rows = slice(u * BAND, (u + 1) * BAND)
            qg = []
            for h in heads:
                qh = q_ref[rows, (h // 2) * LANES:(h // 2 + 1) * LANES]
                qg.append(jnp.where(half == h % 2, qh, jnp.zeros_like(qh)))
            bias = bias_ref[hk]
            if u == 0:
                pen = jnp.where(first == 0, NEG_INF, 0.0)
                bias = jnp.concatenate([bias[:BAND] + pen, bias[BAND:]], axis=0)
            if u == B_STEP_BLOCKS - 1:
                pen = jnp.where(first + u == nb - 1, NEG_INF, 0.0)
                bias = jnp.concatenate([bias[:2 * BAND], bias[2 * BAND:] + pen], axis=0)
            sc = _dot_nt(k_all[u * BAND:(u + 3) * BAND], jnp.concatenate(qg, axis=0))
            sc = sc + bias
            m = jnp.maximum(jnp.max(sc, axis=0, keepdims=True), sink)
            e = jnp.exp2(sc - m)
            den = jnp.sum(e, axis=0, keepdims=True) + jnp.exp2(sink - m)
            ot = _dot(vt_all[:, u * BAND:(u + 3) * BAND], e.astype(BF16)) * (1.0 / den)
            for c in range(group // 2):
                pair = jnp.where(row_half == 0, ot[:, 2 * c * BAND:(2 * c + 1) * BAND],
                                 ot[:, (2 * c + 1) * BAND:(2 * c + 2) * BAND])
                col = hk * (group // 2) + c
                o_ref[rows, col * LANES:(col + 1) * LANES] = pair.T.astype(BF16)


def _band_bias_table():
    kj = jnp.arange(3 * BAND)[:, None] - BAND
    dist = jnp.abs(kj - jnp.arange(BAND)[None, :])
    per_head = [jnp.where(dist <= BAND, -(2.0 ** -(h + 1) * LOG2E) * dist.astype(F32), NEG_INF)
                for h in range(B_Q_HEADS)]
    group = B_Q_HEADS // B_KV_HEADS
    return jnp.stack([jnp.concatenate(per_head[g * group:(g + 1) * group], axis=1)
                      for g in range(B_KV_HEADS)])


def _attn_b(sinks, qb, kb, vbt, bias):
    b, s, _ = qb.shape
    nb = s // BAND
    nq = B_STEP_BLOCKS
    rows = nq * BAND
    prev = lambda j: jnp.maximum(j * nq - 1, 0)
    nxt = lambda j: jnp.minimum(j * nq + nq, nb - 1)
    k_edge = lambda f: pl.BlockSpec((None, BAND, 2 * LANES), lambda bi, j: (bi, f(j), 0))
    v_edge = lambda f: pl.BlockSpec((None, 2 * LANES, BAND), lambda bi, j: (bi, 0, f(j)))
    k_own = pl.BlockSpec((None, rows, 2 * LANES), lambda bi, j: (bi, j, 0))
    v_own = pl.BlockSpec((None, 2 * LANES, rows), lambda bi, j: (bi, 0, j))
    return pl.pallas_call(
        _attn_b_kernel,
        grid=(b, nb // nq),
        in_specs=[pl.BlockSpec(memory_space=pltpu.SMEM),
                  pl.BlockSpec((None, rows, MIX_WIDTH), lambda bi, j: (bi, j, 0)),
                  k_edge(prev), k_own, k_edge(nxt), v_edge(prev), v_own, v_edge(nxt),
                  _full(bias.shape)],
        out_specs=pl.BlockSpec((None, rows, MIX_WIDTH), lambda bi, j: (bi, j, 0)),
        out_shape=jax.ShapeDtypeStruct((b, s, MIX_WIDTH), BF16),
        compiler_params=_params(),
        name="attn_b",
    )(sinks, qb, kb, kb, kb, vbt, vbt, vbt, bias)


def _merge_kernel(x_ref, g_ref, oa_ref, ob_ref, oc_ref, wg_ref, bg_ref, wb_ref, wo_ref, o_ref):
    x = x_ref[...]
    d = x.shape[1]
    r = lax.rsqrt(jnp.mean(x * x, axis=-1, keepdims=True) + EPS)
    h = (x * g_ref[...]).astype(BF16)
    merged = jnp.zeros(x.shape, F32)
    for n, br_ref in enumerate((oa_ref, ob_ref, oc_ref)):
        z = _dot(h, wg_ref[:, n * d:(n + 1) * d]) * r + bg_ref[:, n * d:(n + 1) * d]
        gate = 1.0 / (1.0 + jnp.exp(-z))
        merged = merged + gate * _dot(br_ref[...], wb_ref[n])
    o_ref[...] = x + _dot(merged.astype(BF16), wo_ref[...])


def _merge(x, g, oa, ob, oc, wg, bg, wb, wo, l):
    b, s, d = x.shape
    tm = TOKEN_TILE
    row = lambda c: pl.BlockSpec((None, tm, c), lambda bi, i: (bi, i, 0))
    return pl.pallas_call(
        _merge_kernel,
        grid=(b, s // tm),
        in_specs=[row(d), _full(g.shape), row(MIX_WIDTH), row(MIX_WIDTH), row(MIX_WIDTH),
                  _layer(wg, l), _full(bg.shape), _layer(wb, l), _layer(wo, l)],
        out_specs=row(d),
        out_shape=jax.ShapeDtypeStruct(x.shape, F32),
        compiler_params=_params(9, (5, 7, 8)),
        name="merge",
    )(x, g, oa, ob, oc, wg, bg, wb, wo)


def _prep_w_in(w_in):
    dup = lambda w: jnp.concatenate([w[..., 0:64], w[..., 0:64], w[..., 64:128], w[..., 64:128]], axis=-1)
    kr = w_in[..., 2944:3008]
    return jnp.concatenate([
        w_in[..., 0:2048], dup(w_in[..., 2048:2176]), dup(w_in[..., 2176:2304]), w_in[..., 2304:2944],
        kr, kr[..., 32:64], kr[..., 0:32]], axis=-1).astype(BF16)


def _prep_w_uq(w_uq):
    pe = w_uq[..., C_NOPE:]
    half = C_ROPE // 2
    w = jnp.concatenate([w_uq, pe[..., half:], pe[..., :half]], axis=-1)
    return w.reshape(w_uq.shape[0], w_uq.shape[1], C_HEADS * 2 * LANES).astype(BF16)


def _alibi_key_table(c_alibi):
    term = c_alibi[None, :] * jnp.arange(K_TILE, dtype=F32)[:, None]
    pieces = []
    for _ in range(ALIBI_TERMS):
        piece = term.astype(BF16)
        pieces.append(piece)
        term = term - piece.astype(F32)
    tab = jnp.stack(pieces, axis=-1)
    tab = jnp.pad(tab, ((0, 0), (0, 0), (0, LANES - ALIBI_TERMS)))
    return tab.reshape(K_TILE, -1)


def kernel(x, ffn1_norm, ffn1_w13, ffn1_w2, mix_norm, w_in, w_gate, b_gate, a_lambda, a_subln,
           b_sinks, c_q_norm, c_w_uq, c_kv_norm, c_w_ukv, w_branch, w_out, ffn2_norm, ffn2_w13,
           ffn2_w2, final_norm):
    b, s, d = x.shape
    depth = w_in.shape[0]
    assert s % K_TILE == 0 and s % C_K_TILE == 0 and K_TILE % TOKEN_TILE == 0 and d % LANES == 0

    pos = jnp.arange(s, dtype=F32)
    inv_freq = ROPE_THETA ** (-jnp.arange(0, C_ROPE, 2, dtype=F32) / C_ROPE)
    ang = pos[:, None] * inv_freq[None, :]
    cos, sin = jnp.cos(ang), jnp.sin(ang)
    pad = jnp.zeros((s, LANES - C_ROPE), F32)
    cos2 = jnp.concatenate([cos, cos, pad], axis=1)
    sin2 = jnp.concatenate([-sin, sin, pad], axis=1)
    c_alibi = 2.0 ** (-8.0 * jnp.arange(1, A_HEADS + 1, dtype=F32) / A_HEADS) * LOG2E
    alibi = _alibi_key_table(c_alibi)
    band_bias = _band_bias_table()
    row = lambda v: v.reshape(1, -1)
    gf = row(final_norm)

    bf = lambda w: w.astype(BF16)
    w13_1, w2_1, w13_2, w2_2 = bf(ffn1_w13), bf(ffn1_w2), bf(ffn2_w13), bf(ffn2_w2)
    w_in_x, w_uq_x = _prep_w_in(w_in), _prep_w_uq(c_w_uq)
    w_kn = bf(c_w_ukv[..., :C_NOPE].reshape(depth, C_KV_RANK, -1))
    w_v = bf(c_w_ukv[..., C_NOPE:].reshape(depth, C_KV_RANK, -1))
    w_g, w_b, w_o = bf(w_gate), bf(w_branch), bf(w_out)

    for l in range(depth):
        x = _ffn(x, row(ffn1_norm[l]), w13_1, w2_1, gf, l, False)

        qa, ka, vat, qb, kb, vb, qc, kc, vct, kn2 = _mix_in(
            x, row(mix_norm[l]), w_in_x, row(c_q_norm[l]), w_uq_x, row(c_kv_norm[l]), w_kn, w_v,
            cos2, sin2, alibi, l)

        lam_init = jnp.full((1, 1), 0.8 - 0.6 * math.exp(-0.3 * l), F32)
        oa = _attn_a(c_alibi, qa, ka, vat, kn2, a_lambda[l], lam_init, a_subln[l].reshape(-1, 1))
        ob = _attn_b(b_sinks[l], qb, kb, vb, band_bias)
        oc = _attn_c(qc, kc, vct)

        x = _merge(x, row(mix_norm[l]), oa, ob, oc, w_g, row(b_gate[l]), w_b, w_o, l)
        x = _ffn(x, row(ffn2_norm[l]), w13_2, w2_2, gf, l, l == depth - 1)
    return x
```
